```python
import math
import jax
import jax.numpy as jnp
from jax import lax
import numpy as np

D_MODEL = 2048
BATCH = 4
SEQ = 2048
DEPTH = 4
DEC_BATCH = 128
DEC_SEQ = 4
PAST_LEN = 16384
PAGE_SIZE = 128

RET_HEADS = 4
RET_DK = 128
RET_DV = 256
RET_CHUNK = 64
ROPE_BASE = 10000.0
GDN_HEADS = 8
GDN_DK = 128
GDN_DV = 128
GDN_CONV = 4
GDN_CHUNK = 64
GDN_QKV = GDN_HEADS * (2 * GDN_DK + GDN_DV)
GLA_HEADS = 4
GLA_DK = 128
GLA_DV = 256
GLA_LOWRANK = 16
GLA_GATE_NORM = 16.0
GLA_CHUNK = 16
N_BRANCH = 3
BRANCH_WIDTH = RET_HEADS * RET_DV
D_FF = -(-8 * D_MODEL // (3 * 256)) * 256
EPS = 1e-6
STATE_SCALE = 0.5

IN_SIZES = [RET_HEADS * RET_DK, RET_HEADS * RET_DK, RET_HEADS * RET_DV, RET_HEADS * RET_DV,
            GDN_QKV, GDN_HEADS, GDN_HEADS, GDN_HEADS * GDN_DV,
            GLA_HEADS * GLA_DK, GLA_HEADS * GLA_DK, GLA_HEADS * GLA_DV, GLA_LOWRANK, GLA_HEADS * GLA_DV,
            N_BRANCH * D_MODEL]
IN_OFFSETS = [sum(IN_SIZES[:i]) for i in range(1, len(IN_SIZES))]
D_IN = sum(IN_SIZES)

kernel_name = "hybrid_retention_gdn_gla_gated_merge_step"


def rmsnorm(x, g=None):
    xf = x.astype(jnp.float32)
    y = xf * lax.rsqrt(jnp.mean(xf * xf, axis=-1, keepdims=True) + EPS)
    if g is not None:
        y = y * g.astype(jnp.float32)
    return y.astype(x.dtype)


def l2norm(t):
    return t * lax.rsqrt(jnp.sum(t * t, axis=-1, keepdims=True) + EPS)


def heads(t, n):
    return t.reshape(t.shape[:2] + (n, t.shape[-1] // n))


def to_chunks(t, c):
    b, l = t.shape[:2]
    t = t.reshape((b, l // c, c) + t.shape[2:])
    return jnp.transpose(t, (1, 0, 3, 2) + tuple(range(4, t.ndim)))


def from_chunks(o):
    nc, b, h, c, d = o.shape
    return jnp.transpose(o, (1, 0, 3, 2, 4)).reshape(b, nc * c, h, d)


def rotary(t, pos0):
    l, d = t.shape[1], t.shape[-1]
    inv = 1.0 / (ROPE_BASE ** jnp.linspace(0.0, 1.0, d // 2, dtype=jnp.float32))
    ang = (jnp.arange(l, dtype=jnp.float32) + pos0)[:, None] * inv[None, :]
    cos = jnp.cos(ang)[None, :, None, :]
    sin = jnp.sin(ang)[None, :, None, :]
    t1, t2 = t[..., 0::2], t[..., 1::2]
    return jnp.stack([t1 * cos - t2 * sin, t1 * sin + t2 * cos], axis=-1).reshape(t.shape)


def retention(q, k, v, state0, pos0):
    l = q.shape[1]
    c = math.gcd(l, RET_CHUNK)
    q = rotary(q.astype(jnp.float32), pos0)
    k = rotary(k.astype(jnp.float32), pos0) * (RET_DK ** -0.5)
    v = v.astype(jnp.float32)
    lg = jnp.log1p(-(2.0 ** (-5.0 - jnp.arange(RET_HEADS, dtype=jnp.float32))))
    idx = jnp.arange(c, dtype=jnp.float32)
    diff = idx[:, None] - idx[None, :]
    d_intra = jnp.where(diff >= 0, jnp.exp(lg[:, None, None] * jnp.maximum(diff, 0.0)), 0.0)
    d_q = jnp.exp(lg[:, None] * (idx + 1.0))[..., None]
    d_k = jnp.exp(lg[:, None] * (c - 1.0 - idx))[..., None]
    d_c = jnp.exp(lg * c)[:, None, None]

    def step(s, xs):
        qc, kc, vc = xs
        scores = jnp.einsum('bhid,bhjd->bhij', qc, kc) * d_intra
        o = (jnp.einsum('bhij,bhje->bhie', scores, vc)
             + jnp.einsum('bhid,bhde->bhie', qc * d_q, s))
        s = d_c * s + jnp.einsum('bhjd,bhje->bhde', kc * d_k, vc)
        return s, o

    s, o = lax.scan(step, state0.astype(jnp.float32), (to_chunks(q, c), to_chunks(k, c), to_chunks(v, c)))
    return from_chunks(o), s


def gated_delta(q, k, v, g, beta, state0):
    l = q.shape[1]
    c = math.gcd(l, GDN_CHUNK)
    incl = jnp.tril(jnp.ones((c, c), dtype=bool))
    strict = jnp.tril(jnp.ones((c, c), dtype=bool), -1)
    eye = jnp.eye(c, dtype=jnp.float32)

    def step(s, xs):
        qc, kc, vc, gc, bc = xs
        cum = jnp.cumsum(gc, axis=-1)
        dec = jnp.exp(jnp.where(incl, cum[..., :, None] - cum[..., None, :], -jnp.inf))
        kb = kc * bc[..., None]
        lower = jnp.where(strict, jnp.einsum('bhid,bhjd->bhij', kb, kc) * dec, 0.0)
        rhs = jnp.concatenate([vc * bc[..., None], kb * jnp.exp(cum)[..., None]], axis=-1)
        sol = lax.linalg.triangular_solve(lower + eye, rhs, left_side=True, lower=True, unit_diagonal=True)
        u, w = sol[..., :GDN_DV], sol[..., GDN_DV:]
        v_new = u - jnp.einsum('bhid,bhde->bhie', w, s)
        o = (jnp.einsum('bhid,bhde->bhie', qc * jnp.exp(cum)[..., None], s)
             + jnp.einsum('bhij,bhje->bhie', jnp.einsum('bhid,bhjd->bhij', qc, kc) * dec, v_new))
        s = (jnp.exp(cum[..., -1])[..., None, None] * s
             + jnp.einsum('bhjd,bhje->bhde', kc * jnp.exp(cum[..., -1:] - cum)[..., None], v_new))
        return s, o

    xs = (to_chunks(q, c), to_chunks(k, c), to_chunks(v, c), to_chunks(g, c), to_chunks(beta, c))
    s, o = lax.scan(step, state0.astype(jnp.float32), xs)
    return from_chunks(o), s


def gla(q, k, v, gk, state0):
    l = q.shape[1]
    c = math.gcd(l, GLA_CHUNK)
    mask = jnp.tril(jnp.ones((c, c), dtype=bool))[:, :, None]

    def step(s, xs):
        qc, kc, vc, gc = xs
        b = jnp.cumsum(gc, axis=2)
        rel = jnp.exp(jnp.where(mask, b[:, :, :, None, :] - b[:, :, None, :, :], -jnp.inf))
        scores = jnp.einsum('bhid,bhijd,bhjd->bhij', qc, rel, kc)
        o = (jnp.einsum('bhij,bhje->bhie', scores, vc)
             + jnp.einsum('bhid,bhde->bhie', qc * jnp.exp(b), s))
        b_last = b[:, :, -1:, :]
        s = (jnp.exp(b_last[:, :, 0, :, None]) * s
             + jnp.einsum('bhjd,bhje->bhde', kc * jnp.exp(b_last - b), vc))
        return s, o

    xs = (to_chunks(q, c), to_chunks(k, c), to_chunks(v, c), to_chunks(gk, c))
    s, o = lax.scan(step, state0.astype(jnp.float32), xs)
    return from_chunks(o), s


def mixer_block(h, pos0, s_ret, s_gdn, s_conv, s_gla, w_in, b_merge, conv_w, a_log, dt_bias,
                gdn_norm, gla_w_up, gla_b_up, gla_norm, w_branch, w_o):
    b, l, _ = h.shape
    f32 = jnp.float32
    (rq, rk, rv, rg, dqkv, da, db, dz, lq, lk, lv, llr, lgt, mg) = jnp.split(h @ w_in, IN_OFFSETS, axis=-1)

    o, new_ret = retention(heads(rq, RET_HEADS), heads(rk, RET_HEADS), heads(rv, RET_HEADS), s_ret, pos0)
    y_ret = (rmsnorm(o).reshape(b, l, -1) * jax.nn.silu(rg.astype(f32))).astype(h.dtype)

    xcat = jnp.concatenate([s_conv.astype(h.dtype), dqkv], axis=1)
    conv = xcat[:, 0:l] * conv_w[0]
    for w in range(1, GDN_CONV):
        conv = conv + xcat[:, w:w + l] * conv_w[w]
    conv = jax.nn.silu(conv.astype(f32))
    new_conv = xcat[:, -(GDN_CONV - 1):]
    gq, gk, gv = jnp.split(conv, [GDN_HEADS * GDN_DK, 2 * GDN_HEADS * GDN_DK], axis=-1)
    gq = l2norm(heads(gq, GDN_HEADS)) * (GDN_DK ** -0.5)
    gk = l2norm(heads(gk, GDN_HEADS))
    gdec = -jnp.exp(a_log.astype(f32)) * jax.nn.softplus(da.astype(f32) + dt_bias.astype(f32))
    beta = jax.nn.sigmoid(db.astype(f32))
    o, new_gdn = gated_delta(gq, gk, heads(gv, GDN_HEADS), gdec, beta, s_gdn)
    y_gdn = (rmsnorm(o, gdn_norm) * jax.nn.silu(heads(dz, GDN_HEADS).astype(f32))).reshape(b, l, -1).astype(h.dtype)

    gk_log = jax.nn.log_sigmoid(llr.astype(f32) @ gla_w_up.astype(f32) + gla_b_up.astype(f32)) / GLA_GATE_NORM
    o, new_gla = gla(heads(lq, GLA_HEADS).astype(f32) * (GLA_DK ** -0.5), heads(lk, GLA_HEADS).astype(f32),
                     heads(lv, GLA_HEADS).astype(f32), heads(gk_log, GLA_HEADS), s_gla)
    y_gla = (rmsnorm(o, gla_norm) * jax.nn.silu(heads(lgt, GLA_HEADS).astype(f32))).reshape(b, l, -1).astype(h.dtype)

    br = jnp.stack([y_ret, y_gdn, y_gla], axis=2)
    br = jnp.einsum('blnc,ncd->blnd', br, w_branch)
    gates = jax.nn.sigmoid(mg + b_merge).reshape(b, l, N_BRANCH, D_MODEL)
    out = jnp.sum(gates * br, axis=2) @ w_o
    return out, (new_ret, new_gdn, new_conv, new_gla)


def run_trunk(x, pos0, s_ret, s_gdn, s_conv, s_gla, norm_mix, norm_ffn, norm_final, w_in, b_merge,
              gdn_conv_w, gdn_a_log, gdn_dt_bias, gdn_norm, gla_w_up, gla_b_up, gla_norm,
              w_branch, w_o, w_gate_up, w_down):
    b = x.shape[0]
    new = ([], [], [], [])
    for layer in range(DEPTH):
        if s_ret is None:
            st = (jnp.zeros((b, RET_HEADS, RET_DK, RET_DV), x.dtype),
                  jnp.zeros((b, GDN_HEADS, GDN_DK, GDN_DV), x.dtype),
                  jnp.zeros((b, GDN_CONV - 1, GDN_QKV), x.dtype),
                  jnp.zeros((b, GLA_HEADS, GLA_DK, GLA_DV), x.dtype))
        else:
            st = (s_ret[layer], s_gdn[layer], s_conv[layer], s_gla[layer])
        h = rmsnorm(x, norm_mix[layer])
        mix, layer_states = mixer_block(h, pos0, st[0], st[1], st[2], st[3], w_in[layer], b_merge[layer],
                                        gdn_conv_w[layer], gdn_a_log[layer], gdn_dt_bias[layer], gdn_norm[layer],
                                        gla_w_up[layer], gla_b_up[layer], gla_norm[layer], w_branch[layer], w_o[layer])
        x = x + mix.astype(x.dtype)
        h = rmsnorm(x, norm_ffn[layer])
        gate, up = jnp.split(h @ w_gate_up[layer], 2, axis=-1)
        x = x + ((jax.nn.silu(gate) * up) @ w_down[layer]).astype(x.dtype)
        for lst, s in zip(new, layer_states):
            lst.append(s.astype(x.dtype))
    y = rmsnorm(x, norm_final)
    return y, [jnp.stack(lst) for lst in new]


def setup_inputs(seed: int = 0) -> dict:
    key = jax.random.key(seed)
    ks = jax.random.split(key, 24)
    f32 = jnp.float32
    nrm = lambda k, shape, scale: jax.random.normal(k, shape, f32) * scale
    dt = jnp.exp(jax.random.uniform(ks[10], (DEPTH, GDN_HEADS), f32, math.log(1e-3), math.log(0.1)))
    return {
        "x_prompt": nrm(ks[0], (BATCH, SEQ, D_MODEL), 1.0),
        "x_sample": nrm(ks[1], (DEC_BATCH, DEC_SEQ, D_MODEL), 1.0),
        "state_ret": nrm(ks[2], (DEPTH, DEC_BATCH, RET_HEADS, RET_DK, RET_DV), STATE_SCALE),
        "state_gdn": nrm(ks[3], (DEPTH, DEC_BATCH, GDN_HEADS, GDN_DK, GDN_DV), STATE_SCALE),
        "state_gdn_conv": nrm(ks[4], (DEPTH, DEC_BATCH, GDN_CONV - 1, GDN_QKV), 1.0),
        "state_gla": nrm(ks[5], (DEPTH, DEC_BATCH, GLA_HEADS, GLA_DK, GLA_DV), STATE_SCALE),
        "norm_mix": 1.0 + nrm(ks[6], (DEPTH, D_MODEL), 0.02),
        "norm_ffn": 1.0 + nrm(ks[7], (DEPTH, D_MODEL), 0.02),
        "norm_final": 1.0 + nrm(ks[8], (D_MODEL,), 0.02),
        "w_in": nrm(ks[9], (DEPTH, D_MODEL, D_IN), D_MODEL ** -0.5),
        "b_merge": nrm(ks[11], (DEPTH, N_BRANCH * D_MODEL), 0.02),
        "gdn_conv_w": nrm(ks[12], (DEPTH, GDN_CONV, GDN_QKV), GDN_CONV ** -0.5),
        "gdn_a_log": jnp.log(jax.random.uniform(ks[13], (DEPTH, GDN_HEADS), f32, 1.0, 16.0)),
        "gdn_dt_bias": dt + jnp.log(-jnp.expm1(-dt)),
        "gdn_norm": 1.0 + nrm(ks[14], (DEPTH, GDN_DV), 0.02),
        "gla_w_up": nrm(ks[15], (DEPTH, GLA_LOWRANK, GLA_HEADS * GLA_DK), GLA_LOWRANK ** -0.5),
        "gla_b_up": nrm(ks[16], (DEPTH, GLA_HEADS * GLA_DK), 0.02),
        "gla_norm": 1.0 + nrm(ks[17], (DEPTH, GLA_DV), 0.02),
        "w_branch": nrm(ks[18], (DEPTH, N_BRANCH, BRANCH_WIDTH, D_MODEL), BRANCH_WIDTH ** -0.5),
        "w_o": nrm(ks[19], (DEPTH, D_MODEL, D_MODEL), D_MODEL ** -0.5),
        "w_gate_up": nrm(ks[20], (DEPTH, D_MODEL, 2 * D_FF), D_MODEL ** -0.5),
        "w_down": nrm(ks[21], (DEPTH, D_FF, D_MODEL), D_FF ** -0.5),
    }


def reference(x_prompt, x_sample, state_ret, state_gdn, state_gdn_conv, state_gla, norm_mix, norm_ffn,
              norm_final, w_in, b_merge, gdn_conv_w, gdn_a_log, gdn_dt_bias, gdn_norm, gla_w_up, gla_b_up,
              gla_norm, w_branch, w_o, w_gate_up, w_down):
    weights = (norm_mix, norm_ffn, norm_final, w_in, b_merge, gdn_conv_w, gdn_a_log, gdn_dt_bias, gdn_norm,
               gla_w_up, gla_b_up, gla_norm, w_branch, w_o, w_gate_up, w_down)
    y_prompt, (p_ret, p_gdn, p_conv, p_gla) = run_trunk(x_prompt, 0.0, None, None, None, None, *weights)
    y_sample, (s_ret, s_gdn, s_conv, s_gla) = run_trunk(x_sample, float(PAST_LEN), state_ret, state_gdn,
                                                        state_gdn_conv, state_gla, *weights)
    return (y_prompt, y_sample, p_ret, p_gdn, p_conv, p_gla, s_ret, s_gdn, s_conv, s_gla)
```

```python
import functools
import math

import jax
import jax.numpy as jnp
from jax import lax
from jax.experimental import pallas as pl
from jax.experimental.pallas import tpu as pltpu

F32 = jnp.float32
BF16 = jnp.bfloat16

D_MODEL = 2048
DEPTH = 4
RET_HEADS, RET_DK, RET_DV = 4, 128, 256
GDN_HEADS, GDN_DK, GDN_DV, GDN_CONV = 8, 128, 128, 4
GDN_QKV = GDN_HEADS * (2 * GDN_DK + GDN_DV)
GLA_HEADS, GLA_DK, GLA_DV, GLA_LOWRANK = 4, 128, 256, 16
GLA_GATE_NORM = 16.0
N_BRANCH = 3
BRANCH_WIDTH = 1024
D_FF = 5632
EPS = 1e-6
ROPE_BASE = 10000.0
PAST_LEN = 16384

LANES = 128
SUBLANES = 8
CHUNK = 64
GLA_SUB = 16
SAMPLE_SLAB = 8
VMEM_LIMIT = 56 * 1024 * 1024

P_RQ, P_RK, P_RV, P_RG = 0, 512, 1024, 2048
P_DQKV, P_DZ = 3072, 6144
P_LQ, P_LK, P_LV, P_LGT, P_MG = 7168, 7680, 8192, 9216, 10240
P_MAIN = 16384
S_DA, S_DB, S_LLR = 0, 8, 16


def _cparams(sem):
    return pltpu.CompilerParams(dimension_semantics=sem, vmem_limit_bytes=VMEM_LIMIT)


def _dot(a, b):
    return jnp.dot(a.astype(BF16), b.astype(BF16), preferred_element_type=F32)


def _dot_nt(a, b):
    return lax.dot_general(a.astype(BF16), b.astype(BF16), (((1,), (1,)), ((), ())),
                           preferred_element_type=F32)


def _dot_tn(a, b):
    return lax.dot_general(a.astype(BF16), b.astype(BF16), (((0,), (0,)), ((), ())),
                           preferred_element_type=F32)


def _split2(x):
    hi = x.astype(BF16)
    lo = (x - hi.astype(F32)).astype(BF16)
    return hi, lo


def _split3(x):
    hi = x.astype(BF16)
    r = x - hi.astype(F32)
    mid = r.astype(BF16)
    lo = (r - mid.astype(F32)).astype(BF16)
    return hi, mid, lo


def _dot_hi(a, b):
    ah, al = _split2(a)
    bh, bl = _split2(b)
    d = functools.partial(jnp.dot, preferred_element_type=F32)
    return d(ah, bh) + (d(ah, bl) + d(al, bh))


def _dot_mask_l(m, x):
    mb = m.astype(BF16)
    hi, mid, lo = _split3(x)
    d = functools.partial(jnp.dot, preferred_element_type=F32)
    return d(mb, hi) + (d(mb, mid) + d(mb, lo))


def _dot_mask_r(x, m):
    mb = m.astype(BF16)
    hi, mid, lo = _split3(x)
    d = functools.partial(jnp.dot, preferred_element_type=F32)
    return d(hi, mb) + (d(mid, mb) + d(lo, mb))


def _iota(shape, dim):
    return lax.broadcasted_iota(jnp.int32, shape, dim)


def _silu(x):
    return x * jax.nn.sigmoid(x)


def _rms(x, eps=EPS):
    return x * lax.rsqrt(jnp.mean(x * x, axis=-1, keepdims=True) + eps)


def _row_to_col(r):
    n = r.shape[1]
    eye = _iota((n, n), 0) == _iota((n, n), 1)
    return jnp.sum(jnp.where(eye, jnp.broadcast_to(r, (n, n)), 0.0), axis=1, keepdims=True)


def _tril_incl(c):
    return (_iota((c, c), 1) <= _iota((c, c), 0)).astype(F32)


def _in_proj_kernel(x_ref, g_ref, w_ref, ws_ref, p_ref, ps_ref, h_ref):
    j = pl.program_id(1)

    @pl.when(j == 0)
    def _():
        x = x_ref[...]
        h = (_rms(x) * g_ref[...]).astype(BF16)
        h_ref[...] = h
        ps_ref[...] = jnp.dot(h, ws_ref[...], preferred_element_type=F32)

    p_ref[...] = jnp.dot(h_ref[...], w_ref[...], preferred_element_type=F32)


def _in_proj(x, g, w_main, w_small, layer, tm, tn):
    m, d = x.shape
    n = w_main.shape[2]
    return pl.pallas_call(
        _in_proj_kernel,
        grid=(m // tm, n // tn),
        in_specs=[
            pl.BlockSpec((tm, d), lambda i, j: (i, 0)),
            pl.BlockSpec((None, 1, d), lambda i, j: (layer, 0, 0)),
            pl.BlockSpec((None, d, tn), lambda i, j: (layer, 0, j)),
            pl.BlockSpec((None, d, LANES), lambda i, j: (layer, 0, 0)),
        ],
        out_specs=[
            pl.BlockSpec((tm, tn), lambda i, j: (i, j)),
            pl.BlockSpec((tm, LANES), lambda i, j: (i, 0)),
        ],
        out_shape=[jax.ShapeDtypeStruct((m, n), F32), jax.ShapeDtypeStruct((m, LANES), F32)],
        scratch_shapes=[pltpu.VMEM((tm, d), BF16)],
        compiler_params=_cparams(("arbitrary", "arbitrary")),
        name="in_proj",
    )(x, g, w_main, w_small)


def _merge_kernel(y_ref, wb_ref, mg_ref, bm_ref, o_ref, acc_ref):
    br = pl.program_id(2)

    @pl.when(br == 0)
    def _():
        acc_ref[...] = jnp.zeros_like(acc_ref)

    gate = jax.nn.sigmoid(mg_ref[...] + bm_ref[...])
    acc_ref[...] += gate * jnp.dot(y_ref[...], wb_ref[...], preferred_element_type=F32)

    @pl.when(br == N_BRANCH - 1)
    def _():
        o_ref[...] = acc_ref[...].astype(BF16)


def _merge(y3, wb, p_main, b_merge, layer, tm, tn):
    _, m, bw = y3.shape
    d = wb.shape[3]
    nj = d // tn
    mg_blk0 = P_MG // tn
    return pl.pallas_call(
        _merge_kernel,
        grid=(m // tm, nj, N_BRANCH),
        in_specs=[
            pl.BlockSpec((None, tm, bw), lambda i, j, b: (b, i, 0)),
            pl.BlockSpec((None, None, bw, tn), lambda i, j, b: (layer, b, 0, j)),
            pl.BlockSpec((tm, tn), lambda i, j, b: (i, mg_blk0 + b * nj + j)),
            pl.BlockSpec((None, None, 1, tn), lambda i, j, b: (layer, b, 0, j)),
        ],
        out_specs=pl.BlockSpec((tm, tn), lambda i, j, b: (i, j)),
        out_shape=jax.ShapeDtypeStruct((m, d), BF16),
        scratch_shapes=[pltpu.VMEM((tm, tn), F32)],
        compiler_params=_cparams(("arbitrary", "arbitrary", "arbitrary")),
        name="merge",
    )(y3, wb, p_main, b_merge)


def _out_proj_kernel(m_ref, w_ref, x_ref, o_ref):
    o_ref[...] = x_ref[...] + jnp.dot(m_ref[...], w_ref[...], preferred_element_type=F32)


def _out_proj(mrg, w_o, x, layer, tm, tn):
    m, d = x.shape
    return pl.pallas_call(
        _out_proj_kernel,
        grid=(m // tm, d // tn),
        in_specs=[
            pl.BlockSpec((tm, d), lambda i, j: (i, 0)),
            pl.BlockSpec((None, d, tn), lambda i, j: (layer, 0, j)),
            pl.BlockSpec((tm, tn), lambda i, j: (i, j)),
        ],
        out_specs=pl.BlockSpec((tm, tn), lambda i, j: (i, j)),
        out_shape=jax.ShapeDtypeStruct((m, d), F32),
        compiler_params=_cparams(("arbitrary", "arbitrary")),
        name="out_proj",
    )(mrg, w_o, x)


def _ffn_kernel(x_ref, g_ref, wg_ref, wu_ref, wd_ref, o_ref, h_ref, acc_ref):
    j = pl.program_id(1)

    @pl.when(j == 0)
    def _():
        x = x_ref[...]
        h_ref[...] = (_rms(x) * g_ref[...]).astype(BF16)
        acc_ref[...] = x

    h = h_ref[...]
    gate = jnp.dot(h, wg_ref[...], preferred_element_type=F32)
    up = jnp.dot(h, wu_ref[...], preferred_element_type=F32)
    act = (_silu(gate) * up).astype(BF16)
    acc_ref[...] += jnp.dot(act, wd_ref[...], preferred_element_type=F32)

    @pl.when(j == pl.num_programs(1) - 1)
    def _():
        o_ref[...] = acc_ref[...]


def _ffn(x, g, w_gu, w_dn, layer, tm, tf):
    m, d = x.shape
    dff = w_dn.shape[1]
    nf = dff // tf
    return pl.pallas_call(
        _ffn_kernel,
        grid=(m // tm, nf),
        in_specs=[
            pl.BlockSpec((tm, d), lambda i, j: (i, 0)),
            pl.BlockSpec((None, 1, d), lambda i, j: (layer, 0, 0)),
            pl.BlockSpec((None, d, tf), lambda i, j: (layer, 0, j)),
            pl.BlockSpec((None, d, tf), lambda i, j: (layer, 0, nf + j)),
            pl.BlockSpec((None, tf, d), lambda i, j: (layer, j, 0)),
        ],
        out_specs=pl.BlockSpec((tm, d), lambda i, j: (i, 0)),
        out_shape=jax.ShapeDtypeStruct((m, d), F32),
        scratch_shapes=[pltpu.VMEM((tm, d), BF16), pltpu.VMEM((tm, d), F32)],
        compiler_params=_cparams(("arbitrary", "arbitrary")),
        name="ffn",
    )(x, g, w_gu, w_gu, w_dn)


def _final_norm_kernel(x_ref, g_ref, o_ref):
    o_ref[...] = _rms(x_ref[...]) * g_ref[...]


def _final_norm(x, g, tm):
    m, d = x.shape
    return pl.pallas_call(
        _final_norm_kernel,
        grid=(m // tm,),
        in_specs=[pl.BlockSpec((tm, d), lambda i: (i, 0)), pl.BlockSpec((1, d), lambda i: (0, 0))],
        out_specs=pl.BlockSpec((tm, d), lambda i: (i, 0)),
        out_shape=jax.ShapeDtypeStruct((m, d), F32),
        compiler_params=_cparams(("arbitrary",)),
        name="final_norm",
    )(x, g)


def _rotary(t, cos, sin_signed):
    even = (_iota(t.shape, 1) & 1) == 0
    nxt = pltpu.roll(t, LANES - 1, axis=1)
    prv = pltpu.roll(t, 1, axis=1)
    return t * cos + jnp.where(even, nxt, prv) * sin_signed


def _ret_decays(c, nvalid, lg):
    i = _iota((c, c), 0)
    j = _iota((c, c), 1)
    diff = i - j
    d_intra = jnp.where(diff >= 0, jnp.exp(lg * jnp.maximum(diff, 0).astype(F32)), 0.0)
    pos_i = _iota((c, 1), 0)
    pos = pos_i.astype(F32)
    d_q = jnp.exp(lg * (pos + 1.0))
    d_k = jnp.where(pos_i < nvalid, jnp.exp(lg * (nvalid - 1.0 - pos)), 0.0)
    d_c = math.exp(lg * nvalid)
    return d_intra, d_q, d_k, d_c


def _ret_head(q, k, v, g, s, cos, sin, dec):
    d_intra, d_q, d_k, d_c = dec
    qr = _rotary(q, cos, sin)
    kr = _rotary(k, cos, sin) * (RET_DK ** -0.5)
    scores = _dot_nt(qr, kr) * d_intra
    o = _dot(scores, v) + _dot(qr * d_q, s)
    s_new = d_c * s + _dot_tn(kr * d_k, v)
    y = _rms(o) * _silu(g)
    return y, s_new


def _ret_lg(h):
    return math.log1p(-(2.0 ** (-5.0 - h)))


def _ret_prompt_kernel(q_ref, k_ref, v_ref, g_ref, cos_ref, sin_ref, y_ref, so_ref, s_ref):
    c = pl.program_id(1)

    @pl.when(c == 0)
    def _():
        s_ref[...] = jnp.zeros_like(s_ref)

    cos = cos_ref[...]
    sin = sin_ref[...]
    for h in range(RET_HEADS):
        dec = _ret_decays(CHUNK, CHUNK, _ret_lg(h))
        y, s_new = _ret_head(q_ref[:, h * RET_DK:(h + 1) * RET_DK], k_ref[:, h * RET_DK:(h + 1) * RET_DK],
                             v_ref[:, h * RET_DV:(h + 1) * RET_DV], g_ref[:, h * RET_DV:(h + 1) * RET_DV],
                             s_ref[h], cos, sin, dec)
        y_ref[:, h * RET_DV:(h + 1) * RET_DV] = y.astype(BF16)
        s_ref[h] = s_new

    @pl.when(c == pl.num_programs(1) - 1)
    def _():
        so_ref[...] = s_ref[...]


def _ret_prompt(p_main, cos, sin, nb, nc):
    qw = RET_HEADS * RET_DK
    vw = RET_HEADS * RET_DV
    rows = nb * nc * CHUNK
    return pl.pallas_call(
        _ret_prompt_kernel,
        grid=(nb, nc),
        in_specs=[
            pl.BlockSpec((CHUNK, qw), lambda b, c: (b * nc + c, P_RQ // qw)),
            pl.BlockSpec((CHUNK, qw), lambda b, c: (b * nc + c, P_RK // qw)),
            pl.BlockSpec((CHUNK, vw), lambda b, c: (b * nc + c, P_RV // vw)),
            pl.BlockSpec((CHUNK, vw), lambda b, c: (b * nc + c, P_RG // vw)),
            pl.BlockSpec((CHUNK, LANES), lambda b, c: (c, 0)),
            pl.BlockSpec((CHUNK, LANES), lambda b, c: (c, 0)),
        ],
        out_specs=[
            pl.BlockSpec((CHUNK, vw), lambda b, c: (b * nc + c, 0)),
            pl.BlockSpec((None, RET_HEADS, RET_DK, RET_DV), lambda b, c: (b, 0, 0, 0)),
        ],
        out_shape=[jax.ShapeDtypeStruct((rows, vw), BF16),
                   jax.ShapeDtypeStruct((nb, RET_HEADS, RET_DK, RET_DV), F32)],
        scratch_shapes=[pltpu.VMEM((RET_HEADS, RET_DK, RET_DV), F32)],
        compiler_params=_cparams(("arbitrary", "arbitrary")),
        name="ret_prompt",
    )(p_main, p_main, p_main, p_main, cos, sin)


def _slab_pair(x):
    return x, pltpu.roll(x, SAMPLE_SLAB // 2, axis=0)


def _slab_merge(ya, yb):
    top = _iota(ya.shape, 0) < SAMPLE_SLAB // 2
    return jnp.where(top, ya, pltpu.roll(yb, SAMPLE_SLAB // 2, axis=0))


def _state_io(state, so_prev, layer, ns, heads, dk, dv):
    blk = pl.BlockSpec((None, ns, heads, dk, dv), lambda i: (layer, i, 0, 0, 0))
    ins, specs, alias = [state], [blk], {}
    if so_prev is not None:
        ins.append(so_prev)
        specs.append(pl.BlockSpec(memory_space=pl.ANY))
    return ins, specs, blk, jax.ShapeDtypeStruct(state.shape, F32), alias


def _ret_sample_kernel(q_ref, k_ref, v_ref, g_ref, cos_ref, sin_ref, si_ref, *rest, lseg):
    y_ref, so_ref = rest[-2:]
    npair = q_ref.shape[0] // SAMPLE_SLAB
    cos = cos_ref[...]
    sin = sin_ref[...]
    decs = [_ret_decays(SAMPLE_SLAB, lseg, _ret_lg(h)) for h in range(RET_HEADS)]

    def pair(p, carry):
        r0 = pl.multiple_of(p * SAMPLE_SLAB, SAMPLE_SLAB)
        rows = pl.ds(r0, SAMPLE_SLAB)
        for h in range(RET_HEADS):
            qs = _slab_pair(q_ref[rows, h * RET_DK:(h + 1) * RET_DK])
            ks = _slab_pair(k_ref[rows, h * RET_DK:(h + 1) * RET_DK])
            vs = _slab_pair(v_ref[rows, h * RET_DV:(h + 1) * RET_DV])
            gs = _slab_pair(g_ref[rows, h * RET_DV:(h + 1) * RET_DV])
            ys = []
            for t in range(2):
                y, s_new = _ret_head(qs[t], ks[t], vs[t], gs[t], si_ref[2 * p + t, h], cos, sin, decs[h])
                so_ref[2 * p + t, h] = s_new
                ys.append(y)
            y_ref[rows, h * RET_DV:(h + 1) * RET_DV] = _slab_merge(ys[0], ys[1]).astype(BF16)
        return carry

    lax.fori_loop(0, npair, pair, 0)


def _ret_sample(p_main, cos, sin, state, so_prev, layer, row0, nseq, lseg, ns):
    qw = RET_HEADS * RET_DK
    vw = RET_HEADS * RET_DV
    rb = ns * lseg
    blk0 = row0 // rb
    st_in, st_specs, st_out_spec, st_shape, _ = _state_io(state, so_prev, layer, ns, RET_HEADS, RET_DK, RET_DV)
    n_in = 6 + len(st_in)
    return pl.pallas_call(
        functools.partial(_ret_sample_kernel, lseg=lseg),
        grid=(nseq // ns,),
        in_specs=[
            pl.BlockSpec((rb, qw), lambda i: (blk0 + i, P_RQ // qw)),
            pl.BlockSpec((rb, qw), lambda i: (blk0 + i, P_RK // qw)),
            pl.BlockSpec((rb, vw), lambda i: (blk0 + i, P_RV // vw)),
            pl.BlockSpec((rb, vw), lambda i: (blk0 + i, P_RG // vw)),
            pl.BlockSpec((SAMPLE_SLAB, LANES), lambda i: (0, 0)),
            pl.BlockSpec((SAMPLE_SLAB, LANES), lambda i: (0, 0)),
        ] + st_specs,
        out_specs=[pl.BlockSpec((rb, vw), lambda i: (i, 0)), st_out_spec],
        out_shape=[jax.ShapeDtypeStruct((nseq * lseg, vw), BF16), st_shape],
        input_output_aliases={n_in - 1: 1} if so_prev is not None else {},
        compiler_params=_cparams(("arbitrary",)),
        name="ret_sample",
    )(p_main, p_main, p_main, p_main, cos, sin, *st_in)


def _unit_lower_inverse(a):
    c = a.shape[0]
    i = _iota((c, c), 0)
    j = _iota((c, c), 1)
    eye = (i == j).astype(F32)
    t = eye - jnp.where((i >> 1 == j >> 1), a, 0.0)
    blk = 2
    while blk < c:
        sh = blk.bit_length()
        m = (i >> sh == j >> sh) & ((i & blk) != 0) & ((j & blk) == 0)
        x = jnp.where(m, a, 0.0)
        t = t - _dot_hi(_dot_hi(t, x), t)
        blk *= 2
    return t


def _gdn_gates(small, small_t, alog_r, dtb_r, alog_c, dtb_c, c, nvalid):
    g = -jnp.exp(alog_r) * jax.nn.softplus(small + dtb_r)
    g_t = -jnp.exp(alog_c) * jax.nn.softplus(small_t + dtb_c)
    tril = _tril_incl(c)
    cum = _dot_mask_l(tril, g)
    triu = (_iota((c, c), 0) <= _iota((c, c), 1)).astype(F32)
    cum_t = _dot_mask_r(g_t, triu)
    beta = jax.nn.sigmoid(small)
    return cum, cum_t, beta


def _gdn_head(q, k, v, z, s, cum_c, cum_r, beta_c, norm_g, nvalid):
    c = q.shape[0]
    qn = q * lax.rsqrt(jnp.sum(q * q, axis=-1, keepdims=True) + EPS) * (GDN_DK ** -0.5)
    kn = k * lax.rsqrt(jnp.sum(k * k, axis=-1, keepdims=True) + EPS)
    i = _iota((c, c), 0)
    j = _iota((c, c), 1)
    dec = jnp.exp(jnp.where(j <= i, cum_c - cum_r, -jnp.inf))
    kb = kn * beta_c
    lower = jnp.where(j < i, _dot_nt(kb, kn) * dec, 0.0)
    e_c = jnp.exp(cum_c)
    t = _unit_lower_inverse(lower)
    u = _dot(t, v * beta_c)
    w = _dot(t, kb * e_c)
    v_new = u - _dot(w, s)
    o = _dot(qn * e_c, s) + _dot(_dot_nt(qn, kn) * dec, v_new)
    cum_last = cum_c[nvalid - 1:nvalid, :]
    k_dec = jnp.where(_iota((c, 1), 0) < nvalid, jnp.exp(cum_last - cum_c), 0.0)
    s_new = jnp.exp(cum_last) * s + _dot_tn(kn * k_dec, v_new)
    y = _rms(o) * norm_g * _silu(z)
    return y, s_new


def _gdn_conv(xcat, cw):
    acc = xcat[SUBLANES:, :] * cw[GDN_CONV - 1:GDN_CONV, :]
    for w in range(GDN_CONV - 1):
        sh = GDN_CONV - 1 - w
        acc = acc + pltpu.roll(xcat, sh, axis=0)[SUBLANES:, :] * cw[w:w + 1, :]
    return _silu(acc)


def _gdn_prompt_kernel(x_ref, z_ref, sm_ref, smt_ref, cw_ref, alr_ref, dtr_ref, alc_ref, dtc_ref, ng_ref,
                       y_ref, so_ref, s_ref, halo_ref):
    c = pl.program_id(1)

    @pl.when(c == 0)
    def _():
        s_ref[...] = jnp.zeros_like(s_ref)
        halo_ref[...] = jnp.zeros_like(halo_ref)

    x = x_ref[...]
    xcat = jnp.concatenate([halo_ref[...], x], axis=0)
    halo_ref[...] = x[CHUNK - SUBLANES:, :]
    conv = _gdn_conv(xcat, cw_ref[...])
    cum, cum_t, beta = _gdn_gates(sm_ref[...], smt_ref[...], alr_ref[...], dtr_ref[...],
                                  alc_ref[...], dtc_ref[...], CHUNK, CHUNK)
    hk = GDN_HEADS * GDN_DK
    for h in range(GDN_HEADS):
        y, s_new = _gdn_head(conv[:, h * GDN_DK:(h + 1) * GDN_DK],
                             conv[:, hk + h * GDN_DK: hk + (h + 1) * GDN_DK],
                             conv[:, 2 * hk + h * GDN_DV: 2 * hk + (h + 1) * GDN_DV],
                             z_ref[:, h * GDN_DV:(h + 1) * GDN_DV], s_ref[h],
                             cum[:, S_DA + h:S_DA + h + 1], cum_t[S_DA + h:S_DA + h + 1, :],
                             beta[:, S_DB + h:S_DB + h + 1], ng_ref[...], CHUNK)
        y_ref[:, h * GDN_DV:(h + 1) * GDN_DV] = y.astype(BF16)
        s_ref[h] = s_new

    @pl.when(c == pl.num_programs(1) - 1)
    def _():
        so_ref[...] = s_ref[...]


def _gdn_prompt(p_main, p_small, p_small_t, cw, alr, dtr, alc, dtc, ng, layer, nb, nc):
    rows = nb * nc * CHUNK
    zw = GDN_HEADS * GDN_DV
    return pl.pallas_call(
        _gdn_prompt_kernel,
        grid=(nb, nc),
        in_specs=[
            pl.BlockSpec((CHUNK, GDN_QKV), lambda b, c: (b * nc + c, P_DQKV // GDN_QKV)),
            pl.BlockSpec((CHUNK, zw), lambda b, c: (b * nc + c, P_DZ // zw)),
            pl.BlockSpec((CHUNK, LANES), lambda b, c: (b * nc + c, 0)),
            pl.BlockSpec((None, LANES, CHUNK), lambda b, c: (b * nc + c, 0, 0)),
            pl.BlockSpec((None, GDN_CONV, GDN_QKV), lambda b, c: (layer, 0, 0)),
            pl.BlockSpec((None, 1, LANES), lambda b, c: (layer, 0, 0)),
            pl.BlockSpec((None, 1, LANES), lambda b, c: (layer, 0, 0)),
            pl.BlockSpec((None, LANES, 1), lambda b, c: (layer, 0, 0)),
            pl.BlockSpec((None, LANES, 1), lambda b, c: (layer, 0, 0)),
            pl.BlockSpec((None, 1, GDN_DV), lambda b, c: (layer, 0, 0)),
        ],
        out_specs=[
            pl.BlockSpec((CHUNK, zw), lambda b, c: (b * nc + c, 0)),
            pl.BlockSpec((None, GDN_HEADS, GDN_DK, GDN_DV), lambda b, c: (b, 0, 0, 0)),
        ],
        out_shape=[jax.ShapeDtypeStruct((rows, zw), BF16),
                   jax.ShapeDtypeStruct((nb, GDN_HEADS, GDN_DK, GDN_DV), F32)],
        scratch_shapes=[pltpu.VMEM((GDN_HEADS, GDN_DK, GDN_DV), F32), pltpu.VMEM((SUBLANES, GDN_QKV), F32)],
        compiler_params=_cparams(("arbitrary", "arbitrary")),
        name="gdn_prompt",
    )(p_main, p_main, p_small, p_small_t, cw, alr, dtr, alc, dtc, ng)


def _gdn_sample_kernel(x_ref, z_ref, sm_ref, smt_ref, sc_ref, cw_ref, alr_ref, dtr_ref, alc_ref, dtc_ref, ng_ref,
                       si_ref, *rest, lseg):
    y_ref, so_ref = rest[-2:]
    npair = x_ref.shape[0] // SAMPLE_SLAB
    cw = cw_ref[...]
    hk = GDN_HEADS * GDN_DK
    half = SAMPLE_SLAB // 2

    def pair(p, carry):
        r0 = pl.multiple_of(p * SAMPLE_SLAB, SAMPLE_SLAB)
        rows = pl.ds(r0, SAMPLE_SLAB)
        xs = _slab_pair(x_ref[rows, :])
        sms = _slab_pair(sm_ref[rows, :])
        ys = []
        for t in range(2):
            xcat = jnp.concatenate([sc_ref[2 * p + t], xs[t]], axis=0)
            conv = _gdn_conv(xcat, cw)
            cum, cum_t, beta = _gdn_gates(sms[t], smt_ref[2 * p + t], alr_ref[...], dtr_ref[...],
                                          alc_ref[...], dtc_ref[...], SAMPLE_SLAB, lseg)
            yh = []
            for h in range(GDN_HEADS):
                zrow = z_ref[rows, h * GDN_DV:(h + 1) * GDN_DV]
                z = zrow if t == 0 else pltpu.roll(zrow, half, axis=0)
                y, s_new = _gdn_head(conv[:, h * GDN_DK:(h + 1) * GDN_DK],
                                     conv[:, hk + h * GDN_DK: hk + (h + 1) * GDN_DK],
                                     conv[:, 2 * hk + h * GDN_DV: 2 * hk + (h + 1) * GDN_DV],
                                     z, si_ref[2 * p + t, h],
                                     cum[:, S_DA + h:S_DA + h + 1], cum_t[S_DA + h:S_DA + h + 1, :],
                                     beta[:, S_DB + h:S_DB + h + 1], ng_ref[...], lseg)
                so_ref[2 * p + t, h] = s_new
                yh.append(y)
            ys.append(yh)
        for h in range(GDN_HEADS):
            y_ref[rows, h * GDN_DV:(h + 1) * GDN_DV] = _slab_merge(ys[0][h], ys[1][h]).astype(BF16)
        return carry

    lax.fori_loop(0, npair, pair, 0)


def _gdn_sample(p_main, p_small, p_small_t, sconv, cw, alr, dtr, alc, dtc, ng, state, so_prev, layer, row0, nseq,
                lseg, ns):
    zw = GDN_HEADS * GDN_DV
    rb = ns * lseg
    blk0 = row0 // rb
    st_in, st_specs, st_out_spec, st_shape, _ = _state_io(state, so_prev, layer, ns, GDN_HEADS, GDN_DK, GDN_DV)
    n_in = 11 + len(st_in)
    return pl.pallas_call(
        functools.partial(_gdn_sample_kernel, lseg=lseg),
        grid=(nseq // ns,),
        in_specs=[
            pl.BlockSpec((rb, GDN_QKV), lambda i: (blk0 + i, P_DQKV // GDN_QKV)),
            pl.BlockSpec((rb, zw), lambda i: (blk0 + i, P_DZ // zw)),
            pl.BlockSpec((rb, LANES), lambda i: (blk0 + i, 0)),
            pl.BlockSpec((ns, LANES, SAMPLE_SLAB), lambda i: (i, 0, 0)),
            pl.BlockSpec((ns, SUBLANES, GDN_QKV), lambda i: (i, 0, 0)),
            pl.BlockSpec((None, GDN_CONV, GDN_QKV), lambda i: (layer, 0, 0)),
            pl.BlockSpec((None, 1, LANES), lambda i: (layer, 0, 0)),
            pl.BlockSpec((None, 1, LANES), lambda i: (layer, 0, 0)),
            pl.BlockSpec((None, LANES, 1), lambda i: (layer, 0, 0)),
            pl.BlockSpec((None, LANES, 1), lambda i: (layer, 0, 0)),
            pl.BlockSpec((None, 1, GDN_DV), lambda i: (layer, 0, 0)),
        ] + st_specs,
        out_specs=[pl.BlockSpec((rb, zw), lambda i: (i, 0)), st_out_spec],
        out_shape=[jax.ShapeDtypeStruct((nseq * lseg, zw), BF16), st_shape],
        input_output_aliases={n_in - 1: 1} if so_prev is not None else {},
        compiler_params=_cparams(("arbitrary",)),
        name="gdn_sample",
    )(p_main, p_main, p_small, p_small_t, sconv, cw, alr, dtr, alc, dtc, ng, *st_in)


def _gla_head(q, k, v, z, gk, s, norm_g, nvalid, sub):
    c = q.shape[0]
    q = q * (GLA_DK ** -0.5)
    b = _dot_mask_l(_tril_incl(c), gk)
    o = _dot(q * jnp.exp(b), s)
    jrow = _iota((sub, 1), 0)
    lane = _iota((sub, sub), 1)
    o_blocks = []
    for blk in range(c // sub):
        r0 = blk * sub
        qi, ki, bi, vi = q[r0:r0 + sub], k[r0:r0 + sub], b[r0:r0 + sub], v[r0:r0 + sub]
        st = jnp.zeros((sub, sub), F32)
        for i in range(sub):
            e = jnp.exp(jnp.where(jrow <= i, bi[i:i + 1, :] - bi, -jnp.inf))
            col = jnp.sum(qi[i:i + 1, :] * e * ki, axis=1, keepdims=True)
            st = jnp.where(lane == i, col, st)
        ob = _dot_tn(st, vi)
        if blk > 0:
            b0 = b[r0 - 1:r0, :]
            qt = qi * jnp.exp(bi - b0)
            kt = k[:r0] * jnp.exp(b0 - b[:r0])
            ob = ob + _dot(_dot_nt(qt, kt), v[:r0])
        o_blocks.append(ob)
    o = o + (jnp.concatenate(o_blocks, axis=0) if len(o_blocks) > 1 else o_blocks[0])
    b_last = b[nvalid - 1:nvalid, :]
    k_dec = jnp.where(_iota((c, 1), 0) < nvalid, jnp.exp(b_last - b), 0.0)
    s_new = _row_to_col(jnp.exp(b_last)) * s + _dot_tn(k * k_dec, v)
    y = _rms(o) * norm_g * _silu(z)
    return y, s_new


def _gla_gk(small, wup, bup):
    return jax.nn.log_sigmoid(jnp.dot(small.astype(BF16), wup, preferred_element_type=F32) + bup) / GLA_GATE_NORM


def _gla_prompt_kernel(q_ref, k_ref, v_ref, z_ref, sm_ref, wup_ref, bup_ref, ng_ref, y_ref, so_ref, s_ref):
    c = pl.program_id(1)

    @pl.when(c == 0)
    def _():
        s_ref[...] = jnp.zeros_like(s_ref)

    gk = _gla_gk(sm_ref[...], wup_ref[...], bup_ref[...])
    for h in range(GLA_HEADS):
        y, s_new = _gla_head(q_ref[:, h * GLA_DK:(h + 1) * GLA_DK], k_ref[:, h * GLA_DK:(h + 1) * GLA_DK],
                             v_ref[:, h * GLA_DV:(h + 1) * GLA_DV], z_ref[:, h * GLA_DV:(h + 1) * GLA_DV],
                             gk[:, h * GLA_DK:(h + 1) * GLA_DK], s_ref[h], ng_ref[...], CHUNK, GLA_SUB)
        y_ref[:, h * GLA_DV:(h + 1) * GLA_DV] = y.astype(BF16)
        s_ref[h] = s_new

    @pl.when(c == pl.num_programs(1) - 1)
    def _():
        so_ref[...] = s_ref[...]


def _gla_prompt(p_main, p_small, wup, bup, ng, layer, nb, nc):
    qw = GLA_HEADS * GLA_DK
    vw = GLA_HEADS * GLA_DV
    rows = nb * nc * CHUNK
    return pl.pallas_call(
        _gla_prompt_kernel,
        grid=(nb, nc),
        in_specs=[
            pl.BlockSpec((CHUNK, qw), lambda b, c: (b * nc + c, P_LQ // qw)),
            pl.BlockSpec((CHUNK, qw), lambda b, c: (b * nc + c, P_LK // qw)),
            pl.BlockSpec((CHUNK, vw), lambda b, c: (b * nc + c, P_LV // vw)),
            pl.BlockSpec((CHUNK, vw), lambda b, c: (b * nc + c, P_LGT // vw)),
            pl.BlockSpec((CHUNK, LANES), lambda b, c: (b * nc + c, 0)),
            pl.BlockSpec((None, LANES, qw), lambda b, c: (layer, 0, 0)),
            pl.BlockSpec((None, 1, qw), lambda b, c: (layer, 0, 0)),
            pl.BlockSpec((None, 1, GLA_DV), lambda b, c: (layer, 0, 0)),
        ],
        out_specs=[
            pl.BlockSpec((CHUNK, vw), lambda b, c: (b * nc + c, 0)),
            pl.BlockSpec((None, GLA_HEADS, GLA_DK, GLA_DV), lambda b, c: (b, 0, 0, 0)),
        ],
        out_shape=[jax.ShapeDtypeStruct((rows, vw), BF16),
                   jax.ShapeDtypeStruct((nb, GLA_HEADS, GLA_DK, GLA_DV), F32)],
        scratch_shapes=[pltpu.VMEM((GLA_HEADS, GLA_DK, GLA_DV), F32)],
        compiler_params=_cparams(("arbitrary", "arbitrary")),
        name="gla_prompt",
    )(p_main, p_main, p_main, p_main, p_small, wup, bup, ng)


def _gla_sample_kernel(q_ref, k_ref, v_ref, z_ref, sm_ref, wup_ref, bup_ref, ng_ref, si_ref, *rest, lseg):
    y_ref, so_ref = rest[-2:]
    npair = q_ref.shape[0] // SAMPLE_SLAB

    def pair(p, carry):
        r0 = pl.multiple_of(p * SAMPLE_SLAB, SAMPLE_SLAB)
        rows = pl.ds(r0, SAMPLE_SLAB)
        gk = _gla_gk(sm_ref[rows, :], wup_ref[...], bup_ref[...])
        for h in range(GLA_HEADS):
            qs = _slab_pair(q_ref[rows, h * GLA_DK:(h + 1) * GLA_DK])
            ks = _slab_pair(k_ref[rows, h * GLA_DK:(h + 1) * GLA_DK])
            vs = _slab_pair(v_ref[rows, h * GLA_DV:(h + 1) * GLA_DV])
            zs = _slab_pair(z_ref[rows, h * GLA_DV:(h + 1) * GLA_DV])
            gs = _slab_pair(gk[:, h * GLA_DK:(h + 1) * GLA_DK])
            ys = []
            for t in range(2):
                y, s_new = _gla_head(qs[t], ks[t], vs[t], zs[t], gs[t], si_ref[2 * p + t, h], ng_ref[...],
                                     lseg, SAMPLE_SLAB)
                so_ref[2 * p + t, h] = s_new
                ys.append(y)
            y_ref[rows, h * GLA_DV:(h + 1) * GLA_DV] = _slab_merge(ys[0], ys[1]).astype(BF16)
        return carry

    lax.fori_loop(0, npair, pair, 0)


def _gla_sample(p_main, p_small, wup, bup, ng, state, so_prev, layer, row0, nseq, lseg, ns):
    qw = GLA_HEADS * GLA_DK
    vw = GLA_HEADS * GLA_DV
    rb = ns * lseg
    blk0 = row0 // rb
    st_in, st_specs, st_out_spec, st_shape, _ = _state_io(state, so_prev, layer, ns, GLA_HEADS, GLA_DK, GLA_DV)
    n_in = 8 + len(st_in)
    return pl.pallas_call(
        functools.partial(_gla_sample_kernel, lseg=lseg),
        grid=(nseq // ns,),
        in_specs=[
            pl.BlockSpec((rb, qw), lambda i: (blk0 + i, P_LQ // qw)),
            pl.BlockSpec((rb, qw), lambda i: (blk0 + i, P_LK // qw)),
            pl.BlockSpec((rb, vw), lambda i: (blk0 + i, P_LV // vw)),
            pl.BlockSpec((rb, vw), lambda i: (blk0 + i, P_LGT // vw)),
            pl.BlockSpec((rb, LANES), lambda i: (blk0 + i, 0)),
            pl.BlockSpec((None, LANES, qw), lambda i: (layer, 0, 0)),
            pl.BlockSpec((None, 1, qw), lambda i: (layer, 0, 0)),
            pl.BlockSpec((None, 1, GLA_DV), lambda i: (layer, 0, 0)),
        ] + st_specs,
        out_specs=[pl.BlockSpec((rb, vw), lambda i: (i, 0)), st_out_spec],
        out_shape=[jax.ShapeDtypeStruct((nseq * lseg, vw), BF16), st_shape],
        input_output_aliases={n_in - 1: 1} if so_prev is not None else {},
        compiler_params=_cparams(("arbitrary",)),
        name="gla_sample",
    )(p_main, p_main, p_main, p_main, p_small, wup, bup, ng, *st_in)


def _rope_tables(pos0, length):
    inv = 1.0 / (ROPE_BASE ** jnp.linspace(0.0, 1.0, RET_DK // 2, dtype=F32))
    ang = (jnp.arange(length, dtype=F32) + pos0)[:, None] * inv[None, :]
    cos = jnp.repeat(jnp.cos(ang), 2, axis=1)
    sin = jnp.stack([-jnp.sin(ang), jnp.sin(ang)], axis=-1).reshape(length, RET_DK)
    return cos, sin


def _pack_w_in(w_in):
    main = jnp.concatenate([w_in[..., 0:6144], w_in[..., 6160:9232], w_in[..., 9248:16416]], axis=-1)
    small = jnp.concatenate([w_in[..., 6144:6160], w_in[..., 9232:9248],
                             jnp.zeros(w_in.shape[:-1] + (LANES - 32,), w_in.dtype)], axis=-1)
    return main.astype(BF16), small.astype(BF16)


def _lane_pad(v, off):
    n = v.shape[-1]
    return jnp.pad(v, ((0, 0), (off, LANES - off - n)))[:, None, :]


def kernel(x_prompt, x_sample, state_ret, state_gdn, state_gdn_conv, state_gla, norm_mix, norm_ffn, norm_final,
           w_in, b_merge, gdn_conv_w, gdn_a_log, gdn_dt_bias, gdn_norm, gla_w_up, gla_b_up, gla_norm, w_branch,
           w_o, w_gate_up, w_down):
    nb, seq, d = x_prompt.shape
    nsq, lseg, _ = x_sample.shape
    depth = w_in.shape[0]
    n_p = nb * seq
    n_s = nsq * lseg
    nc = seq // CHUNK
    ns = 8
    tm = 512
    assert seq % CHUNK == 0 and (n_p + n_s) % tm == 0 and nsq % ns == 0 and n_p % (ns * lseg) == 0
    assert 2 * lseg == SAMPLE_SLAB and lseg >= GDN_CONV - 1

    w_main, w_small = _pack_w_in(w_in)
    wb = w_branch.astype(BF16)
    wo = w_o.astype(BF16)
    wgu = w_gate_up.astype(BF16)
    wdn = w_down.astype(BF16)
    g_mix = norm_mix[:, None, :]
    g_ffn = norm_ffn[:, None, :]
    bm = b_merge.reshape(depth, N_BRANCH, 1, D_MODEL)
    alr = _lane_pad(gdn_a_log, S_DA)
    dtr = _lane_pad(gdn_dt_bias, S_DA)
    alc = jnp.swapaxes(alr, 1, 2)
    dtc = jnp.swapaxes(dtr, 1, 2)
    gdn_ng = gdn_norm[:, None, :]
    gla_ng = gla_norm[:, None, :]
    wup = jnp.pad(gla_w_up, ((0, 0), (S_LLR, LANES - S_LLR - GLA_LOWRANK), (0, 0))).astype(BF16)
    bup = gla_b_up[:, None, :]
    cos_p, sin_p = _rope_tables(0.0, seq)
    cos_s, sin_s = _rope_tables(float(PAST_LEN), lseg)
    cos_s = jnp.tile(cos_s, (SAMPLE_SLAB // lseg, 1))
    sin_s = jnp.tile(sin_s, (SAMPLE_SLAB // lseg, 1))

    x = jnp.concatenate([x_prompt.reshape(n_p, d), x_sample.reshape(n_s, d)], axis=0)

    outs = {k: [] for k in ("p_ret", "p_gdn", "p_conv", "p_gla", "s_conv")}
    s_ret = s_gdn = s_gla = None
    for layer in range(depth):
        p_main, p_small = _in_proj(x, g_mix, w_main, w_small, layer, tm, 1024)
        pst_p = jnp.swapaxes(p_small[:n_p].reshape(nb * nc, CHUNK, LANES), 1, 2)
        pst_s = jnp.swapaxes(jnp.pad(p_small[n_p:].reshape(nsq, lseg, LANES),
                                     ((0, 0), (0, SAMPLE_SLAB - lseg), (0, 0))), 1, 2)

        y_ret_p, st = _ret_prompt(p_main, cos_p, sin_p, nb, nc)
        outs["p_ret"].append(st)
        y_ret_s, s_ret = _ret_sample(p_main, cos_s, sin_s, state_ret, s_ret, layer, n_p, nsq, lseg, ns)

        y_gdn_p, st = _gdn_prompt(p_main, p_small, pst_p, gdn_conv_w, alr, dtr, alc, dtc, gdn_ng, layer, nb, nc)
        outs["p_gdn"].append(st)
        sconv = jnp.pad(state_gdn_conv[layer], ((0, 0), (SUBLANES - (GDN_CONV - 1), 0), (0, 0)))
        y_gdn_s, s_gdn = _gdn_sample(p_main, p_small, pst_s, sconv, gdn_conv_w, alr, dtr, alc, dtc, gdn_ng,
                                     state_gdn, s_gdn, layer, n_p, nsq, lseg, ns)
        dqkv = p_main[:, P_DQKV:P_DQKV + GDN_QKV]
        outs["p_conv"].append(dqkv[:n_p].reshape(nb, seq, GDN_QKV)[:, seq - (GDN_CONV - 1):])
        outs["s_conv"].append(jnp.concatenate([state_gdn_conv[layer], dqkv[n_p:].reshape(nsq, lseg, GDN_QKV)],
                                              axis=1)[:, -(GDN_CONV - 1):])

        y_gla_p, st = _gla_prompt(p_main, p_small, wup, bup, gla_ng, layer, nb, nc)
        outs["p_gla"].append(st)
        y_gla_s, s_gla = _gla_sample(p_main, p_small, wup, bup, gla_ng, state_gla, s_gla, layer, n_p, nsq, lseg, ns)

        y3 = jnp.stack([jnp.concatenate([y_ret_p, y_ret_s], axis=0),
                        jnp.concatenate([y_gdn_p, y_gdn_s], axis=0),
                        jnp.concatenate([y_gla_p, y_gla_s], axis=0)])
        mrg = _merge(y3, wb, p_main, bm, layer, tm, 1024)
        x = _out_proj(mrg, wo, x, layer, tm, 1024)
        x = _ffn(x, g_ffn, wgu, wdn, layer, tm, 512)

    y = _final_norm(x, norm_final[None, :], tm)
    st = {k: jnp.stack(v) for k, v in outs.items()}
    return (y[:n_p].reshape(nb, seq, d), y[n_p:].reshape(nsq, lseg, d),
            st["p_ret"], st["p_gdn"], st["p_conv"], st["p_gla"],
            s_ret, s_gdn, st["s_conv"], s_gla)
```

```python
import functools
import math

import jax
import jax.numpy as jnp
from jax import lax
from jax.experimental import pallas as pl
from jax.experimental.pallas import tpu as pltpu

F32 = jnp.float32
BF16 = jnp.bfloat16

D_MODEL = 2048
RET_HEADS, RET_DK, RET_DV = 4, 128, 256
GDN_HEADS, GDN_DK, GDN_DV, GDN_CONV = 8, 128, 128, 4
GDN_QKV = GDN_HEADS * (2 * GDN_DK + GDN_DV)
GLA_HEADS, GLA_DK, GLA_DV, GLA_LOWRANK = 4, 128, 256, 16
GLA_GATE_NORM = 16.0
N_BRANCH = 3
BRANCH_WIDTH = 1024
EPS = 1e-6
ROPE_BASE = 10000.0
PAST_LEN = 16384

LANES = 128
SUBLANES = 8
CHUNK = 64
GLA_SUB = 16
SAMPLE_SLAB = 8
SAMPLE_SEQS = 8
VMEM_LIMIT = 56 * 1024 * 1024

W_SEGS = ((0, 6144), (6160, 9232), (9248, 16416))
W_SMALL = ((6144, 6160), (9232, 9248))
P_RQ, P_RK, P_RV, P_RG = 0, 512, 1024, 2048
P_DQKV, P_DZ = 3072, 6144
P_LQ, P_LK, P_LV, P_LGT, P_MG = 7168, 7680, 8192, 9216, 10240
P_MAIN = 16384
S_DA, S_DB, S_LLR = 0, 8, 16


def _cparams(sem):
    return pltpu.CompilerParams(dimension_semantics=sem, vmem_limit_bytes=VMEM_LIMIT)


def _dense_tiles(m):
    best = 64
    for t in range(64, 1089, 64):
        if m % t == 0:
            best = t
    return best


_mm = functools.partial(jnp.dot, preferred_element_type=F32)


def _dot(a, b):
    return _mm(a.astype(BF16), b.astype(BF16))


def _dot_nt(a, b):
    return lax.dot_general(a.astype(BF16), b.astype(BF16), (((1,), (1,)), ((), ())),
                           preferred_element_type=F32)


def _dot_tn(a, b):
    return lax.dot_general(a.astype(BF16), b.astype(BF16), (((0,), (0,)), ((), ())),
                           preferred_element_type=F32)


def _split2(x):
    hi = x.astype(BF16)
    lo = (x - hi.astype(F32)).astype(BF16)
    return hi, lo


def _split3(x):
    hi = x.astype(BF16)
    r = x - hi.astype(F32)
    mid = r.astype(BF16)
    lo = (r - mid.astype(F32)).astype(BF16)
    return hi, mid, lo


def _dot_hi_all(a_list, b_list):
    sa = [_split2(a) for a in a_list]
    sb = [_split2(b) for b in b_list]
    r0 = [_mm(a[0], b[0]) for a, b in zip(sa, sb)]
    r1 = [_mm(a[0], b[1]) for a, b in zip(sa, sb)]
    r2 = [_mm(a[1], b[0]) for a, b in zip(sa, sb)]
    return [x + (y + z) for x, y, z in zip(r0, r1, r2)]


def _dot_mask_l(m, x):
    mb = m.astype(BF16)
    hi, mid, lo = _split3(x)
    return _mm(mb, hi) + (_mm(mb, mid) + _mm(mb, lo))


def _dot_mask_r(x, m):
    mb = m.astype(BF16)
    hi, mid, lo = _split3(x)
    return _mm(hi, mb) + (_mm(mid, mb) + _mm(lo, mb))


def _iota(shape, dim):
    return lax.broadcasted_iota(jnp.int32, shape, dim)


def _silu(x):
    return x * jax.nn.sigmoid(x)


def _rms(x, eps=EPS):
    return x * lax.rsqrt(jnp.mean(x * x, axis=-1, keepdims=True) + eps)


def _row_to_col(r):
    n = r.shape[1]
    eye = _iota((n, n), 0) == _iota((n, n), 1)
    return jnp.sum(jnp.where(eye, jnp.broadcast_to(r, (n, n)), 0.0), axis=1, keepdims=True)


def _tril_incl(c):
    return (_iota((c, c), 1) <= _iota((c, c), 0)).astype(F32)


def _in_proj_kernel(x_ref, g_ref, wa_ref, wb_ref, wc_ref, ws_ref, p_ref, ps_ref, h_ref, *, na, nb):
    j = pl.program_id(1)

    @pl.when(j == 0)
    def _():
        h = (_rms(x_ref[...]) * g_ref[...]).astype(BF16)
        h_ref[...] = h
        ps_ref[...] = _mm(h, ws_ref[...])

    @pl.when(j < na)
    def _():
        p_ref[...] = _mm(h_ref[...], wa_ref[...])

    @pl.when((j >= na) & (j < na + nb))
    def _():
        p_ref[...] = _mm(h_ref[...], wb_ref[...])

    @pl.when(j >= na + nb)
    def _():
        p_ref[...] = _mm(h_ref[...], wc_ref[...])


def _in_proj(x, g, w_segs, w_small, layer, tm, tn):
    m, d = x.shape
    na, nb, nc = (w.shape[2] // tn for w in w_segs)
    return pl.pallas_call(
        functools.partial(_in_proj_kernel, na=na, nb=nb),
        grid=(m // tm, na + nb + nc),
        in_specs=[
            pl.BlockSpec((tm, d), lambda i, j: (i, 0), pipeline_mode=pl.Buffered(1)),
            pl.BlockSpec((None, 1, d), lambda i, j: (layer, 0, 0)),
            pl.BlockSpec((None, d, tn), lambda i, j: (layer, 0, jnp.minimum(j, na - 1))),
            pl.BlockSpec((None, d, tn), lambda i, j: (layer, 0, jnp.clip(j - na, 0, nb - 1))),
            pl.BlockSpec((None, d, tn), lambda i, j: (layer, 0, jnp.clip(j - na - nb, 0, nc - 1))),
            pl.BlockSpec((None, d, LANES), lambda i, j: (layer, 0, 0)),
        ],
        out_specs=[
            pl.BlockSpec((tm, tn), lambda i, j: (i, j)),
            pl.BlockSpec((tm, LANES), lambda i, j: (i, 0)),
        ],
        out_shape=[jax.ShapeDtypeStruct((m, (na + nb + nc) * tn), F32), jax.ShapeDtypeStruct((m, LANES), F32)],
        scratch_shapes=[pltpu.VMEM((tm, d), BF16)],
        compiler_params=_cparams(("arbitrary", "arbitrary")),
        name="in_proj",
    )(x, g, *w_segs, w_small)


def _merge_kernel(y0_ref, y1_ref, y2_ref, w0_ref, w1_ref, w2_ref, g0_ref, g1_ref, g2_ref,
                  b0_ref, b1_ref, b2_ref, o_ref):
    acc = None
    for y_ref, w_ref, g_ref, b_ref in ((y0_ref, w0_ref, g0_ref, b0_ref), (y1_ref, w1_ref, g1_ref, b1_ref),
                                       (y2_ref, w2_ref, g2_ref, b2_ref)):
        t = jax.nn.sigmoid(g_ref[...] + b_ref[...]) * _mm(y_ref[...], w_ref[...])
        acc = t if acc is None else acc + t
    o_ref[...] = acc.astype(BF16)


def _merge(ys, wb, p_main, b_merge, layer, tm, tn):
    m, bw = ys[0].shape
    d = wb.shape[3]
    nj = d // tn
    mg0 = P_MG // tn

    def y_spec():
        return pl.BlockSpec((tm, bw), lambda i, j: (i, 0), pipeline_mode=pl.Buffered(1))

    def w_spec(b):
        return pl.BlockSpec((None, None, bw, tn), lambda i, j: (layer, b, 0, j))

    def g_spec(b):
        return pl.BlockSpec((tm, tn), lambda i, j: (i, mg0 + b * nj + j))

    def b_spec(b):
        return pl.BlockSpec((None, None, 1, tn), lambda i, j: (layer, b, 0, j))

    rng = range(N_BRANCH)
    return pl.pallas_call(
        _merge_kernel,
        grid=(m // tm, nj),
        in_specs=[y_spec() for _ in rng] + [w_spec(b) for b in rng] + [g_spec(b) for b in rng]
        + [b_spec(b) for b in rng],
        out_specs=pl.BlockSpec((tm, tn), lambda i, j: (i, j)),
        out_shape=jax.ShapeDtypeStruct((m, d), BF16),
        compiler_params=_cparams(("arbitrary", "arbitrary")),
        name="merge",
    )(*ys, wb, wb, wb, p_main, p_main, p_main, b_merge, b_merge, b_merge)


def _out_proj_kernel(m_ref, w_ref, x_ref, o_ref):
    o_ref[...] = x_ref[...] + _mm(m_ref[...], w_ref[...])


def _out_proj(mrg, w_o, x, layer, tm, tn):
    m, d = x.shape
    return pl.pallas_call(
        _out_proj_kernel,
        grid=(m // tm, d // tn),
        in_specs=[
            pl.BlockSpec((tm, d), lambda i, j: (i, 0)),
            pl.BlockSpec((None, d, tn), lambda i, j: (layer, 0, j)),
            pl.BlockSpec((tm, tn), lambda i, j: (i, j)),
        ],
        out_specs=pl.BlockSpec((tm, tn), lambda i, j: (i, j)),
        out_shape=jax.ShapeDtypeStruct((m, d), F32),
        compiler_params=_cparams(("arbitrary", "arbitrary")),
        name="out_proj",
    )(mrg, w_o, x)


def _ffn_kernel(x_ref, g_ref, wg_ref, wu_ref, wd_ref, o_ref, h_ref):
    j = pl.program_id(1)

    @pl.when(j == 0)
    def _():
        x = x_ref[...]
        h_ref[...] = (_rms(x) * g_ref[...]).astype(BF16)
        o_ref[...] = x

    h = h_ref[...]
    act = (_silu(_mm(h, wg_ref[...])) * _mm(h, wu_ref[...])).astype(BF16)
    o_ref[...] += _mm(act, wd_ref[...])


def _ffn(x, g, w_gu, w_dn, layer, tm, tf):
    m, d = x.shape
    nf = w_dn.shape[1] // tf
    return pl.pallas_call(
        _ffn_kernel,
        grid=(m // tm, nf),
        in_specs=[
            pl.BlockSpec((tm, d), lambda i, j: (i, 0), pipeline_mode=pl.Buffered(1)),
            pl.BlockSpec((None, 1, d), lambda i, j: (layer, 0, 0)),
            pl.BlockSpec((None, d, tf), lambda i, j: (layer, 0, j)),
            pl.BlockSpec((None, d, tf), lambda i, j: (layer, 0, nf + j)),
            pl.BlockSpec((None, tf, d), lambda i, j: (layer, j, 0)),
        ],
        out_specs=pl.BlockSpec((tm, d), lambda i, j: (i, 0)),
        out_shape=jax.ShapeDtypeStruct((m, d), F32),
        scratch_shapes=[pltpu.VMEM((tm, d), BF16)],
        compiler_params=_cparams(("arbitrary", "arbitrary")),
        name="ffn",
    )(x, g, w_gu, w_gu, w_dn)


def _final_norm_kernel(x_ref, g_ref, o_ref):
    o_ref[...] = _rms(x_ref[...]) * g_ref[...]


def _final_norm(x, g, row0, rows, tm):
    d = x.shape[1]
    blk0 = row0 // tm
    return pl.pallas_call(
        _final_norm_kernel,
        grid=(rows // tm,),
        in_specs=[pl.BlockSpec((tm, d), lambda i: (blk0 + i, 0)), pl.BlockSpec((1, d), lambda i: (0, 0))],
        out_specs=pl.BlockSpec((tm, d), lambda i: (i, 0)),
        out_shape=jax.ShapeDtypeStruct((rows, d), F32),
        compiler_params=_cparams(("arbitrary",)),
        name="final_norm",
    )(x, g)


def _slab_pair(x):
    return x, pltpu.roll(x, SAMPLE_SLAB // 2, axis=0)


def _slab_merge(ya, yb):
    top = _iota(ya.shape, 0) < SAMPLE_SLAB // 2
    return jnp.where(top, ya, pltpu.roll(yb, SAMPLE_SLAB // 2, axis=0))


def _sample_io(state, so_prev, y_full, layer, heads, dk, dv, n_fixed):
    blk = pl.BlockSpec((None, SAMPLE_SEQS, heads, dk, dv), lambda i: (layer, i, 0, 0, 0))
    ins, specs = [state, y_full], [blk, pl.BlockSpec(memory_space=pl.ANY)]
    aliases = {n_fixed + 1: 0}
    if so_prev is not None:
        ins.append(so_prev)
        specs.append(pl.BlockSpec(memory_space=pl.ANY))
        aliases[n_fixed + 2] = 1
    out_shapes = [jax.ShapeDtypeStruct(y_full.shape, y_full.dtype), jax.ShapeDtypeStruct(state.shape, F32)]
    return ins, specs, blk, out_shapes, aliases


def _rotary(t, cos, sin_signed):
    even = (_iota(t.shape, 1) & 1) == 0
    nxt = pltpu.roll(t, LANES - 1, axis=1)
    prv = pltpu.roll(t, 1, axis=1)
    return t * cos + jnp.where(even, nxt, prv) * sin_signed


def _ret_decays(c, nvalid, lg):
    i = _iota((c, c), 0)
    j = _iota((c, c), 1)
    diff = i - j
    d_intra = jnp.where(diff >= 0, jnp.exp(lg * jnp.maximum(diff, 0).astype(F32)), 0.0)
    pos_i = _iota((c, 1), 0)
    pos = pos_i.astype(F32)
    d_q = jnp.exp(lg * (pos + 1.0))
    d_k = jnp.where(pos_i < nvalid, jnp.exp(lg * (nvalid - 1.0 - pos)), 0.0)
    d_c = math.exp(lg * nvalid)
    return d_intra, d_q, d_k, d_c


def _ret_lg(h):
    return math.log1p(-(2.0 ** (-5.0 - h)))


def _ret_chunks(qs, ks, vs, gs, ss, cos, sin, decs):
    n = range(len(qs))
    qr = [_rotary(q, cos, sin) for q in qs]
    kr = [_rotary(k, cos, sin) * (RET_DK ** -0.5) for k in ks]
    sc = [_dot_nt(qr[p], kr[p]) * decs[p][0] for p in n]
    o_st = [_dot(qr[p] * decs[p][1], ss[p]) for p in n]
    kv = [_dot_tn(kr[p] * decs[p][2], vs[p]) for p in n]
    o = [_dot(sc[p], vs[p]) + o_st[p] for p in n]
    s_new = [decs[p][3] * ss[p] + kv[p] for p in n]
    ys = [_rms(o[p]) * _silu(gs[p]) for p in n]
    return ys, s_new


def _ret_prompt_kernel(q_ref, k_ref, v_ref, g_ref, cos_ref, sin_ref, y_ref, so_ref, s_ref):
    c = pl.program_id(1)

    @pl.when(c == 0)
    def _():
        s_ref[...] = jnp.zeros_like(s_ref)

    hs = range(RET_HEADS)
    decs = [_ret_decays(CHUNK, CHUNK, _ret_lg(h)) for h in hs]
    ys, s_new = _ret_chunks([q_ref[:, h * RET_DK:(h + 1) * RET_DK] for h in hs],
                            [k_ref[:, h * RET_DK:(h + 1) * RET_DK] for h in hs],
                            [v_ref[:, h * RET_DV:(h + 1) * RET_DV] for h in hs],
                            [g_ref[:, h * RET_DV:(h + 1) * RET_DV] for h in hs],
                            [s_ref[h] for h in hs], cos_ref[...], sin_ref[...], decs)
    for h in hs:
        y_ref[:, h * RET_DV:(h + 1) * RET_DV] = ys[h].astype(BF16)
        s_ref[h] = s_new[h]

    @pl.when(c == pl.num_programs(1) - 1)
    def _():
        so_ref[...] = s_ref[...]


def _ret_prompt(p_main, cos, sin, nb, nc):
    qw = RET_HEADS * RET_DK
    vw = RET_HEADS * RET_DV
    return pl.pallas_call(
        _ret_prompt_kernel,
        grid=(nb, nc),
        in_specs=[
            pl.BlockSpec((CHUNK, qw), lambda b, c: (b * nc + c, P_RQ // qw)),
            pl.BlockSpec((CHUNK, qw), lambda b, c: (b * nc + c, P_RK // qw)),
            pl.BlockSpec((CHUNK, vw), lambda b, c: (b * nc + c, P_RV // vw)),
            pl.BlockSpec((CHUNK, vw), lambda b, c: (b * nc + c, P_RG // vw)),
            pl.BlockSpec((CHUNK, LANES), lambda b, c: (c, 0)),
            pl.BlockSpec((CHUNK, LANES), lambda b, c: (c, 0)),
        ],
        out_specs=[
            pl.BlockSpec((CHUNK, vw), lambda b, c: (b * nc + c, 0)),
            pl.BlockSpec((None, RET_HEADS, RET_DK, RET_DV), lambda b, c: (b, 0, 0, 0)),
        ],
        out_shape=[jax.ShapeDtypeStruct((p_main.shape[0], vw), BF16),
                   jax.ShapeDtypeStruct((nb, RET_HEADS, RET_DK, RET_DV), F32)],
        scratch_shapes=[pltpu.VMEM((RET_HEADS, RET_DK, RET_DV), F32)],
        compiler_params=_cparams(("arbitrary", "arbitrary")),
        name="ret_prompt",
    )(p_main, p_main, p_main, p_main, cos, sin)


def _ret_sample_kernel(q_ref, k_ref, v_ref, g_ref, cos_ref, sin_ref, si_ref, *rest, lseg):
    y_ref, so_ref = rest[-2:]
    npair = q_ref.shape[0] // SAMPLE_SLAB
    cos = cos_ref[...]
    sin = sin_ref[...]
    hs = range(RET_HEADS)
    decs = [_ret_decays(SAMPLE_SLAB, lseg, _ret_lg(h)) for h in hs] * 2

    def pair(p, carry):
        r0 = pl.multiple_of(p * SAMPLE_SLAB, SAMPLE_SLAB)
        rows = pl.ds(r0, SAMPLE_SLAB)
        qs = [_slab_pair(q_ref[rows, h * RET_DK:(h + 1) * RET_DK]) for h in hs]
        ks = [_slab_pair(k_ref[rows, h * RET_DK:(h + 1) * RET_DK]) for h in hs]
        vs = [_slab_pair(v_ref[rows, h * RET_DV:(h + 1) * RET_DV]) for h in hs]
        gs = [_slab_pair(g_ref[rows, h * RET_DV:(h + 1) * RET_DV]) for h in hs]
        order = [(t, h) for t in range(2) for h in hs]
        ys, s_new = _ret_chunks([qs[h][t] for t, h in order], [ks[h][t] for t, h in order],
                                [vs[h][t] for t, h in order], [gs[h][t] for t, h in order],
                                [si_ref[2 * p + t, h] for t, h in order], cos, sin, decs)
        for idx, (t, h) in enumerate(order):
            so_ref[2 * p + t, h] = s_new[idx]
        for h in hs:
            y_ref[rows, h * RET_DV:(h + 1) * RET_DV] = _slab_merge(ys[h], ys[RET_HEADS + h]).astype(BF16)
        return carry

    lax.fori_loop(0, npair, pair, 0)


def _ret_sample(p_main, cos, sin, state, so_prev, y_full, layer, row0, nseq, lseg):
    qw = RET_HEADS * RET_DK
    vw = RET_HEADS * RET_DV
    rb = SAMPLE_SEQS * lseg
    blk0 = row0 // rb
    st_in, st_specs, st_out_spec, out_shapes, aliases = _sample_io(state, so_prev, y_full, layer, RET_HEADS,
                                                                   RET_DK, RET_DV, 6)
    return pl.pallas_call(
        functools.partial(_ret_sample_kernel, lseg=lseg),
        grid=(nseq // SAMPLE_SEQS,),
        in_specs=[
            pl.BlockSpec((rb, qw), lambda i: (blk0 + i, P_RQ // qw)),
            pl.BlockSpec((rb, qw), lambda i: (blk0 + i, P_RK // qw)),
            pl.BlockSpec((rb, vw), lambda i: (blk0 + i, P_RV // vw)),
            pl.BlockSpec((rb, vw), lambda i: (blk0 + i, P_RG // vw)),
            pl.BlockSpec((SAMPLE_SLAB, LANES), lambda i: (0, 0)),
            pl.BlockSpec((SAMPLE_SLAB, LANES), lambda i: (0, 0)),
        ] + st_specs,
        out_specs=[pl.BlockSpec((rb, vw), lambda i: (blk0 + i, 0)), st_out_spec],
        out_shape=out_shapes,
        input_output_aliases=aliases,
        compiler_params=_cparams(("arbitrary",)),
        name="ret_sample",
    )(p_main, p_main, p_main, p_main, cos, sin, *st_in)


def _unit_lower_inverse_all(a_list, nvalid):
    c = a_list[0].shape[0]
    i = _iota((c, c), 0)
    j = _iota((c, c), 1)
    eye = (i == j).astype(F32)
    pair = (i >> 1) == (j >> 1)
    ts = [eye - jnp.where(pair, a, 0.0) for a in a_list]
    blk = 2
    while blk < min(c, nvalid):
        sh = blk.bit_length()
        m = ((i >> sh) == (j >> sh)) & ((i & blk) != 0) & ((j & blk) == 0)
        xs = [jnp.where(m, a, 0.0) for a in a_list]
        txt = _dot_hi_all(_dot_hi_all(ts, xs), ts)
        ts = [t - y for t, y in zip(ts, txt)]
        blk *= 2
    return ts


def _gdn_gates(small, small_t, alog_r, dtb_r, alog_c, dtb_c, c):
    g = -jnp.exp(alog_r) * jax.nn.softplus(small + dtb_r)
    g_t = -jnp.exp(alog_c) * jax.nn.softplus(small_t + dtb_c)
    cum = _dot_mask_l(_tril_incl(c), g)
    triu = (_iota((c, c), 0) <= _iota((c, c), 1)).astype(F32)
    cum_t = _dot_mask_r(g_t, triu)
    beta = jax.nn.sigmoid(small)
    return cum, cum_t, beta


def _gdn_chunks(qs, ks, vs, zs, ss, cum_cs, cum_rs, beta_cs, norm_g, nvalid):
    n = range(len(qs))
    c = qs[0].shape[0]
    i = _iota((c, c), 0)
    j = _iota((c, c), 1)
    rowv = _iota((c, 1), 0) < nvalid
    qn = [q * lax.rsqrt(jnp.sum(q * q, axis=-1, keepdims=True) + EPS) * (GDN_DK ** -0.5) for q in qs]
    kn = [k * lax.rsqrt(jnp.sum(k * k, axis=-1, keepdims=True) + EPS) for k in ks]
    dec = [jnp.exp(jnp.where(j <= i, cum_cs[p] - cum_rs[p], -jnp.inf)) for p in n]
    e_c = [jnp.exp(cum_cs[p]) for p in n]
    kb = [kn[p] * beta_cs[p] for p in n]
    kk = [_dot_nt(kb[p], kn[p]) for p in n]
    qk = [_dot_nt(qn[p], kn[p]) for p in n]
    qs_st = [_dot(qn[p] * e_c[p], ss[p]) for p in n]
    lower = [jnp.where(j < i, kk[p] * dec[p], 0.0) for p in n]
    ts = _unit_lower_inverse_all(lower, nvalid)
    rhs = [jnp.concatenate([vs[p] * beta_cs[p], kb[p] * e_c[p]], axis=1) for p in n]
    sol = [_dot(ts[p], rhs[p]) for p in n]
    ws = [_dot(sol[p][:, GDN_DV:], ss[p]) for p in n]
    v_new = [sol[p][:, :GDN_DV] - ws[p] for p in n]
    cum_last = [cum_cs[p][nvalid - 1:nvalid, :] for p in n]
    k_dec = [jnp.where(rowv, jnp.exp(cum_last[p] - cum_cs[p]), 0.0) for p in n]
    o = [qs_st[p] + _dot(qk[p] * dec[p], v_new[p]) for p in n]
    kv = [_dot_tn(kn[p] * k_dec[p], v_new[p]) for p in n]
    s_new = [jnp.exp(cum_last[p]) * ss[p] + kv[p] for p in n]
    ys = [_rms(o[p]) * norm_g * _silu(zs[p]) for p in n]
    return ys, s_new


def _gdn_conv(xcat, cw):
    acc = xcat[SUBLANES:, :] * cw[GDN_CONV - 1:GDN_CONV, :]
    for w in range(GDN_CONV - 1):
        sh = GDN_CONV - 1 - w
        acc = acc + pltpu.roll(xcat, sh, axis=0)[SUBLANES:, :] * cw[w:w + 1, :]
    return _silu(acc)


def _gdn_split(conv, h):
    hk = GDN_HEADS * GDN_DK
    return (conv[:, h * GDN_DK:(h + 1) * GDN_DK], conv[:, hk + h * GDN_DK: hk + (h + 1) * GDN_DK],
            conv[:, 2 * hk + h * GDN_DV: 2 * hk + (h + 1) * GDN_DV])


def _gdn_prompt_kernel(x_ref, z_ref, sm_ref, smt_ref, cw_ref, alr_ref, dtr_ref, alc_ref, dtc_ref, ng_ref,
                       y_ref, so_ref, s_ref, halo_ref):
    c = pl.program_id(1)

    @pl.when(c == 0)
    def _():
        s_ref[...] = jnp.zeros_like(s_ref)
        halo_ref[...] = jnp.zeros_like(halo_ref)

    x = x_ref[...]
    xcat = jnp.concatenate([halo_ref[...], x], axis=0)
    halo_ref[...] = x[CHUNK - SUBLANES:, :]
    conv = _gdn_conv(xcat, cw_ref[...])
    cum, cum_t, beta = _gdn_gates(sm_ref[...], smt_ref[...], alr_ref[...], dtr_ref[...],
                                  alc_ref[...], dtc_ref[...], CHUNK)
    hs = range(GDN_HEADS)
    qkv = [_gdn_split(conv, h) for h in hs]
    ys, s_new = _gdn_chunks([t[0] for t in qkv], [t[1] for t in qkv], [t[2] for t in qkv],
                            [z_ref[:, h * GDN_DV:(h + 1) * GDN_DV] for h in hs], [s_ref[h] for h in hs],
                            [cum[:, S_DA + h:S_DA + h + 1] for h in hs],
                            [cum_t[S_DA + h:S_DA + h + 1, :] for h in hs],
                            [beta[:, S_DB + h:S_DB + h + 1] for h in hs], ng_ref[...], CHUNK)
    for h in hs:
        y_ref[:, h * GDN_DV:(h + 1) * GDN_DV] = ys[h].astype(BF16)
        s_ref[h] = s_new[h]

    @pl.when(c == pl.num_programs(1) - 1)
    def _():
        so_ref[...] = s_ref[...]


def _gdn_prompt(p_main, p_small, p_small_t, cw, alr, dtr, alc, dtc, ng, layer, nb, nc):
    zw = GDN_HEADS * GDN_DV
    return pl.pallas_call(
        _gdn_prompt_kernel,
        grid=(nb, nc),
        in_specs=[
            pl.BlockSpec((CHUNK, GDN_QKV), lambda b, c: (b * nc + c, P_DQKV // GDN_QKV)),
            pl.BlockSpec((CHUNK, zw), lambda b, c: (b * nc + c, P_DZ // zw)),
            pl.BlockSpec((CHUNK, LANES), lambda b, c: (b * nc + c, 0)),
            pl.BlockSpec((None, LANES, CHUNK), lambda b, c: (b * nc + c, 0, 0)),
            pl.BlockSpec((None, GDN_CONV, GDN_QKV), lambda b, c: (layer, 0, 0)),
            pl.BlockSpec((None, 1, LANES), lambda b, c: (layer, 0, 0)),
            pl.BlockSpec((None, 1, LANES), lambda b, c: (layer, 0, 0)),
            pl.BlockSpec((None, LANES, 1), lambda b, c: (layer, 0, 0)),
            pl.BlockSpec((None, LANES, 1), lambda b, c: (layer, 0, 0)),
            pl.BlockSpec((None, 1, GDN_DV), lambda b, c: (layer, 0, 0)),
        ],
        out_specs=[
            pl.BlockSpec((CHUNK, zw), lambda b, c: (b * nc + c, 0)),
            pl.BlockSpec((None, GDN_HEADS, GDN_DK, GDN_DV), lambda b, c: (b, 0, 0, 0)),
        ],
        out_shape=[jax.ShapeDtypeStruct((p_main.shape[0], zw), BF16),
                   jax.ShapeDtypeStruct((nb, GDN_HEADS, GDN_DK, GDN_DV), F32)],
        scratch_shapes=[pltpu.VMEM((GDN_HEADS, GDN_DK, GDN_DV), F32), pltpu.VMEM((SUBLANES, GDN_QKV), F32)],
        compiler_params=_cparams(("arbitrary", "arbitrary")),
        name="gdn_prompt",
    )(p_main, p_main, p_small, p_small_t, cw, alr, dtr, alc, dtc, ng)


def _gdn_sample_kernel(x_ref, z_ref, sm_ref, smt_ref, sc_ref, cw_ref, alr_ref, dtr_ref, alc_ref, dtc_ref, ng_ref,
                       si_ref, *rest, lseg):
    y_ref, so_ref = rest[-2:]
    npair = x_ref.shape[0] // SAMPLE_SLAB
    cw = cw_ref[...]
    hs = range(GDN_HEADS)

    def pair(p, carry):
        r0 = pl.multiple_of(p * SAMPLE_SLAB, SAMPLE_SLAB)
        rows = pl.ds(r0, SAMPLE_SLAB)
        xs = _slab_pair(x_ref[rows, :])
        sms = _slab_pair(sm_ref[rows, :])
        zs = [_slab_pair(z_ref[rows, h * GDN_DV:(h + 1) * GDN_DV]) for h in hs]
        order = [(t, h) for t in range(2) for h in hs]
        convs, gates = [], []
        for t in range(2):
            xcat = jnp.concatenate([sc_ref[2 * p + t], xs[t]], axis=0)
            convs.append(_gdn_conv(xcat, cw))
            gates.append(_gdn_gates(sms[t], smt_ref[2 * p + t], alr_ref[...], dtr_ref[...],
                                    alc_ref[...], dtc_ref[...], SAMPLE_SLAB))
        qkv = [_gdn_split(convs[t], h) for t, h in order]
        ys, s_new = _gdn_chunks([a[0] for a in qkv], [a[1] for a in qkv], [a[2] for a in qkv],
                                [zs[h][t] for t, h in order], [si_ref[2 * p + t, h] for t, h in order],
                                [gates[t][0][:, S_DA + h:S_DA + h + 1] for t, h in order],
                                [gates[t][1][S_DA + h:S_DA + h + 1, :] for t, h in order],
                                [gates[t][2][:, S_DB + h:S_DB + h + 1] for t, h in order], ng_ref[...], lseg)
        for idx, (t, h) in enumerate(order):
            so_ref[2 * p + t, h] = s_new[idx]
        for h in hs:
            y_ref[rows, h * GDN_DV:(h + 1) * GDN_DV] = _slab_merge(ys[h], ys[GDN_HEADS + h]).astype(BF16)
        return carry

    lax.fori_loop(0, npair, pair, 0)


def _gdn_sample(p_main, p_small, p_small_t, sconv, cw, alr, dtr, alc, dtc, ng, state, so_prev, y_full, layer, row0,
                nseq, lseg):
    zw = GDN_HEADS * GDN_DV
    rb = SAMPLE_SEQS * lseg
    blk0 = row0 // rb
    st_in, st_specs, st_out_spec, out_shapes, aliases = _sample_io(state, so_prev, y_full, layer, GDN_HEADS,
                                                                   GDN_DK, GDN_DV, 11)
    return pl.pallas_call(
        functools.partial(_gdn_sample_kernel, lseg=lseg),
        grid=(nseq // SAMPLE_SEQS,),
        in_specs=[
            pl.BlockSpec((rb, GDN_QKV), lambda i: (blk0 + i, P_DQKV // GDN_QKV)),
            pl.BlockSpec((rb, zw), lambda i: (blk0 + i, P_DZ // zw)),
            pl.BlockSpec((rb, LANES), lambda i: (blk0 + i, 0)),
            pl.BlockSpec((SAMPLE_SEQS, LANES, SAMPLE_SLAB), lambda i: (i, 0, 0)),
            pl.BlockSpec((SAMPLE_SEQS, SUBLANES, GDN_QKV), lambda i: (i, 0, 0)),
            pl.BlockSpec((None, GDN_CONV, GDN_QKV), lambda i: (layer, 0, 0)),
            pl.BlockSpec((None, 1, LANES), lambda i: (layer, 0, 0)),
            pl.BlockSpec((None, 1, LANES), lambda i: (layer, 0, 0)),
            pl.BlockSpec((None, LANES, 1), lambda i: (layer, 0, 0)),
            pl.BlockSpec((None, LANES, 1), lambda i: (layer, 0, 0)),
            pl.BlockSpec((None, 1, GDN_DV), lambda i: (layer, 0, 0)),
        ] + st_specs,
        out_specs=[pl.BlockSpec((rb, zw), lambda i: (blk0 + i, 0)), st_out_spec],
        out_shape=out_shapes,
        input_output_aliases=aliases,
        compiler_params=_cparams(("arbitrary",)),
        name="gdn_sample",
    )(p_main, p_main, p_small, p_small_t, sconv, cw, alr, dtr, alc, dtc, ng, *st_in)


def _gla_scores(q, k, b, sub):
    c = q.shape[0]
    irow = _iota((sub, 1), 0)
    jrow_all = _iota((c, 1), 0)
    lane = _iota((sub, c), 1)
    blocks = []
    for blk in range(c // sub):
        r0 = blk * sub
        qi, ki, bi = q[r0:r0 + sub], k[r0:r0 + sub], b[r0:r0 + sub]
        if blk > 0:
            b0 = b[r0 - 1:r0, :]
            qt = qi * jnp.exp(bi - b0)
            kt = k * jnp.exp(jnp.where(jrow_all < r0, b0 - b, -jnp.inf))
            a = _dot_nt(qt, kt)
        else:
            a = jnp.zeros((sub, c), F32)
        for jj in range(sub):
            e = jnp.exp(jnp.where(irow >= jj, bi - bi[jj:jj + 1, :], -jnp.inf))
            col = jnp.sum(qi * e * ki[jj:jj + 1, :], axis=1, keepdims=True)
            a = jnp.where(lane == r0 + jj, col, a)
        blocks.append(a)
    return jnp.concatenate(blocks, axis=0) if len(blocks) > 1 else blocks[0]


def _gla_chunks(qs, ks, vs, zs, gks, ss, norm_g, nvalid, sub):
    n = range(len(qs))
    c = qs[0].shape[0]
    tril = _tril_incl(c)
    rowv = _iota((c, 1), 0) < nvalid
    qsc = [q * (GLA_DK ** -0.5) for q in qs]
    b = [_dot_mask_l(tril, gk) for gk in gks]
    o_st = [_dot(qsc[p] * jnp.exp(b[p]), ss[p]) for p in n]
    b_last = [b[p][nvalid - 1:nvalid, :] for p in n]
    kv = [_dot_tn(ks[p] * jnp.where(rowv, jnp.exp(b_last[p] - b[p]), 0.0), vs[p]) for p in n]
    a = [_gla_scores(qsc[p], ks[p], b[p], sub) for p in n]
    o = [o_st[p] + _dot(a[p], vs[p]) for p in n]
    s_new = [_row_to_col(jnp.exp(b_last[p])) * ss[p] + kv[p] for p in n]
    ys = [_rms(o[p]) * norm_g * _silu(zs[p]) for p in n]
    return ys, s_new


def _gla_gk(small, wup, bup):
    return jax.nn.log_sigmoid(_mm(small.astype(BF16), wup) + bup) / GLA_GATE_NORM


def _gla_prompt_kernel(q_ref, k_ref, v_ref, z_ref, sm_ref, wup_ref, bup_ref, ng_ref, y_ref, so_ref, s_ref):
    c = pl.program_id(1)

    @pl.when(c == 0)
    def _():
        s_ref[...] = jnp.zeros_like(s_ref)

    gk = _gla_gk(sm_ref[...], wup_ref[...], bup_ref[...])
    hs = range(GLA_HEADS)
    ys, s_new = _gla_chunks([q_ref[:, h * GLA_DK:(h + 1) * GLA_DK] for h in hs],
                            [k_ref[:, h * GLA_DK:(h + 1) * GLA_DK] for h in hs],
                            [v_ref[:, h * GLA_DV:(h + 1) * GLA_DV] for h in hs],
                            [z_ref[:, h * GLA_DV:(h + 1) * GLA_DV] for h in hs],
                            [gk[:, h * GLA_DK:(h + 1) * GLA_DK] for h in hs],
                            [s_ref[h] for h in hs], ng_ref[...], CHUNK, GLA_SUB)
    for h in hs:
        y_ref[:, h * GLA_DV:(h + 1) * GLA_DV] = ys[h].astype(BF16)
        s_ref[h] = s_new[h]

    @pl.when(c == pl.num_programs(1) - 1)
    def _():
        so_ref[...] = s_ref[...]


def _gla_prompt(p_main, p_small, wup, bup, ng, layer, nb, nc):
    qw = GLA_HEADS * GLA_DK
    vw = GLA_HEADS * GLA_DV
    return pl.pallas_call(
        _gla_prompt_kernel,
        grid=(nb, nc),
        in_specs=[
            pl.BlockSpec((CHUNK, qw), lambda b, c: (b * nc + c, P_LQ // qw)),
            pl.BlockSpec((CHUNK, qw), lambda b, c: (b * nc + c, P_LK // qw)),
            pl.BlockSpec((CHUNK, vw), lambda b, c: (b * nc + c, P_LV // vw)),
            pl.BlockSpec((CHUNK, vw), lambda b, c: (b * nc + c, P_LGT // vw)),
            pl.BlockSpec((CHUNK, LANES), lambda b, c: (b * nc + c, 0)),
            pl.BlockSpec((None, LANES, qw), lambda b, c: (layer, 0, 0)),
            pl.BlockSpec((None, 1, qw), lambda b, c: (layer, 0, 0)),
            pl.BlockSpec((None, 1, GLA_DV), lambda b, c: (layer, 0, 0)),
        ],
        out_specs=[
            pl.BlockSpec((CHUNK, vw), lambda b, c: (b * nc + c, 0)),
            pl.BlockSpec((None, GLA_HEADS, GLA_DK, GLA_DV), lambda b, c: (b, 0, 0, 0)),
        ],
        out_shape=[jax.ShapeDtypeStruct((p_main.shape[0], vw), BF16),
                   jax.ShapeDtypeStruct((nb, GLA_HEADS, GLA_DK, GLA_DV), F32)],
        scratch_shapes=[pltpu.VMEM((GLA_HEADS, GLA_DK, GLA_DV), F32)],
        compiler_params=_cparams(("arbitrary", "arbitrary")),
        name="gla_prompt",
    )(p_main, p_main, p_main, p_main, p_small, wup, bup, ng)


def _gla_sample_kernel(q_ref, k_ref, v_ref, z_ref, sm_ref, wup_ref, bup_ref, ng_ref, si_ref, *rest, lseg):
    y_ref, so_ref = rest[-2:]
    npair = q_ref.shape[0] // SAMPLE_SLAB
    hs = range(GLA_HEADS)

    def pair(p, carry):
        r0 = pl.multiple_of(p * SAMPLE_SLAB, SAMPLE_SLAB)
        rows = pl.ds(r0, SAMPLE_SLAB)
        gk = _gla_gk(sm_ref[rows, :], wup_ref[...], bup_ref[...])
        qs = [_slab_pair(q_ref[rows, h * GLA_DK:(h + 1) * GLA_DK]) for h in hs]
        ks = [_slab_pair(k_ref[rows, h * GLA_DK:(h + 1) * GLA_DK]) for h in hs]
        vs = [_slab_pair(v_ref[rows, h * GLA_DV:(h + 1) * GLA_DV]) for h in hs]
        zs = [_slab_pair(z_ref[rows, h * GLA_DV:(h + 1) * GLA_DV]) for h in hs]
        gs = [_slab_pair(gk[:, h * GLA_DK:(h + 1) * GLA_DK]) for h in hs]
        order = [(t, h) for t in range(2) for h in hs]
        ys, s_new = _gla_chunks([qs[h][t] for t, h in order], [ks[h][t] for t, h in order],
                                [vs[h][t] for t, h in order], [zs[h][t] for t, h in order],
                                [gs[h][t] for t, h in order], [si_ref[2 * p + t, h] for t, h in order],
                                ng_ref[...], lseg, SAMPLE_SLAB)
        for idx, (t, h) in enumerate(order):
            so_ref[2 * p + t, h] = s_new[idx]
        for h in hs:
            y_ref[rows, h * GLA_DV:(h + 1) * GLA_DV] = _slab_merge(ys[h], ys[GLA_HEADS + h]).astype(BF16)
        return carry

    lax.fori_loop(0, npair, pair, 0)


def _gla_sample(p_main, p_small, wup, bup, ng, state, so_prev, y_full, layer, row0, nseq, lseg):
    qw = GLA_HEADS * GLA_DK
    vw = GLA_HEADS * GLA_DV
    rb = SAMPLE_SEQS * lseg
    blk0 = row0 // rb
    st_in, st_specs, st_out_spec, out_shapes, aliases = _sample_io(state, so_prev, y_full, layer, GLA_HEADS,
                                                                   GLA_DK, GLA_DV, 8)
    return pl.pallas_call(
        functools.partial(_gla_sample_kernel, lseg=lseg),
        grid=(nseq // SAMPLE_SEQS,),
        in_specs=[
            pl.BlockSpec((rb, qw), lambda i: (blk0 + i, P_LQ // qw)),
            pl.BlockSpec((rb, qw), lambda i: (blk0 + i, P_LK // qw)),
            pl.BlockSpec((rb, vw), lambda i: (blk0 + i, P_LV // vw)),
            pl.BlockSpec((rb, vw), lambda i: (blk0 + i, P_LGT // vw)),
            pl.BlockSpec((rb, LANES), lambda i: (blk0 + i, 0)),
            pl.BlockSpec((None, LANES, qw), lambda i: (layer, 0, 0)),
            pl.BlockSpec((None, 1, qw), lambda i: (layer, 0, 0)),
            pl.BlockSpec((None, 1, GLA_DV), lambda i: (layer, 0, 0)),
        ] + st_specs,
        out_specs=[pl.BlockSpec((rb, vw), lambda i: (blk0 + i, 0)), st_out_spec],
        out_shape=out_shapes,
        input_output_aliases=aliases,
        compiler_params=_cparams(("arbitrary",)),
        name="gla_sample",
    )(p_main, p_main, p_main, p_main, p_small, wup, bup, ng, *st_in)


def _rope_tables(pos0, length):
    inv = 1.0 / (ROPE_BASE ** jnp.linspace(0.0, 1.0, RET_DK // 2, dtype=F32))
    ang = (jnp.arange(length, dtype=F32) + pos0)[:, None] * inv[None, :]
    cos = jnp.repeat(jnp.cos(ang), 2, axis=1)
    sin = jnp.stack([-jnp.sin(ang), jnp.sin(ang)], axis=-1).reshape(length, RET_DK)
    return cos, sin


def _lane_pad(v, off):
    n = v.shape[-1]
    return jnp.pad(v, ((0, 0), (off, LANES - off - n)))[:, None, :]


def kernel(x_prompt, x_sample, state_ret, state_gdn, state_gdn_conv, state_gla, norm_mix, norm_ffn, norm_final,
           w_in, b_merge, gdn_conv_w, gdn_a_log, gdn_dt_bias, gdn_norm, gla_w_up, gla_b_up, gla_norm, w_branch,
           w_o, w_gate_up, w_down):
    nb, seq, d = x_prompt.shape
    nsq, lseg, _ = x_sample.shape
    depth = w_in.shape[0]
    n_p = nb * seq
    n_s = nsq * lseg
    nc = seq // CHUNK
    tm = _dense_tiles(n_p + n_s)
    tm_out = math.gcd(n_p, n_s)
    assert seq % CHUNK == 0 and nsq % SAMPLE_SEQS == 0 and n_p % (SAMPLE_SEQS * lseg) == 0
    assert 2 * lseg == SAMPLE_SLAB and lseg >= GDN_CONV - 1

    w_segs = [w_in[..., a:b].astype(BF16) for a, b in W_SEGS]
    w_small = jnp.concatenate([w_in[..., a:b] for a, b in W_SMALL]
                              + [jnp.zeros(w_in.shape[:-1] + (LANES - 32,), w_in.dtype)], axis=-1).astype(BF16)
    wb = w_branch.astype(BF16)
    wo = w_o.astype(BF16)
    wgu = w_gate_up.astype(BF16)
    wdn = w_down.astype(BF16)
    g_mix = norm_mix[:, None, :]
    g_ffn = norm_ffn[:, None, :]
    bm = b_merge.reshape(depth, N_BRANCH, 1, D_MODEL)
    alr = _lane_pad(gdn_a_log, S_DA)
    dtr = _lane_pad(gdn_dt_bias, S_DA)
    alc = jnp.swapaxes(alr, 1, 2)
    dtc = jnp.swapaxes(dtr, 1, 2)
    gdn_ng = gdn_norm[:, None, :]
    gla_ng = gla_norm[:, None, :]
    wup = jnp.pad(gla_w_up, ((0, 0), (S_LLR, LANES - S_LLR - GLA_LOWRANK), (0, 0))).astype(BF16)
    bup = gla_b_up[:, None, :]
    cos_p, sin_p = _rope_tables(0.0, seq)
    cos_s, sin_s = _rope_tables(float(PAST_LEN), lseg)
    cos_s = jnp.tile(cos_s, (SAMPLE_SLAB // lseg, 1))
    sin_s = jnp.tile(sin_s, (SAMPLE_SLAB // lseg, 1))

    x = jnp.concatenate([x_prompt.reshape(n_p, d), x_sample.reshape(n_s, d)], axis=0)

    outs = {k: [] for k in ("p_ret", "p_gdn", "p_conv", "p_gla", "s_conv")}
    s_ret = s_gdn = s_gla = None
    for layer in range(depth):
        p_main, p_small = _in_proj(x, g_mix, w_segs, w_small, layer, tm, 1024)
        pst_p = jnp.swapaxes(p_small[:n_p].reshape(nb * nc, CHUNK, LANES), 1, 2)
        pst_s = jnp.swapaxes(jnp.pad(p_small[n_p:].reshape(nsq, lseg, LANES),
                                     ((0, 0), (0, SAMPLE_SLAB - lseg), (0, 0))), 1, 2)

        y_ret, st = _ret_prompt(p_main, cos_p, sin_p, nb, nc)
        outs["p_ret"].append(st)
        y_ret, s_ret = _ret_sample(p_main, cos_s, sin_s, state_ret, s_ret, y_ret, layer, n_p, nsq, lseg)

        y_gdn, st = _gdn_prompt(p_main, p_small, pst_p, gdn_conv_w, alr, dtr, alc, dtc, gdn_ng, layer, nb, nc)
        outs["p_gdn"].append(st)
        sconv = jnp.pad(state_gdn_conv[layer], ((0, 0), (SUBLANES - (GDN_CONV - 1), 0), (0, 0)))
        y_gdn, s_gdn = _gdn_sample(p_main, p_small, pst_s, sconv, gdn_conv_w, alr, dtr, alc, dtc, gdn_ng,
                                   state_gdn, s_gdn, y_gdn, layer, n_p, nsq, lseg)
        outs["p_conv"].append(jnp.stack([
            lax.slice(p_main, ((b + 1) * seq - (GDN_CONV - 1), P_DQKV), ((b + 1) * seq, P_DQKV + GDN_QKV))
            for b in range(nb)]))
        dq_s = lax.slice(p_main, (n_p, P_DQKV), (n_p + n_s, P_DQKV + GDN_QKV)).reshape(nsq, lseg, GDN_QKV)
        outs["s_conv"].append(jnp.concatenate([state_gdn_conv[layer], dq_s], axis=1)[:, -(GDN_CONV - 1):])

        y_gla, st = _gla_prompt(p_main, p_small, wup, bup, gla_ng, layer, nb, nc)
        outs["p_gla"].append(st)
        y_gla, s_gla = _gla_sample(p_main, p_small, wup, bup, gla_ng, state_gla, s_gla, y_gla, layer, n_p, nsq, lseg)

        mrg = _merge((y_ret, y_gdn, y_gla), wb, p_main, bm, layer, tm, 512)
        x = _out_proj(mrg, wo, x, layer, tm, 1024)
        x = _ffn(x, g_ffn, wgu, wdn, layer, tm, 512)

    g_fin = norm_final[None, :]
    y_p = _final_norm(x, g_fin, 0, n_p, tm_out)
    y_s = _final_norm(x, g_fin, n_p, n_s, tm_out)
    st = {k: jnp.stack(v) for k, v in outs.items()}
    return (y_p.reshape(nb, seq, d), y_s.reshape(nsq, lseg, d),
            st["p_ret"], st["p_gdn"], st["p_conv"], st["p_gla"],
            s_ret, s_gdn, st["s_conv"], s_gla)
```

```python
import functools
import math

import jax
import jax.numpy as jnp
from jax import lax
from jax.experimental import pallas as pl
from jax.experimental.pallas import tpu as pltpu

F32 = jnp.float32
BF16 = jnp.bfloat16

D_MODEL = 2048
RET_HEADS, RET_DK, RET_DV = 4, 128, 256
GDN_HEADS, GDN_DK, GDN_DV, GDN_CONV = 8, 128, 128, 4
GDN_QKV = GDN_HEADS * (2 * GDN_DK + GDN_DV)
GLA_HEADS, GLA_DK, GLA_DV, GLA_LOWRANK = 4, 128, 256, 16
GLA_GATE_NORM = 16.0
N_BRANCH = 3
BRANCH_WIDTH = 1024
EPS = 1e-6
ROPE_BASE = 10000.0
PAST_LEN = 16384

LANES = 128
SUBLANES = 8
CHUNK = 64
MIX_CHUNKS = 2
RET_CHUNK = 256
GLA_SUB = 16
SAMPLE_SLAB = 8
SAMPLE_SEQS = 8
VMEM_LIMIT = 56 * 1024 * 1024

W_SEGS = ((0, 6144), (6160, 9232), (9248, 16416))
W_SMALL = ((6144, 6160), (9232, 9248))
P_RQ, P_RK, P_RV, P_RG = 0, 512, 1024, 2048
P_DQKV, P_DZ = 3072, 6144
P_LQ, P_LK, P_LV, P_LGT, P_MG = 7168, 7680, 8192, 9216, 10240
P_MAIN = 16384
S_DA, S_DB, S_LLR = 0, 8, 16


def _cparams(sem):
    return pltpu.CompilerParams(dimension_semantics=sem, vmem_limit_bytes=VMEM_LIMIT)


def _dense_tiles(m):
    best = 64
    for t in range(64, 1089, 64):
        if m % t == 0:
            best = t
    return best


_mm = functools.partial(jnp.dot, preferred_element_type=F32)


def _dot(a, b):
    return _mm(a.astype(BF16), b.astype(BF16))


def _dot_nt(a, b):
    return lax.dot_general(a.astype(BF16), b.astype(BF16), (((1,), (1,)), ((), ())),
                           preferred_element_type=F32)


def _dot_tn(a, b):
    return lax.dot_general(a.astype(BF16), b.astype(BF16), (((0,), (0,)), ((), ())),
                           preferred_element_type=F32)


def _split2(x):
    hi = x.astype(BF16)
    lo = (x - hi.astype(F32)).astype(BF16)
    return hi, lo


def _split3(x):
    hi = x.astype(BF16)
    r = x - hi.astype(F32)
    mid = r.astype(BF16)
    lo = (r - mid.astype(F32)).astype(BF16)
    return hi, mid, lo


def _dot_hi_all(a_list, b_list):
    sa = [_split2(a) for a in a_list]
    sb = [_split2(b) for b in b_list]
    r0 = [_mm(a[0], b[0]) for a, b in zip(sa, sb)]
    r1 = [_mm(a[0], b[1]) for a, b in zip(sa, sb)]
    r2 = [_mm(a[1], b[0]) for a, b in zip(sa, sb)]
    return [x + (y + z) for x, y, z in zip(r0, r1, r2)]


def _dot_mask_l(m, x):
    mb = m.astype(BF16)
    hi, mid, lo = _split3(x)
    return _mm(mb, hi) + (_mm(mb, mid) + _mm(mb, lo))


def _dot_mask_r(x, m):
    mb = m.astype(BF16)
    hi, mid, lo = _split3(x)
    return _mm(hi, mb) + (_mm(mid, mb) + _mm(lo, mb))


def _iota(shape, dim):
    return lax.broadcasted_iota(jnp.int32, shape, dim)


def _silu(x):
    return x * jax.nn.sigmoid(x)


def _rms(x, eps=EPS):
    return x * lax.rsqrt(jnp.mean(x * x, axis=-1, keepdims=True) + eps)


def _row_to_col(r):
    n = r.shape[1]
    eye = _iota((n, n), 0) == _iota((n, n), 1)
    return jnp.sum(jnp.where(eye, jnp.broadcast_to(r, (n, n)), 0.0), axis=1, keepdims=True)


def _tril_incl(c):
    return (_iota((c, c), 1) <= _iota((c, c), 0)).astype(F32)


def _in_proj_kernel(x_ref, g_ref, wa_ref, wb_ref, wc_ref, ws_ref, p_ref, ps_ref, h_ref, *, na, nb):
    j = pl.program_id(1)

    @pl.when(j == 0)
    def _():
        h = (_rms(x_ref[...]) * g_ref[...]).astype(BF16)
        h_ref[...] = h
        ps_ref[...] = _mm(h, ws_ref[...])

    @pl.when(j < na)
    def _():
        p_ref[...] = _mm(h_ref[...], wa_ref[...])

    @pl.when((j >= na) & (j < na + nb))
    def _():
        p_ref[...] = _mm(h_ref[...], wb_ref[...])

    @pl.when(j >= na + nb)
    def _():
        p_ref[...] = _mm(h_ref[...], wc_ref[...])


def _in_proj(x, g, w_segs, w_small, layer, tm, tn):
    m, d = x.shape
    na, nb, nc = (w.shape[2] // tn for w in w_segs)
    return pl.pallas_call(
        functools.partial(_in_proj_kernel, na=na, nb=nb),
        grid=(m // tm, na + nb + nc),
        in_specs=[
            pl.BlockSpec((tm, d), lambda i, j: (i, 0), pipeline_mode=pl.Buffered(1)),
            pl.BlockSpec((None, 1, d), lambda i, j: (layer, 0, 0)),
            pl.BlockSpec((None, d, tn), lambda i, j: (layer, 0, jnp.minimum(j, na - 1))),
            pl.BlockSpec((None, d, tn), lambda i, j: (layer, 0, jnp.clip(j - na, 0, nb - 1))),
            pl.BlockSpec((None, d, tn), lambda i, j: (layer, 0, jnp.clip(j - na - nb, 0, nc - 1))),
            pl.BlockSpec((None, d, LANES), lambda i, j: (layer, 0, 0)),
        ],
        out_specs=[
            pl.BlockSpec((tm, tn), lambda i, j: (i, j)),
            pl.BlockSpec((tm, LANES), lambda i, j: (i, 0)),
        ],
        out_shape=[jax.ShapeDtypeStruct((m, (na + nb + nc) * tn), F32), jax.ShapeDtypeStruct((m, LANES), F32)],
        scratch_shapes=[pltpu.VMEM((tm, d), BF16)],
        compiler_params=_cparams(("arbitrary", "arbitrary")),
        name="in_proj",
    )(x, g, *w_segs, w_small)


def _merge_kernel(y0_ref, y1_ref, y2_ref, w0_ref, w1_ref, w2_ref, g0_ref, g1_ref, g2_ref,
                  b0_ref, b1_ref, b2_ref, o_ref):
    acc = None
    for y_ref, w_ref, g_ref, b_ref in ((y0_ref, w0_ref, g0_ref, b0_ref), (y1_ref, w1_ref, g1_ref, b1_ref),
                                       (y2_ref, w2_ref, g2_ref, b2_ref)):
        t = jax.nn.sigmoid(g_ref[...] + b_ref[...]) * _mm(y_ref[...], w_ref[...])
        acc = t if acc is None else acc + t
    o_ref[...] = acc.astype(BF16)


def _merge(ys, wb, p_main, b_merge, layer, tm, tn):
    m, bw = ys[0].shape
    d = wb.shape[3]
    nj = d // tn
    mg0 = P_MG // tn

    def y_spec():
        return pl.BlockSpec((tm, bw), lambda i, j: (i, 0), pipeline_mode=pl.Buffered(1))

    def w_spec(b):
        return pl.BlockSpec((None, None, bw, tn), lambda i, j: (layer, b, 0, j))

    def g_spec(b):
        return pl.BlockSpec((tm, tn), lambda i, j: (i, mg0 + b * nj + j))

    def b_spec(b):
        return pl.BlockSpec((None, None, 1, tn), lambda i, j: (layer, b, 0, j))

    rng = range(N_BRANCH)
    return pl.pallas_call(
        _merge_kernel,
        grid=(m // tm, nj),
        in_specs=[y_spec() for _ in rng] + [w_spec(b) for b in rng] + [g_spec(b) for b in rng]
        + [b_spec(b) for b in rng],
        out_specs=pl.BlockSpec((tm, tn), lambda i, j: (i, j)),
        out_shape=jax.ShapeDtypeStruct((m, d), BF16),
        compiler_params=_cparams(("arbitrary", "arbitrary")),
        name="merge",
    )(*ys, wb, wb, wb, p_main, p_main, p_main, b_merge, b_merge, b_merge)


def _out_proj_kernel(m_ref, w_ref, x_ref, o_ref):
    o_ref[...] = x_ref[...] + _mm(m_ref[...], w_ref[...])


def _out_proj(mrg, w_o, x, layer, tm, tn):
    m, d = x.shape
    return pl.pallas_call(
        _out_proj_kernel,
        grid=(m // tm, d // tn),
        in_specs=[
            pl.BlockSpec((tm, d), lambda i, j: (i, 0)),
            pl.BlockSpec((None, d, tn), lambda i, j: (layer, 0, j)),
            pl.BlockSpec((tm, tn), lambda i, j: (i, j)),
        ],
        out_specs=pl.BlockSpec((tm, tn), lambda i, j: (i, j)),
        out_shape=jax.ShapeDtypeStruct((m, d), F32),
        compiler_params=_cparams(("arbitrary", "arbitrary")),
        name="out_proj",
    )(mrg, w_o, x)


def _ffn_kernel(x_ref, g_ref, wg_ref, wu_ref, wd_ref, o_ref, h_ref):
    j = pl.program_id(1)

    @pl.when(j == 0)
    def _():
        x = x_ref[...]
        h_ref[...] = (_rms(x) * g_ref[...]).astype(BF16)
        o_ref[...] = x

    h = h_ref[...]
    act = (_silu(_mm(h, wg_ref[...])) * _mm(h, wu_ref[...])).astype(BF16)
    o_ref[...] += _mm(act, wd_ref[...])


def _ffn(x, g, w_gu, w_dn, layer, tm, tf):
    m, d = x.shape
    nf = w_dn.shape[1] // tf
    return pl.pallas_call(
        _ffn_kernel,
        grid=(m // tm, nf),
        in_specs=[
            pl.BlockSpec((tm, d), lambda i, j: (i, 0), pipeline_mode=pl.Buffered(1)),
            pl.BlockSpec((None, 1, d), lambda i, j: (layer, 0, 0)),
            pl.BlockSpec((None, d, tf), lambda i, j: (layer, 0, j)),
            pl.BlockSpec((None, d, tf), lambda i, j: (layer, 0, nf + j)),
            pl.BlockSpec((None, tf, d), lambda i, j: (layer, j, 0)),
        ],
        out_specs=pl.BlockSpec((tm, d), lambda i, j: (i, 0)),
        out_shape=jax.ShapeDtypeStruct((m, d), F32),
        scratch_shapes=[pltpu.VMEM((tm, d), BF16)],
        compiler_params=_cparams(("arbitrary", "arbitrary")),
        name="ffn",
    )(x, g, w_gu, w_gu, w_dn)


def _final_norm_kernel(x_ref, g_ref, o_ref):
    o_ref[...] = _rms(x_ref[...]) * g_ref[...]


def _final_norm(x, g, row0, rows, tm):
    d = x.shape[1]
    blk0 = row0 // tm
    return pl.pallas_call(
        _final_norm_kernel,
        grid=(rows // tm,),
        in_specs=[pl.BlockSpec((tm, d), lambda i: (blk0 + i, 0)), pl.BlockSpec((1, d), lambda i: (0, 0))],
        out_specs=pl.BlockSpec((tm, d), lambda i: (i, 0)),
        out_shape=jax.ShapeDtypeStruct((rows, d), F32),
        compiler_params=_cparams(("arbitrary",)),
        name="final_norm",
    )(x, g)


def _slab_pair(x):
    return x, pltpu.roll(x, SAMPLE_SLAB // 2, axis=0)


def _slab_merge(ya, yb):
    top = _iota(ya.shape, 0) < SAMPLE_SLAB // 2
    return jnp.where(top, ya, pltpu.roll(yb, SAMPLE_SLAB // 2, axis=0))


def _sample_io(state, so_prev, y_full, layer, heads, dk, dv, n_fixed):
    blk = pl.BlockSpec((None, SAMPLE_SEQS, heads, dk, dv), lambda i: (layer, i, 0, 0, 0))
    ins, specs = [state, y_full], [blk, pl.BlockSpec(memory_space=pl.ANY)]
    aliases = {n_fixed + 1: 0}
    if so_prev is not None:
        ins.append(so_prev)
        specs.append(pl.BlockSpec(memory_space=pl.ANY))
        aliases[n_fixed + 2] = 1
    out_shapes = [jax.ShapeDtypeStruct(y_full.shape, y_full.dtype), jax.ShapeDtypeStruct(state.shape, F32)]
    return ins, specs, blk, out_shapes, aliases


def _rotary(t, cos, sin_signed):
    even = (_iota(t.shape, 1) & 1) == 0
    nxt = pltpu.roll(t, LANES - 1, axis=1)
    prv = pltpu.roll(t, 1, axis=1)
    return t * cos + jnp.where(even, nxt, prv) * sin_signed


def _ret_decays(c, nvalid, lg):
    i = _iota((c, c), 0)
    j = _iota((c, c), 1)
    diff = i - j
    d_intra = jnp.where(diff >= 0, jnp.exp(lg * jnp.maximum(diff, 0).astype(F32)), 0.0)
    pos_i = _iota((c, 1), 0)
    pos = pos_i.astype(F32)
    d_q = jnp.exp(lg * (pos + 1.0))
    d_k = jnp.where(pos_i < nvalid, jnp.exp(lg * (nvalid - 1.0 - pos)), 0.0)
    d_c = math.exp(lg * nvalid)
    return d_intra, d_q, d_k, d_c


def _ret_lg(h):
    return math.log1p(-(2.0 ** (-5.0 - h)))


def _ret_chunks(qs, ks, vs, gs, ss, cos, sin, decs):
    n = range(len(qs))
    qr = [_rotary(q, cos, sin) for q in qs]
    kr = [_rotary(k, cos, sin) * (RET_DK ** -0.5) for k in ks]
    sc = [_dot_nt(qr[p], kr[p]) * decs[p][0] for p in n]
    o_st = [_dot(qr[p] * decs[p][1], ss[p]) for p in n]
    kv = [_dot_tn(kr[p] * decs[p][2], vs[p]) for p in n]
    o = [_dot(sc[p], vs[p]) + o_st[p] for p in n]
    s_new = [decs[p][3] * ss[p] + kv[p] for p in n]
    ys = [_rms(o[p]) * _silu(gs[p]) for p in n]
    return ys, s_new


def _ret_prompt_kernel(q_ref, k_ref, v_ref, g_ref, cos_ref, sin_ref, y_ref, so_ref, s_ref):
    c = pl.program_id(1)

    @pl.when(c == 0)
    def _():
        s_ref[...] = jnp.zeros_like(s_ref)

    hs = range(RET_HEADS)
    decs = [_ret_decays(RET_CHUNK, RET_CHUNK, _ret_lg(h)) for h in hs]
    ys, s_new = _ret_chunks([q_ref[:, h * RET_DK:(h + 1) * RET_DK] for h in hs],
                            [k_ref[:, h * RET_DK:(h + 1) * RET_DK] for h in hs],
                            [v_ref[:, h * RET_DV:(h + 1) * RET_DV] for h in hs],
                            [g_ref[:, h * RET_DV:(h + 1) * RET_DV] for h in hs],
                            [s_ref[h] for h in hs], cos_ref[...], sin_ref[...], decs)
    for h in hs:
        y_ref[:, h * RET_DV:(h + 1) * RET_DV] = ys[h].astype(BF16)
        s_ref[h] = s_new[h]

    @pl.when(c == pl.num_programs(1) - 1)
    def _():
        so_ref[...] = s_ref[...]


def _ret_prompt(p_main, cos, sin, nb, nc):
    qw = RET_HEADS * RET_DK
    vw = RET_HEADS * RET_DV
    return pl.pallas_call(
        _ret_prompt_kernel,
        grid=(nb, nc),
        in_specs=[
            pl.BlockSpec((RET_CHUNK, qw), lambda b, c: (b * nc + c, P_RQ // qw)),
            pl.BlockSpec((RET_CHUNK, qw), lambda b, c: (b * nc + c, P_RK // qw)),
            pl.BlockSpec((RET_CHUNK, vw), lambda b, c: (b * nc + c, P_RV // vw)),
            pl.BlockSpec((RET_CHUNK, vw), lambda b, c: (b * nc + c, P_RG // vw)),
            pl.BlockSpec((RET_CHUNK, LANES), lambda b, c: (c, 0)),
            pl.BlockSpec((RET_CHUNK, LANES), lambda b, c: (c, 0)),
        ],
        out_specs=[
            pl.BlockSpec((RET_CHUNK, vw), lambda b, c: (b * nc + c, 0)),
            pl.BlockSpec((None, RET_HEADS, RET_DK, RET_DV), lambda b, c: (b, 0, 0, 0)),
        ],
        out_shape=[jax.ShapeDtypeStruct((p_main.shape[0], vw), BF16),
                   jax.ShapeDtypeStruct((nb, RET_HEADS, RET_DK, RET_DV), F32)],
        scratch_shapes=[pltpu.VMEM((RET_HEADS, RET_DK, RET_DV), F32)],
        compiler_params=_cparams(("arbitrary", "arbitrary")),
        name="ret_prompt",
    )(p_main, p_main, p_main, p_main, cos, sin)


def _ret_sample_kernel(q_ref, k_ref, v_ref, g_ref, cos_ref, sin_ref, si_ref, *rest, lseg):
    y_ref, so_ref = rest[-2:]
    npair = q_ref.shape[0] // SAMPLE_SLAB
    cos = cos_ref[...]
    sin = sin_ref[...]
    hs = range(RET_HEADS)
    decs = [_ret_decays(SAMPLE_SLAB, lseg, _ret_lg(h)) for h in hs] * 2

    def pair(p, carry):
        r0 = pl.multiple_of(p * SAMPLE_SLAB, SAMPLE_SLAB)
        rows = pl.ds(r0, SAMPLE_SLAB)
        qs = [_slab_pair(q_ref[rows, h * RET_DK:(h + 1) * RET_DK]) for h in hs]
        ks = [_slab_pair(k_ref[rows, h * RET_DK:(h + 1) * RET_DK]) for h in hs]
        vs = [_slab_pair(v_ref[rows, h * RET_DV:(h + 1) * RET_DV]) for h in hs]
        gs = [_slab_pair(g_ref[rows, h * RET_DV:(h + 1) * RET_DV]) for h in hs]
        order = [(t, h) for t in range(2) for h in hs]
        ys, s_new = _ret_chunks([qs[h][t] for t, h in order], [ks[h][t] for t, h in order],
                                [vs[h][t] for t, h in order], [gs[h][t] for t, h in order],
                                [si_ref[2 * p + t, h] for t, h in order], cos, sin, decs)
        for idx, (t, h) in enumerate(order):
            so_ref[2 * p + t, h] = s_new[idx]
        for h in hs:
            y_ref[rows, h * RET_DV:(h + 1) * RET_DV] = _slab_merge(ys[h], ys[RET_HEADS + h]).astype(BF16)
        return carry

    lax.fori_loop(0, npair, pair, 0)


def _ret_sample(p_main, cos, sin, state, so_prev, y_full, layer, row0, nseq, lseg):
    qw = RET_HEADS * RET_DK
    vw = RET_HEADS * RET_DV
    rb = SAMPLE_SEQS * lseg
    blk0 = row0 // rb
    st_in, st_specs, st_out_spec, out_shapes, aliases = _sample_io(state, so_prev, y_full, layer, RET_HEADS,
                                                                   RET_DK, RET_DV, 6)
    return pl.pallas_call(
        functools.partial(_ret_sample_kernel, lseg=lseg),
        grid=(nseq // SAMPLE_SEQS,),
        in_specs=[
            pl.BlockSpec((rb, qw), lambda i: (blk0 + i, P_RQ // qw)),
            pl.BlockSpec((rb, qw), lambda i: (blk0 + i, P_RK // qw)),
            pl.BlockSpec((rb, vw), lambda i: (blk0 + i, P_RV // vw)),
            pl.BlockSpec((rb, vw), lambda i: (blk0 + i, P_RG // vw)),
            pl.BlockSpec((SAMPLE_SLAB, LANES), lambda i: (0, 0)),
            pl.BlockSpec((SAMPLE_SLAB, LANES), lambda i: (0, 0)),
        ] + st_specs,
        out_specs=[pl.BlockSpec((rb, vw), lambda i: (blk0 + i, 0)), st_out_spec],
        out_shape=out_shapes,
        input_output_aliases=aliases,
        compiler_params=_cparams(("arbitrary",)),
        name="ret_sample",
    )(p_main, p_main, p_main, p_main, cos, sin, *st_in)


def _unit_lower_inverse_all(a_list, nvalid):
    c = a_list[0].shape[0]
    n = range(len(a_list))
    i = _iota((c, c), 0)
    j = _iota((c, c), 1)
    eye = (i == j).astype(F32)
    pair = (i >> 1) == (j >> 1)
    ts = [eye - jnp.where(pair, a, 0.0) for a in a_list]
    blk = 2
    while blk < min(c, nvalid):
        sh = blk.bit_length()
        m = ((i >> sh) == (j >> sh)) & ((i & blk) != 0) & ((j & blk) == 0)
        xs = [jnp.where(m, a, 0.0).astype(BF16) for a in a_list]
        tsp = [_split2(t) for t in ts]
        tx = [_mm(tsp[p][0], xs[p]) + _mm(tsp[p][1], xs[p]) for p in n]
        txs = [_split2(y) for y in tx]
        r0 = [_mm(txs[p][0], tsp[p][0]) for p in n]
        r1 = [_mm(txs[p][0], tsp[p][1]) for p in n]
        r2 = [_mm(txs[p][1], tsp[p][0]) for p in n]
        ts = [ts[p] - (r0[p] + (r1[p] + r2[p])) for p in n]
        blk *= 2
    return ts


def _gdn_gates(small, small_t, alog_r, dtb_r, alog_c, dtb_c, c):
    r = small.shape[0]
    g = -jnp.exp(alog_r) * jax.nn.softplus(small + dtb_r)
    g_t = -jnp.exp(alog_c) * jax.nn.softplus(small_t + dtb_c)
    i = _iota((r, r), 0)
    j = _iota((r, r), 1)
    sh = c.bit_length() - 1
    same = ((i >> sh) == (j >> sh)) if r > c else True
    cum = _dot_mask_l((same & (j <= i)).astype(F32), g)
    cum_t = _dot_mask_r(g_t, (same & (i <= j)).astype(F32))
    beta = jax.nn.sigmoid(small)
    return cum, cum_t, beta


def _gdn_prep(qs, ks, vs, cum_cs, cum_rs, beta_cs, nvalid):
    n = range(len(qs))
    c = qs[0].shape[0]
    i = _iota((c, c), 0)
    j = _iota((c, c), 1)
    rowv = _iota((c, 1), 0) < nvalid
    qn = [q * lax.rsqrt(jnp.sum(q * q, axis=-1, keepdims=True) + EPS) * (GDN_DK ** -0.5) for q in qs]
    kn = [k * lax.rsqrt(jnp.sum(k * k, axis=-1, keepdims=True) + EPS) for k in ks]
    dec = [jnp.exp(jnp.where(j <= i, cum_cs[p] - cum_rs[p], -jnp.inf)) for p in n]
    e_c = [jnp.exp(cum_cs[p]) for p in n]
    kb = [kn[p] * beta_cs[p] for p in n]
    kk = [_dot_nt(kb[p], kn[p]) for p in n]
    qk = [_dot_nt(qn[p], kn[p]) for p in n]
    lower = [jnp.where(j < i, kk[p] * dec[p], 0.0) for p in n]
    ts = _unit_lower_inverse_all(lower, nvalid)
    rhs = [jnp.concatenate([vs[p] * beta_cs[p], kb[p] * e_c[p]], axis=1) for p in n]
    sol = [_dot(ts[p], rhs[p]) for p in n]
    cum_last = [cum_cs[p][nvalid - 1:nvalid, :] for p in n]
    out = []
    for p in n:
        wq = jnp.concatenate([sol[p][:, GDN_DV:], qn[p] * e_c[p]], axis=0).astype(BF16)
        kd = (kn[p] * jnp.where(rowv, jnp.exp(cum_last[p] - cum_cs[p]), 0.0)).astype(BF16)
        out.append((wq, sol[p][:, :GDN_DV], (qk[p] * dec[p]).astype(BF16), kd, jnp.exp(cum_last[p])))
    return out


def _gdn_seq(prep, zs, ss, norm_g):
    n = range(len(prep))
    c = prep[0][1].shape[0]
    st = [_mm(prep[p][0], ss[p].astype(BF16)) for p in n]
    v_new = [prep[p][1] - st[p][:c] for p in n]
    o = [st[p][c:] + _dot(prep[p][2], v_new[p]) for p in n]
    kv = [_dot_tn(prep[p][3], v_new[p]) for p in n]
    s_new = [prep[p][4] * ss[p] + kv[p] for p in n]
    ys = [_rms(o[p]) * norm_g * _silu(zs[p]) for p in n]
    return ys, s_new


def _gdn_conv(xcat, cw):
    acc = xcat[SUBLANES:, :] * cw[GDN_CONV - 1:GDN_CONV, :]
    for w in range(GDN_CONV - 1):
        sh = GDN_CONV - 1 - w
        acc = acc + pltpu.roll(xcat, sh, axis=0)[SUBLANES:, :] * cw[w:w + 1, :]
    return _silu(acc)


def _gdn_split(conv, h):
    hk = GDN_HEADS * GDN_DK
    return (conv[:, h * GDN_DK:(h + 1) * GDN_DK], conv[:, hk + h * GDN_DK: hk + (h + 1) * GDN_DK],
            conv[:, 2 * hk + h * GDN_DV: 2 * hk + (h + 1) * GDN_DV])


def _gdn_prompt_kernel(x_ref, z_ref, sm_ref, smt_ref, cw_ref, alr_ref, dtr_ref, alc_ref, dtc_ref, ng_ref,
                       y_ref, so_ref, s_ref, halo_ref):
    c = pl.program_id(1)

    @pl.when(c == 0)
    def _():
        s_ref[...] = jnp.zeros_like(s_ref)
        halo_ref[...] = jnp.zeros_like(halo_ref)

    x = x_ref[...]
    rows = x.shape[0]
    xcat = jnp.concatenate([halo_ref[...], x], axis=0)
    halo_ref[...] = x[rows - SUBLANES:, :]
    conv = _gdn_conv(xcat, cw_ref[...])
    cum, cum_t, beta = _gdn_gates(sm_ref[...], smt_ref[...], alr_ref[...], dtr_ref[...],
                                  alc_ref[...], dtc_ref[...], CHUNK)
    hs = range(GDN_HEADS)
    order = [(ch, h) for ch in range(rows // CHUNK) for h in hs]

    def rws(a, ch):
        return a[ch * CHUNK:(ch + 1) * CHUNK]

    qkv = [_gdn_split(rws(conv, ch), h) for ch, h in order]
    prep = _gdn_prep([t[0] for t in qkv], [t[1] for t in qkv], [t[2] for t in qkv],
                     [rws(cum, ch)[:, S_DA + h:S_DA + h + 1] for ch, h in order],
                     [cum_t[S_DA + h:S_DA + h + 1, ch * CHUNK:(ch + 1) * CHUNK] for ch, h in order],
                     [rws(beta, ch)[:, S_DB + h:S_DB + h + 1] for ch, h in order], CHUNK)
    ss = [s_ref[h] for h in hs]
    for ch in range(rows // CHUNK):
        ys, ss = _gdn_seq(prep[ch * GDN_HEADS:(ch + 1) * GDN_HEADS],
                          [z_ref[ch * CHUNK:(ch + 1) * CHUNK, h * GDN_DV:(h + 1) * GDN_DV] for h in hs],
                          ss, ng_ref[...])
        for h in hs:
            y_ref[ch * CHUNK:(ch + 1) * CHUNK, h * GDN_DV:(h + 1) * GDN_DV] = ys[h].astype(BF16)
    for h in hs:
        s_ref[h] = ss[h]

    @pl.when(c == pl.num_programs(1) - 1)
    def _():
        so_ref[...] = s_ref[...]


def _gdn_prompt(p_main, p_small, p_small_t, cw, alr, dtr, alc, dtc, ng, layer, nb, nc):
    zw = GDN_HEADS * GDN_DV
    rb = MIX_CHUNKS * CHUNK
    return pl.pallas_call(
        _gdn_prompt_kernel,
        grid=(nb, nc),
        in_specs=[
            pl.BlockSpec((rb, GDN_QKV), lambda b, c: (b * nc + c, P_DQKV // GDN_QKV)),
            pl.BlockSpec((rb, zw), lambda b, c: (b * nc + c, P_DZ // zw)),
            pl.BlockSpec((rb, LANES), lambda b, c: (b * nc + c, 0)),
            pl.BlockSpec((None, LANES, rb), lambda b, c: (b * nc + c, 0, 0)),
            pl.BlockSpec((None, GDN_CONV, GDN_QKV), lambda b, c: (layer, 0, 0)),
            pl.BlockSpec((None, 1, LANES), lambda b, c: (layer, 0, 0)),
            pl.BlockSpec((None, 1, LANES), lambda b, c: (layer, 0, 0)),
            pl.BlockSpec((None, LANES, 1), lambda b, c: (layer, 0, 0)),
            pl.BlockSpec((None, LANES, 1), lambda b, c: (layer, 0, 0)),
            pl.BlockSpec((None, 1, GDN_DV), lambda b, c: (layer, 0, 0)),
        ],
        out_specs=[
            pl.BlockSpec((rb, zw), lambda b, c: (b * nc + c, 0)),
            pl.BlockSpec((None, GDN_HEADS, GDN_DK, GDN_DV), lambda b, c: (b, 0, 0, 0)),
        ],
        out_shape=[jax.ShapeDtypeStruct((p_main.shape[0], zw), BF16),
                   jax.ShapeDtypeStruct((nb, GDN_HEADS, GDN_DK, GDN_DV), F32)],
        scratch_shapes=[pltpu.VMEM((GDN_HEADS, GDN_DK, GDN_DV), F32), pltpu.VMEM((SUBLANES, GDN_QKV), F32)],
        compiler_params=_cparams(("arbitrary", "arbitrary")),
        name="gdn_prompt",
    )(p_main, p_main, p_small, p_small_t, cw, alr, dtr, alc, dtc, ng)


def _gdn_sample_kernel(x_ref, z_ref, sm_ref, smt_ref, sc_ref, cw_ref, alr_ref, dtr_ref, alc_ref, dtc_ref, ng_ref,
                       si_ref, *rest, lseg):
    y_ref, so_ref = rest[-2:]
    npair = x_ref.shape[0] // SAMPLE_SLAB
    cw = cw_ref[...]
    hs = range(GDN_HEADS)

    def pair(p, carry):
        r0 = pl.multiple_of(p * SAMPLE_SLAB, SAMPLE_SLAB)
        rows = pl.ds(r0, SAMPLE_SLAB)
        xs = _slab_pair(x_ref[rows, :])
        sms = _slab_pair(sm_ref[rows, :])
        zs = [_slab_pair(z_ref[rows, h * GDN_DV:(h + 1) * GDN_DV]) for h in hs]
        order = [(t, h) for t in range(2) for h in hs]
        convs, gates = [], []
        for t in range(2):
            xcat = jnp.concatenate([sc_ref[2 * p + t], xs[t]], axis=0)
            convs.append(_gdn_conv(xcat, cw))
            gates.append(_gdn_gates(sms[t], smt_ref[2 * p + t], alr_ref[...], dtr_ref[...],
                                    alc_ref[...], dtc_ref[...], SAMPLE_SLAB))
        qkv = [_gdn_split(convs[t], h) for t, h in order]
        prep = _gdn_prep([a[0] for a in qkv], [a[1] for a in qkv], [a[2] for a in qkv],
                         [gates[t][0][:, S_DA + h:S_DA + h + 1] for t, h in order],
                         [gates[t][1][S_DA + h:S_DA + h + 1, :] for t, h in order],
                         [gates[t][2][:, S_DB + h:S_DB + h + 1] for t, h in order], lseg)
        ys, s_new = _gdn_seq(prep, [zs[h][t] for t, h in order], [si_ref[2 * p + t, h] for t, h in order],
                             ng_ref[...])
        for idx, (t, h) in enumerate(order):
            so_ref[2 * p + t, h] = s_new[idx]
        for h in hs:
            y_ref[rows, h * GDN_DV:(h + 1) * GDN_DV] = _slab_merge(ys[h], ys[GDN_HEADS + h]).astype(BF16)
        return carry

    lax.fori_loop(0, npair, pair, 0)


def _gdn_sample(p_main, p_small, p_small_t, sconv, cw, alr, dtr, alc, dtc, ng, state, so_prev, y_full, layer, row0,
                nseq, lseg):
    zw = GDN_HEADS * GDN_DV
    rb = SAMPLE_SEQS * lseg
    blk0 = row0 // rb
    st_in, st_specs, st_out_spec, out_shapes, aliases = _sample_io(state, so_prev, y_full, layer, GDN_HEADS,
                                                                   GDN_DK, GDN_DV, 11)
    return pl.pallas_call(
        functools.partial(_gdn_sample_kernel, lseg=lseg),
        grid=(nseq // SAMPLE_SEQS,),
        in_specs=[
            pl.BlockSpec((rb, GDN_QKV), lambda i: (blk0 + i, P_DQKV // GDN_QKV)),
            pl.BlockSpec((rb, zw), lambda i: (blk0 + i, P_DZ // zw)),
            pl.BlockSpec((rb, LANES), lambda i: (blk0 + i, 0)),
            pl.BlockSpec((SAMPLE_SEQS, LANES, SAMPLE_SLAB), lambda i: (i, 0, 0)),
            pl.BlockSpec((SAMPLE_SEQS, SUBLANES, GDN_QKV), lambda i: (i, 0, 0)),
            pl.BlockSpec((None, GDN_CONV, GDN_QKV), lambda i: (layer, 0, 0)),
            pl.BlockSpec((None, 1, LANES), lambda i: (layer, 0, 0)),
            pl.BlockSpec((None, 1, LANES), lambda i: (layer, 0, 0)),
            pl.BlockSpec((None, LANES, 1), lambda i: (layer, 0, 0)),
            pl.BlockSpec((None, LANES, 1), lambda i: (layer, 0, 0)),
            pl.BlockSpec((None, 1, GDN_DV), lambda i: (layer, 0, 0)),
        ] + st_specs,
        out_specs=[pl.BlockSpec((rb, zw), lambda i: (blk0 + i, 0)), st_out_spec],
        out_shape=out_shapes,
        input_output_aliases=aliases,
        compiler_params=_cparams(("arbitrary",)),
        name="gdn_sample",
    )(p_main, p_main, p_small, p_small_t, sconv, cw, alr, dtr, alc, dtc, ng, *st_in)


def _gla_scores(q, k, b, sub):
    c = q.shape[0]
    irow = _iota((sub, 1), 0)
    jrow_all = _iota((c, 1), 0)
    lane = _iota((sub, c), 1)
    blocks = []
    for blk in range(c // sub):
        r0 = blk * sub
        qi, ki, bi = q[r0:r0 + sub], k[r0:r0 + sub], b[r0:r0 + sub]
        if blk > 0:
            b0 = b[r0 - 1:r0, :]
            qt = qi * jnp.exp(bi - b0)
            kt = k * jnp.exp(jnp.where(jrow_all < r0, b0 - b, -jnp.inf))
            a = _dot_nt(qt, kt)
        else:
            a = jnp.zeros((sub, c), F32)
        for jj in range(sub):
            e = jnp.exp(jnp.where(irow >= jj, bi - bi[jj:jj + 1, :], -jnp.inf))
            col = jnp.sum(qi * e * ki[jj:jj + 1, :], axis=1, keepdims=True)
            a = jnp.where(lane == r0 + jj, col, a)
        blocks.append(a)
    return jnp.concatenate(blocks, axis=0) if len(blocks) > 1 else blocks[0]


def _gla_prep(qs, ks, vs, bs, nvalid, sub):
    n = range(len(qs))
    c = qs[0].shape[0]
    rowv = _iota((c, 1), 0) < nvalid
    qsc = [q * (GLA_DK ** -0.5) for q in qs]
    b_last = [bs[p][nvalid - 1:nvalid, :] for p in n]
    kv = [_dot_tn(ks[p] * jnp.where(rowv, jnp.exp(b_last[p] - bs[p]), 0.0), vs[p]) for p in n]
    a = [_gla_scores(qsc[p], ks[p], bs[p], sub) for p in n]
    o_in = [_dot(a[p], vs[p]) for p in n]
    return [((qsc[p] * jnp.exp(bs[p])).astype(BF16), o_in[p], kv[p], _row_to_col(jnp.exp(b_last[p]))) for p in n]


def _gla_seq(prep, zs, ss, norm_g):
    n = range(len(prep))
    o = [prep[p][1] + _mm(prep[p][0], ss[p].astype(BF16)) for p in n]
    s_new = [prep[p][3] * ss[p] + prep[p][2] for p in n]
    ys = [_rms(o[p]) * norm_g * _silu(zs[p]) for p in n]
    return ys, s_new


def _gla_cumsum(gk, c):
    r = gk.shape[0]
    i = _iota((r, r), 0)
    j = _iota((r, r), 1)
    sh = c.bit_length() - 1
    same = ((i >> sh) == (j >> sh)) if r > c else True
    return _dot_mask_l((same & (j <= i)).astype(F32), gk)


def _gla_gk(small, wup, bup):
    return jax.nn.log_sigmoid(_mm(small.astype(BF16), wup) + bup) / GLA_GATE_NORM


def _gla_prompt_kernel(q_ref, k_ref, v_ref, z_ref, sm_ref, wup_ref, bup_ref, ng_ref, y_ref, so_ref, s_ref):
    c = pl.program_id(1)

    @pl.when(c == 0)
    def _():
        s_ref[...] = jnp.zeros_like(s_ref)

    rows = q_ref.shape[0]
    b_all = _gla_cumsum(_gla_gk(sm_ref[...], wup_ref[...], bup_ref[...]), CHUNK)
    hs = range(GLA_HEADS)
    order = [(ch, h) for ch in range(rows // CHUNK) for h in hs]

    def blk(ref, ch, h, w):
        return ref[ch * CHUNK:(ch + 1) * CHUNK, h * w:(h + 1) * w]

    prep = _gla_prep([blk(q_ref, ch, h, GLA_DK) for ch, h in order], [blk(k_ref, ch, h, GLA_DK) for ch, h in order],
                     [blk(v_ref, ch, h, GLA_DV) for ch, h in order], [blk(b_all, ch, h, GLA_DK) for ch, h in order],
                     CHUNK, GLA_SUB)
    ss = [s_ref[h] for h in hs]
    for ch in range(rows // CHUNK):
        ys, ss = _gla_seq(prep[ch * GLA_HEADS:(ch + 1) * GLA_HEADS], [blk(z_ref, ch, h, GLA_DV) for h in hs],
                          ss, ng_ref[...])
        for h in hs:
            y_ref[ch * CHUNK:(ch + 1) * CHUNK, h * GLA_DV:(h + 1) * GLA_DV] = ys[h].astype(BF16)
    for h in hs:
        s_ref[h] = ss[h]

    @pl.when(c == pl.num_programs(1) - 1)
    def _():
        so_ref[...] = s_ref[...]


def _gla_prompt(p_main, p_small, wup, bup, ng, layer, nb, nc):
    qw = GLA_HEADS * GLA_DK
    vw = GLA_HEADS * GLA_DV
    rb = MIX_CHUNKS * CHUNK
    return pl.pallas_call(
        _gla_prompt_kernel,
        grid=(nb, nc),
        in_specs=[
            pl.BlockSpec((rb, qw), lambda b, c: (b * nc + c, P_LQ // qw)),
            pl.BlockSpec((rb, qw), lambda b, c: (b * nc + c, P_LK // qw)),
            pl.BlockSpec((rb, vw), lambda b, c: (b * nc + c, P_LV // vw)),
            pl.BlockSpec((rb, vw), lambda b, c: (b * nc + c, P_LGT // vw)),
            pl.BlockSpec((rb, LANES), lambda b, c: (b * nc + c, 0)),
            pl.BlockSpec((None, LANES, qw), lambda b, c: (layer, 0, 0)),
            pl.BlockSpec((None, 1, qw), lambda b, c: (layer, 0, 0)),
            pl.BlockSpec((None, 1, GLA_DV), lambda b, c: (layer, 0, 0)),
        ],
        out_specs=[
            pl.BlockSpec((rb, vw), lambda b, c: (b * nc + c, 0)),
            pl.BlockSpec((None, GLA_HEADS, GLA_DK, GLA_DV), lambda b, c: (b, 0, 0, 0)),
        ],
        out_shape=[jax.ShapeDtypeStruct((p_main.shape[0], vw), BF16),
                   jax.ShapeDtypeStruct((nb, GLA_HEADS, GLA_DK, GLA_DV), F32)],
        scratch_shapes=[pltpu.VMEM((GLA_HEADS, GLA_DK, GLA_DV), F32)],
        compiler_params=_cparams(("arbitrary", "arbitrary")),
        name="gla_prompt",
    )(p_main, p_main, p_main, p_main, p_small, wup, bup, ng)


def _gla_sample_kernel(q_ref, k_ref, v_ref, z_ref, sm_ref, wup_ref, bup_ref, ng_ref, si_ref, *rest, lseg):
    y_ref, so_ref = rest[-2:]
    npair = q_ref.shape[0] // SAMPLE_SLAB
    hs = range(GLA_HEADS)

    def pair(p, carry):
        r0 = pl.multiple_of(p * SAMPLE_SLAB, SAMPLE_SLAB)
        rows = pl.ds(r0, SAMPLE_SLAB)
        gk = _gla_gk(sm_ref[rows, :], wup_ref[...], bup_ref[...])
        qs = [_slab_pair(q_ref[rows, h * GLA_DK:(h + 1) * GLA_DK]) for h in hs]
        ks = [_slab_pair(k_ref[rows, h * GLA_DK:(h + 1) * GLA_DK]) for h in hs]
        vs = [_slab_pair(v_ref[rows, h * GLA_DV:(h + 1) * GLA_DV]) for h in hs]
        zs = [_slab_pair(z_ref[rows, h * GLA_DV:(h + 1) * GLA_DV]) for h in hs]
        bs = [_gla_cumsum(g, SAMPLE_SLAB) for g in _slab_pair(gk)]
        order = [(t, h) for t in range(2) for h in hs]
        prep = _gla_prep([qs[h][t] for t, h in order], [ks[h][t] for t, h in order], [vs[h][t] for t, h in order],
                         [bs[t][:, h * GLA_DK:(h + 1) * GLA_DK] for t, h in order], lseg, SAMPLE_SLAB)
        ys, s_new = _gla_seq(prep, [zs[h][t] for t, h in order], [si_ref[2 * p + t, h] for t, h in order],
                             ng_ref[...])
        for idx, (t, h) in enumerate(order):
            so_ref[2 * p + t, h] = s_new[idx]
        for h in hs:
            y_ref[rows, h * GLA_DV:(h + 1) * GLA_DV] = _slab_merge(ys[h], ys[GLA_HEADS + h]).astype(BF16)
        return carry

    lax.fori_loop(0, npair, pair, 0)


def _gla_sample(p_main, p_small, wup, bup, ng, state, so_prev, y_full, layer, row0, nseq, lseg):
    qw = GLA_HEADS * GLA_DK
    vw = GLA_HEADS * GLA_DV
    rb = SAMPLE_SEQS * lseg
    blk0 = row0 // rb
    st_in, st_specs, st_out_spec, out_shapes, aliases = _sample_io(state, so_prev, y_full, layer, GLA_HEADS,
                                                                   GLA_DK, GLA_DV, 8)
    return pl.pallas_call(
        functools.partial(_gla_sample_kernel, lseg=lseg),
        grid=(nseq // SAMPLE_SEQS,),
        in_specs=[
            pl.BlockSpec((rb, qw), lambda i: (blk0 + i, P_LQ // qw)),
            pl.BlockSpec((rb, qw), lambda i: (blk0 + i, P_LK // qw)),
            pl.BlockSpec((rb, vw), lambda i: (blk0 + i, P_LV // vw)),
            pl.BlockSpec((rb, vw), lambda i: (blk0 + i, P_LGT // vw)),
            pl.BlockSpec((rb, LANES), lambda i: (blk0 + i, 0)),
            pl.BlockSpec((None, LANES, qw), lambda i: (layer, 0, 0)),
            pl.BlockSpec((None, 1, qw), lambda i: (layer, 0, 0)),
            pl.BlockSpec((None, 1, GLA_DV), lambda i: (layer, 0, 0)),
        ] + st_specs,
        out_specs=[pl.BlockSpec((rb, vw), lambda i: (blk0 + i, 0)), st_out_spec],
        out_shape=out_shapes,
        input_output_aliases=aliases,
        compiler_params=_cparams(("arbitrary",)),
        name="gla_sample",
    )(p_main, p_main, p_main, p_main, p_small, wup, bup, ng, *st_in)


def _rope_tables(pos0, length):
    inv = 1.0 / (ROPE_BASE ** jnp.linspace(0.0, 1.0, RET_DK // 2, dtype=F32))
    ang = (jnp.arange(length, dtype=F32) + pos0)[:, None] * inv[None, :]
    cos = jnp.repeat(jnp.cos(ang), 2, axis=1)
    sin = jnp.stack([-jnp.sin(ang), jnp.sin(ang)], axis=-1).reshape(length, RET_DK)
    return cos, sin


def _lane_pad(v, off):
    n = v.shape[-1]
    return jnp.pad(v, ((0, 0), (off, LANES - off - n)))[:, None, :]


def kernel(x_prompt, x_sample, state_ret, state_gdn, state_gdn_conv, state_gla, norm_mix, norm_ffn, norm_final,
           w_in, b_merge, gdn_conv_w, gdn_a_log, gdn_dt_bias, gdn_norm, gla_w_up, gla_b_up, gla_norm, w_branch,
           w_o, w_gate_up, w_down):
    nb, seq, d = x_prompt.shape
    nsq, lseg, _ = x_sample.shape
    depth = w_in.shape[0]
    n_p = nb * seq
    n_s = nsq * lseg
    mix_rows = MIX_CHUNKS * CHUNK
    nc = seq // mix_rows
    nc_ret = seq // RET_CHUNK
    tm = _dense_tiles(n_p + n_s)
    tm_out = math.gcd(n_p, n_s)
    assert seq % mix_rows == 0 and seq % RET_CHUNK == 0
    assert nsq % SAMPLE_SEQS == 0 and n_p % (SAMPLE_SEQS * lseg) == 0
    assert 2 * lseg == SAMPLE_SLAB and lseg >= GDN_CONV - 1

    w_segs = [w_in[..., a:b].astype(BF16) for a, b in W_SEGS]
    w_small = jnp.concatenate([w_in[..., a:b] for a, b in W_SMALL]
                              + [jnp.zeros(w_in.shape[:-1] + (LANES - 32,), w_in.dtype)], axis=-1).astype(BF16)
    wb = w_branch.astype(BF16)
    wo = w_o.astype(BF16)
    wgu = w_gate_up.astype(BF16)
    wdn = w_down.astype(BF16)
    g_mix = norm_mix[:, None, :]
    g_ffn = norm_ffn[:, None, :]
    bm = b_merge.reshape(depth, N_BRANCH, 1, D_MODEL)
    alr = _lane_pad(gdn_a_log, S_DA)
    dtr = _lane_pad(gdn_dt_bias, S_DA)
    alc = jnp.swapaxes(alr, 1, 2)
    dtc = jnp.swapaxes(dtr, 1, 2)
    gdn_ng = gdn_norm[:, None, :]
    gla_ng = gla_norm[:, None, :]
    wup = jnp.pad(gla_w_up, ((0, 0), (S_LLR, LANES - S_LLR - GLA_LOWRANK), (0, 0))).astype(BF16)
    bup = gla_b_up[:, None, :]
    cos_p, sin_p = _rope_tables(0.0, seq)
    cos_s, sin_s = _rope_tables(float(PAST_LEN), lseg)
    cos_s = jnp.tile(cos_s, (SAMPLE_SLAB // lseg, 1))
    sin_s = jnp.tile(sin_s, (SAMPLE_SLAB // lseg, 1))

    x = jnp.concatenate([x_prompt.reshape(n_p, d), x_sample.reshape(n_s, d)], axis=0)

    outs = {k: [] for k in ("p_ret", "p_gdn", "p_conv", "p_gla", "s_conv")}
    s_ret = s_gdn = s_gla = None
    for layer in range(depth):
        p_main, p_small = _in_proj(x, g_mix, w_segs, w_small, layer, tm, 1024)
        pst_p = jnp.swapaxes(p_small[:n_p].reshape(nb * nc, mix_rows, LANES), 1, 2)
        pst_s = jnp.swapaxes(jnp.pad(p_small[n_p:].reshape(nsq, lseg, LANES),
                                     ((0, 0), (0, SAMPLE_SLAB - lseg), (0, 0))), 1, 2)

        y_ret, st = _ret_prompt(p_main, cos_p, sin_p, nb, nc_ret)
        outs["p_ret"].append(st)
        y_ret, s_ret = _ret_sample(p_main, cos_s, sin_s, state_ret, s_ret, y_ret, layer, n_p, nsq, lseg)

        y_gdn, st = _gdn_prompt(p_main, p_small, pst_p, gdn_conv_w, alr, dtr, alc, dtc, gdn_ng, layer, nb, nc)
        outs["p_gdn"].append(st)
        sconv = jnp.pad(state_gdn_conv[layer], ((0, 0), (SUBLANES - (GDN_CONV - 1), 0), (0, 0)))
        y_gdn, s_gdn = _gdn_sample(p_main, p_small, pst_s, sconv, gdn_conv_w, alr, dtr, alc, dtc, gdn_ng,
                                   state_gdn, s_gdn, y_gdn, layer, n_p, nsq, lseg)
        outs["p_conv"].append(jnp.stack([
            lax.slice(p_main, ((b + 1) * seq - (GDN_CONV - 1), P_DQKV), ((b + 1) * seq, P_DQKV + GDN_QKV))
            for b in range(nb)]))
        dq_s = lax.slice(p_main, (n_p, P_DQKV), (n_p + n_s, P_DQKV + GDN_QKV)).reshape(nsq, lseg, GDN_QKV)
        outs["s_conv"].append(jnp.concatenate([state_gdn_conv[layer], dq_s], axis=1)[:, -(GDN_CONV - 1):])

        y_gla, st = _gla_prompt(p_main, p_small, wup, bup, gla_ng, layer, nb, nc)
        outs["p_gla"].append(st)
        y_gla, s_gla = _gla_sample(p_main, p_small, wup, bup, gla_ng, state_gla, s_gla, y_gla, layer, n_p, nsq, lseg)

        mrg = _merge((y_ret, y_gdn, y_gla), wb, p_main, bm, layer, tm, 512)
        x = _out_proj(mrg, wo, x, layer, tm, 1024)
        x = _ffn(x, g_ffn, wgu, wdn, layer, tm, 512)

    g_fin = norm_final[None, :]
    y_p = _final_norm(x, g_fin, 0, n_p, tm_out)
    y_s = _final_norm(x, g_fin, n_p, n_s, tm_out)
    st = {k: jnp.stack(v) for k, v in outs.items()}
    return (y_p.reshape(nb, seq, d), y_s.reshape(nsq, lseg, d),
            st["p_ret"], st["p_gdn"], st["p_conv"], st["p_gla"],
            s_ret, s_gdn, st["s_conv"], s_gla)
```

```python
import functools
import math

import jax
import jax.numpy as jnp
from jax import lax
from jax.experimental import pallas as pl
from jax.experimental.pallas import tpu as pltpu

F32 = jnp.float32
BF16 = jnp.bfloat16

D_MODEL = 2048
RET_HEADS, RET_DK, RET_DV = 4, 128, 256
GDN_HEADS, GDN_DK, GDN_DV, GDN_CONV = 8, 128, 128, 4
GDN_QKV = GDN_HEADS * (2 * GDN_DK + GDN_DV)
GLA_HEADS, GLA_DK, GLA_DV, GLA_LOWRANK = 4, 128, 256, 16
GLA_GATE_NORM = 16.0
N_BRANCH = 3
BRANCH_WIDTH = 1024
EPS = 1e-6
ROPE_BASE = 10000.0
PAST_LEN = 16384

LANES = 128
SUBLANES = 8
CHUNK = 64
MIX_CHUNKS = 4
RET_CHUNK = 256
GLA_SUB = 16
SAMPLE_SLAB = 8
SAMPLE_SEQS = 8
SAMPLE_PAIRS = 2
VMEM_LIMIT = 56 * 1024 * 1024

W_SEGS = ((0, 6144), (6160, 9232), (9248, 16416))
W_SMALL = ((6144, 6160), (9232, 9248))
P_RQ, P_RK, P_RV, P_RG = 0, 512, 1024, 2048
P_DQKV, P_DZ = 3072, 6144
P_LQ, P_LK, P_LV, P_LGT, P_MG = 7168, 7680, 8192, 9216, 10240
P_MAIN = 16384
S_DA, S_DB, S_LLR = 0, 8, 16


def _cparams(sem):
    return pltpu.CompilerParams(dimension_semantics=sem, vmem_limit_bytes=VMEM_LIMIT)


def _dense_tiles(m):
    best = 64
    for t in range(64, 1089, 64):
        if m % t == 0:
            best = t
    return best


_mm = functools.partial(jnp.dot, preferred_element_type=F32)


def _dot(a, b):
    return _mm(a.astype(BF16), b.astype(BF16))


def _dot_nt(a, b):
    return lax.dot_general(a.astype(BF16), b.astype(BF16), (((1,), (1,)), ((), ())),
                           preferred_element_type=F32)


def _dot_tn(a, b):
    return lax.dot_general(a.astype(BF16), b.astype(BF16), (((0,), (0,)), ((), ())),
                           preferred_element_type=F32)


def _split2(x):
    hi = x.astype(BF16)
    lo = (x - hi.astype(F32)).astype(BF16)
    return hi, lo


def _split3(x):
    hi = x.astype(BF16)
    r = x - hi.astype(F32)
    mid = r.astype(BF16)
    lo = (r - mid.astype(F32)).astype(BF16)
    return hi, mid, lo


def _dot_hi_all(a_list, b_list):
    sa = [_split2(a) for a in a_list]
    sb = [_split2(b) for b in b_list]
    r0 = [_mm(a[0], b[0]) for a, b in zip(sa, sb)]
    r1 = [_mm(a[0], b[1]) for a, b in zip(sa, sb)]
    r2 = [_mm(a[1], b[0]) for a, b in zip(sa, sb)]
    return [x + (y + z) for x, y, z in zip(r0, r1, r2)]


def _dot_mask_l(m, x):
    mb = m.astype(BF16)
    hi, mid, lo = _split3(x)
    return _mm(mb, hi) + (_mm(mb, mid) + _mm(mb, lo))


def _dot_mask_r(x, m):
    mb = m.astype(BF16)
    hi, mid, lo = _split3(x)
    return _mm(hi, mb) + (_mm(mid, mb) + _mm(lo, mb))


def _iota(shape, dim):
    return lax.broadcasted_iota(jnp.int32, shape, dim)


def _silu(x):
    return x * jax.nn.sigmoid(x)


def _rms(x, eps=EPS):
    return x * lax.rsqrt(jnp.mean(x * x, axis=-1, keepdims=True) + eps)


def _row_to_col(r):
    n = r.shape[1]
    eye = _iota((n, n), 0) == _iota((n, n), 1)
    return jnp.sum(jnp.where(eye, jnp.broadcast_to(r, (n, n)), 0.0), axis=1, keepdims=True)


def _tril_incl(c):
    return (_iota((c, c), 1) <= _iota((c, c), 0)).astype(F32)


def _pack_kernel(a_ref, b_ref, o_ref, *, edges):
    j = pl.program_id(2)
    lo = 0
    for shift, hi in edges:
        def _(shift=shift):
            if shift == 0:
                o_ref[...] = a_ref[...].astype(BF16)
            else:
                cat = jnp.concatenate([a_ref[...], b_ref[...]], axis=1)
                width = cat.shape[1]
                o_ref[...] = pltpu.roll(cat, width - shift, axis=1)[:, :a_ref.shape[1]].astype(BF16)
        pl.when((j >= lo) & (j < hi))(_)
        lo = hi


def _pack_w_in(w_in, tk, tn):
    depth, d, _ = w_in.shape
    edges, out0 = [], 0
    for a, b in W_SEGS:
        assert out0 % tn == 0 and (b - a) % tn == 0 and 0 <= a - out0 < LANES
        edges.append((a - out0, (out0 + b - a) // tn))
        out0 += b - a
    nt = P_MAIN // tn
    assert out0 == P_MAIN
    return pl.pallas_call(
        functools.partial(_pack_kernel, edges=tuple(edges)),
        grid=(depth, d // tk, nt),
        in_specs=[
            pl.BlockSpec((None, tk, tn), lambda l, k, j: (l, k, j)),
            pl.BlockSpec((None, tk, LANES), lambda l, k, j: (l, k, (j + 1) * (tn // LANES))),
        ],
        out_specs=pl.BlockSpec((None, tk, tn), lambda l, k, j: (l, k, j)),
        out_shape=jax.ShapeDtypeStruct((depth, d, P_MAIN), BF16),
        compiler_params=_cparams(("arbitrary", "arbitrary", "arbitrary")),
        name="pack_w_in",
    )(w_in, w_in)


def _in_proj_kernel(x_ref, g_ref, w_ref, ws_ref, p_ref, ps_ref, h_ref):
    j = pl.program_id(1)

    @pl.when(j == 0)
    def _():
        h = (_rms(x_ref[...]) * g_ref[...]).astype(BF16)
        h_ref[...] = h
        ps_ref[...] = _mm(h, ws_ref[...])

    p_ref[...] = _mm(h_ref[...], w_ref[...])


def _in_proj(x, g, w_main, w_small, layer, tm, tn):
    m, d = x.shape
    n = w_main.shape[2]
    return pl.pallas_call(
        _in_proj_kernel,
        grid=(m // tm, n // tn),
        in_specs=[
            pl.BlockSpec((tm, d), lambda i, j: (i, 0), pipeline_mode=pl.Buffered(1)),
            pl.BlockSpec((None, 1, d), lambda i, j: (layer, 0, 0)),
            pl.BlockSpec((None, d, tn), lambda i, j: (layer, 0, j)),
            pl.BlockSpec((None, d, LANES), lambda i, j: (layer, 0, 0)),
        ],
        out_specs=[
            pl.BlockSpec((tm, tn), lambda i, j: (i, j)),
            pl.BlockSpec((tm, LANES), lambda i, j: (i, 0)),
        ],
        out_shape=[jax.ShapeDtypeStruct((m, n), F32), jax.ShapeDtypeStruct((m, LANES), F32)],
        scratch_shapes=[pltpu.VMEM((tm, d), BF16)],
        compiler_params=_cparams(("arbitrary", "arbitrary")),
        name="in_proj",
    )(x, g, w_main, w_small)


def _merge_kernel(y0_ref, y1_ref, y2_ref, w0_ref, w1_ref, w2_ref, g0_ref, g1_ref, g2_ref,
                  b0_ref, b1_ref, b2_ref, o_ref):
    acc = None
    for y_ref, w_ref, g_ref, b_ref in ((y0_ref, w0_ref, g0_ref, b0_ref), (y1_ref, w1_ref, g1_ref, b1_ref),
                                       (y2_ref, w2_ref, g2_ref, b2_ref)):
        t = jax.nn.sigmoid(g_ref[...] + b_ref[...]) * _mm(y_ref[...], w_ref[...])
        acc = t if acc is None else acc + t
    o_ref[...] = acc.astype(BF16)


def _merge(ys, wb, p_main, b_merge, layer, tm, tn):
    m, bw = ys[0].shape
    d = wb.shape[3]
    nj = d // tn
    mg0 = P_MG // tn

    def y_spec():
        return pl.BlockSpec((tm, bw), lambda i, j: (i, 0), pipeline_mode=pl.Buffered(1))

    def w_spec(b):
        return pl.BlockSpec((None, None, bw, tn), lambda i, j: (layer, b, 0, j))

    def g_spec(b):
        return pl.BlockSpec((tm, tn), lambda i, j: (i, mg0 + b * nj + j))

    def b_spec(b):
        return pl.BlockSpec((None, None, 1, tn), lambda i, j: (layer, b, 0, j))

    rng = range(N_BRANCH)
    return pl.pallas_call(
        _merge_kernel,
        grid=(m // tm, nj),
        in_specs=[y_spec() for _ in rng] + [w_spec(b) for b in rng] + [g_spec(b) for b in rng]
        + [b_spec(b) for b in rng],
        out_specs=pl.BlockSpec((tm, tn), lambda i, j: (i, j)),
        out_shape=jax.ShapeDtypeStruct((m, d), BF16),
        compiler_params=_cparams(("arbitrary", "arbitrary")),
        name="merge",
    )(*ys, wb, wb, wb, p_main, p_main, p_main, b_merge, b_merge, b_merge)


def _out_proj_kernel(m_ref, w_ref, x_ref, o_ref):
    o_ref[...] = x_ref[...] + _mm(m_ref[...], w_ref[...])


def _out_proj(mrg, w_o, x, layer, tm, tn):
    m, d = x.shape
    return pl.pallas_call(
        _out_proj_kernel,
        grid=(m // tm, d // tn),
        in_specs=[
            pl.BlockSpec((tm, d), lambda i, j: (i, 0)),
            pl.BlockSpec((None, d, tn), lambda i, j: (layer, 0, j)),
            pl.BlockSpec((tm, tn), lambda i, j: (i, j)),
        ],
        out_specs=pl.BlockSpec((tm, tn), lambda i, j: (i, j)),
        out_shape=jax.ShapeDtypeStruct((m, d), F32),
        compiler_params=_cparams(("arbitrary", "arbitrary")),
        name="out_proj",
    )(mrg, w_o, x)


def _ffn_kernel(x_ref, g_ref, wg_ref, wu_ref, wd_ref, o_ref, h_ref):
    j = pl.program_id(1)

    @pl.when(j == 0)
    def _():
        x = x_ref[...]
        h_ref[...] = (_rms(x) * g_ref[...]).astype(BF16)
        o_ref[...] = x

    h = h_ref[...]
    act = (_silu(_mm(h, wg_ref[...])) * _mm(h, wu_ref[...])).astype(BF16)
    o_ref[...] += _mm(act, wd_ref[...])


def _ffn(x, g, w_gu, w_dn, layer, tm, tf):
    m, d = x.shape
    nf = w_dn.shape[1] // tf
    return pl.pallas_call(
        _ffn_kernel,
        grid=(m // tm, nf),
        in_specs=[
            pl.BlockSpec((tm, d), lambda i, j: (i, 0), pipeline_mode=pl.Buffered(1)),
            pl.BlockSpec((None, 1, d), lambda i, j: (layer, 0, 0)),
            pl.BlockSpec((None, d, tf), lambda i, j: (layer, 0, j)),
            pl.BlockSpec((None, d, tf), lambda i, j: (layer, 0, nf + j)),
            pl.BlockSpec((None, tf, d), lambda i, j: (layer, j, 0)),
        ],
        out_specs=pl.BlockSpec((tm, d), lambda i, j: (i, 0)),
        out_shape=jax.ShapeDtypeStruct((m, d), F32),
        scratch_shapes=[pltpu.VMEM((tm, d), BF16)],
        compiler_params=_cparams(("arbitrary", "arbitrary")),
        name="ffn",
    )(x, g, w_gu, w_gu, w_dn)


def _final_norm_kernel(x_ref, g_ref, o_ref):
    o_ref[...] = _rms(x_ref[...]) * g_ref[...]


def _final_norm(x, g, row0, rows, tm):
    d = x.shape[1]
    blk0 = row0 // tm
    return pl.pallas_call(
        _final_norm_kernel,
        grid=(rows // tm,),
        in_specs=[pl.BlockSpec((tm, d), lambda i: (blk0 + i, 0)), pl.BlockSpec((1, d), lambda i: (0, 0))],
        out_specs=pl.BlockSpec((tm, d), lambda i: (i, 0)),
        out_shape=jax.ShapeDtypeStruct((rows, d), F32),
        compiler_params=_cparams(("arbitrary",)),
        name="final_norm",
    )(x, g)


def _slab_pair(x):
    return x, pltpu.roll(x, SAMPLE_SLAB // 2, axis=0)


def _slab_merge(ya, yb):
    top = _iota(ya.shape, 0) < SAMPLE_SLAB // 2
    return jnp.where(top, ya, pltpu.roll(yb, SAMPLE_SLAB // 2, axis=0))


def _sample_io(state, so_prev, y_full, layer, heads, dk, dv, n_fixed):
    blk = pl.BlockSpec((None, SAMPLE_SEQS, heads, dk, dv), lambda i: (layer, i, 0, 0, 0))
    ins, specs = [state, y_full], [blk, pl.BlockSpec(memory_space=pl.ANY)]
    aliases = {n_fixed + 1: 0}
    if so_prev is not None:
        ins.append(so_prev)
        specs.append(pl.BlockSpec(memory_space=pl.ANY))
        aliases[n_fixed + 2] = 1
    out_shapes = [jax.ShapeDtypeStruct(y_full.shape, y_full.dtype), jax.ShapeDtypeStruct(state.shape, F32)]
    return ins, specs, blk, out_shapes, aliases


def _rotary(t, cos, sin_signed):
    even = (_iota(t.shape, 1) & 1) == 0
    nxt = pltpu.roll(t, LANES - 1, axis=1)
    prv = pltpu.roll(t, 1, axis=1)
    return t * cos + jnp.where(even, nxt, prv) * sin_signed


def _ret_decays(c, nvalid, lg):
    i = _iota((c, c), 0)
    j = _iota((c, c), 1)
    diff = i - j
    d_intra = jnp.where(diff >= 0, jnp.exp(lg * jnp.maximum(diff, 0).astype(F32)), 0.0)
    pos_i = _iota((c, 1), 0)
    pos = pos_i.astype(F32)
    d_q = jnp.exp(lg * (pos + 1.0))
    d_k = jnp.where(pos_i < nvalid, jnp.exp(lg * (nvalid - 1.0 - pos)), 0.0)
    d_c = math.exp(lg * nvalid)
    return d_intra, d_q, d_k, d_c


def _ret_lg(h):
    return math.log1p(-(2.0 ** (-5.0 - h)))


def _ret_chunks(qs, ks, vs, gs, ss, cos, sin, decs):
    n = range(len(qs))
    qr = [_rotary(q, cos, sin) for q in qs]
    kr = [_rotary(k, cos, sin) * (RET_DK ** -0.5) for k in ks]
    sc = [_dot_nt(qr[p], kr[p]) * decs[p][0] for p in n]
    o_st = [_dot(qr[p] * decs[p][1], ss[p]) for p in n]
    kv = [_dot_tn(kr[p] * decs[p][2], vs[p]) for p in n]
    o = [_dot(sc[p], vs[p]) + o_st[p] for p in n]
    s_new = [decs[p][3] * ss[p] + kv[p] for p in n]
    ys = [_rms(o[p]) * _silu(gs[p]) for p in n]
    return ys, s_new


def _ret_prompt_kernel(q_ref, k_ref, v_ref, g_ref, cos_ref, sin_ref, y_ref, so_ref, s_ref):
    c = pl.program_id(1)

    @pl.when(c == 0)
    def _():
        s_ref[...] = jnp.zeros_like(s_ref)

    hs = range(RET_HEADS)
    decs = [_ret_decays(RET_CHUNK, RET_CHUNK, _ret_lg(h)) for h in hs]
    ys, s_new = _ret_chunks([q_ref[:, h * RET_DK:(h + 1) * RET_DK] for h in hs],
                            [k_ref[:, h * RET_DK:(h + 1) * RET_DK] for h in hs],
                            [v_ref[:, h * RET_DV:(h + 1) * RET_DV] for h in hs],
                            [g_ref[:, h * RET_DV:(h + 1) * RET_DV] for h in hs],
                            [s_ref[h] for h in hs], cos_ref[...], sin_ref[...], decs)
    for h in hs:
        y_ref[:, h * RET_DV:(h + 1) * RET_DV] = ys[h].astype(BF16)
        s_ref[h] = s_new[h]

    @pl.when(c == pl.num_programs(1) - 1)
    def _():
        so_ref[...] = s_ref[...]


def _ret_prompt(p_main, cos, sin, nb, nc):
    qw = RET_HEADS * RET_DK
    vw = RET_HEADS * RET_DV
    return pl.pallas_call(
        _ret_prompt_kernel,
        grid=(nb, nc),
        in_specs=[
            pl.BlockSpec((RET_CHUNK, qw), lambda b, c: (b * nc + c, P_RQ // qw)),
            pl.BlockSpec((RET_CHUNK, qw), lambda b, c: (b * nc + c, P_RK // qw)),
            pl.BlockSpec((RET_CHUNK, vw), lambda b, c: (b * nc + c, P_RV // vw)),
            pl.BlockSpec((RET_CHUNK, vw), lambda b, c: (b * nc + c, P_RG // vw)),
            pl.BlockSpec((RET_CHUNK, LANES), lambda b, c: (c, 0)),
            pl.BlockSpec((RET_CHUNK, LANES), lambda b, c: (c, 0)),
        ],
        out_specs=[
            pl.BlockSpec((RET_CHUNK, vw), lambda b, c: (b * nc + c, 0)),
            pl.BlockSpec((None, RET_HEADS, RET_DK, RET_DV), lambda b, c: (b, 0, 0, 0)),
        ],
        out_shape=[jax.ShapeDtypeStruct((p_main.shape[0], vw), BF16),
                   jax.ShapeDtypeStruct((nb, RET_HEADS, RET_DK, RET_DV), F32)],
        scratch_shapes=[pltpu.VMEM((RET_HEADS, RET_DK, RET_DV), F32)],
        compiler_params=_cparams(("arbitrary", "arbitrary")),
        name="ret_prompt",
    )(p_main, p_main, p_main, p_main, cos, sin)


def _ret_sample_kernel(q_ref, k_ref, v_ref, g_ref, cos_ref, sin_ref, si_ref, *rest, lseg):
    y_ref, so_ref = rest[-2:]
    npair = q_ref.shape[0] // SAMPLE_SLAB
    cos = cos_ref[...]
    sin = sin_ref[...]
    hs = range(RET_HEADS)
    decs = [_ret_decays(SAMPLE_SLAB, lseg, _ret_lg(h)) for h in hs] * (2 * SAMPLE_PAIRS)
    order = [(u, t, h) for u in range(SAMPLE_PAIRS) for t in range(2) for h in hs]

    def pairs(it, carry):
        ps = [it * SAMPLE_PAIRS + u for u in range(SAMPLE_PAIRS)]
        rows = [pl.ds(pl.multiple_of(p * SAMPLE_SLAB, SAMPLE_SLAB), SAMPLE_SLAB) for p in ps]

        def slabs(ref, w):
            return [[_slab_pair(ref[r, h * w:(h + 1) * w]) for h in hs] for r in rows]

        qs, ks, vs, gs = slabs(q_ref, RET_DK), slabs(k_ref, RET_DK), slabs(v_ref, RET_DV), slabs(g_ref, RET_DV)
        ys, s_new = _ret_chunks([qs[u][h][t] for u, t, h in order], [ks[u][h][t] for u, t, h in order],
                                [vs[u][h][t] for u, t, h in order], [gs[u][h][t] for u, t, h in order],
                                [si_ref[2 * ps[u] + t, h] for u, t, h in order], cos, sin, decs)
        for idx, (u, t, h) in enumerate(order):
            so_ref[2 * ps[u] + t, h] = s_new[idx]
        for u in range(SAMPLE_PAIRS):
            base = u * 2 * RET_HEADS
            for h in hs:
                y_ref[rows[u], h * RET_DV:(h + 1) * RET_DV] = _slab_merge(
                    ys[base + h], ys[base + RET_HEADS + h]).astype(BF16)
        return carry

    lax.fori_loop(0, npair // SAMPLE_PAIRS, pairs, 0)


def _ret_sample(p_main, cos, sin, state, so_prev, y_full, layer, row0, nseq, lseg):
    qw = RET_HEADS * RET_DK
    vw = RET_HEADS * RET_DV
    rb = SAMPLE_SEQS * lseg
    blk0 = row0 // rb
    st_in, st_specs, st_out_spec, out_shapes, aliases = _sample_io(state, so_prev, y_full, layer, RET_HEADS,
                                                                   RET_DK, RET_DV, 6)
    return pl.pallas_call(
        functools.partial(_ret_sample_kernel, lseg=lseg),
        grid=(nseq // SAMPLE_SEQS,),
        in_specs=[
            pl.BlockSpec((rb, qw), lambda i: (blk0 + i, P_RQ // qw)),
            pl.BlockSpec((rb, qw), lambda i: (blk0 + i, P_RK // qw)),
            pl.BlockSpec((rb, vw), lambda i: (blk0 + i, P_RV // vw)),
            pl.BlockSpec((rb, vw), lambda i: (blk0 + i, P_RG // vw)),
            pl.BlockSpec((SAMPLE_SLAB, LANES), lambda i: (0, 0)),
            pl.BlockSpec((SAMPLE_SLAB, LANES), lambda i: (0, 0)),
        ] + st_specs,
        out_specs=[pl.BlockSpec((rb, vw), lambda i: (blk0 + i, 0)), st_out_spec],
        out_shape=out_shapes,
        input_output_aliases=aliases,
        compiler_params=_cparams(("arbitrary",)),
        name="ret_sample",
    )(p_main, p_main, p_main, p_main, cos, sin, *st_in)


def _unit_lower_inverse_all(a_list, nvalid):
    c = a_list[0].shape[0]
    n = range(len(a_list))
    i = _iota((c, c), 0)
    j = _iota((c, c), 1)
    eye = (i == j).astype(F32)
    pair = (i >> 1) == (j >> 1)
    ts = [eye - jnp.where(pair, a, 0.0) for a in a_list]
    blk = 2
    while blk < min(c, nvalid):
        sh = blk.bit_length()
        m = ((i >> sh) == (j >> sh)) & ((i & blk) != 0) & ((j & blk) == 0)
        xs = [jnp.where(m, a, 0.0).astype(BF16) for a in a_list]
        tsp = [_split2(t) for t in ts]
        tx = [_mm(tsp[p][0], xs[p]) + _mm(tsp[p][1], xs[p]) for p in n]
        txs = [_split2(y) for y in tx]
        r0 = [_mm(txs[p][0], tsp[p][0]) for p in n]
        r1 = [_mm(txs[p][0], tsp[p][1]) for p in n]
        r2 = [_mm(txs[p][1], tsp[p][0]) for p in n]
        ts = [ts[p] - (r0[p] + (r1[p] + r2[p])) for p in n]
        blk *= 2
    return ts


def _gdn_gates(small, small_t, alog_r, dtb_r, alog_c, dtb_c, c):
    r = small.shape[0]
    g = -jnp.exp(alog_r) * jax.nn.softplus(small + dtb_r)
    g_t = -jnp.exp(alog_c) * jax.nn.softplus(small_t + dtb_c)
    i = _iota((r, r), 0)
    j = _iota((r, r), 1)
    sh = c.bit_length() - 1
    same = ((i >> sh) == (j >> sh)) if r > c else True
    cum = _dot_mask_l((same & (j <= i)).astype(F32), g)
    cum_t = _dot_mask_r(g_t, (same & (i <= j)).astype(F32))
    beta = jax.nn.sigmoid(small)
    return cum, cum_t, beta


def _gdn_prep(qs, ks, vs, cum_cs, cum_rs, beta_cs, nvalid):
    n = range(len(qs))
    c = qs[0].shape[0]
    i = _iota((c, c), 0)
    j = _iota((c, c), 1)
    rowv = _iota((c, 1), 0) < nvalid
    qn = [q * lax.rsqrt(jnp.sum(q * q, axis=-1, keepdims=True) + EPS) * (GDN_DK ** -0.5) for q in qs]
    kn = [k * lax.rsqrt(jnp.sum(k * k, axis=-1, keepdims=True) + EPS) for k in ks]
    dec = [jnp.exp(jnp.where(j <= i, cum_cs[p] - cum_rs[p], -jnp.inf)) for p in n]
    e_c = [jnp.exp(cum_cs[p]) for p in n]
    kb = [kn[p] * beta_cs[p] for p in n]
    kk = [_dot_nt(kb[p], kn[p]) for p in n]
    qk = [_dot_nt(qn[p], kn[p]) for p in n]
    lower = [jnp.where(j < i, kk[p] * dec[p], 0.0) for p in n]
    ts = _unit_lower_inverse_all(lower, nvalid)
    rhs = [jnp.concatenate([vs[p] * beta_cs[p], kb[p] * e_c[p]], axis=1) for p in n]
    sol = [_dot(ts[p], rhs[p]) for p in n]
    cum_last = [cum_cs[p][nvalid - 1:nvalid, :] for p in n]
    out = []
    for p in n:
        wq = jnp.concatenate([sol[p][:, GDN_DV:], qn[p] * e_c[p]], axis=0).astype(BF16)
        kd = (kn[p] * jnp.where(rowv, jnp.exp(cum_last[p] - cum_cs[p]), 0.0)).astype(BF16)
        out.append((wq, sol[p][:, :GDN_DV], (qk[p] * dec[p]).astype(BF16), kd, jnp.exp(cum_last[p])))
    return out


def _gdn_seq(prep, zs, ss, norm_g):
    n = range(len(prep))
    c = prep[0][1].shape[0]
    st = [_mm(prep[p][0], ss[p].astype(BF16)) for p in n]
    v_new = [prep[p][1] - st[p][:c] for p in n]
    o = [st[p][c:] + _dot(prep[p][2], v_new[p]) for p in n]
    kv = [_dot_tn(prep[p][3], v_new[p]) for p in n]
    s_new = [prep[p][4] * ss[p] + kv[p] for p in n]
    ys = [_rms(o[p]) * norm_g * _silu(zs[p]) for p in n]
    return ys, s_new


def _gdn_conv(xcat, cw):
    acc = xcat[SUBLANES:, :] * cw[GDN_CONV - 1:GDN_CONV, :]
    for w in range(GDN_CONV - 1):
        sh = GDN_CONV - 1 - w
        acc = acc + pltpu.roll(xcat, sh, axis=0)[SUBLANES:, :] * cw[w:w + 1, :]
    return _silu(acc)


def _gdn_split(conv, h):
    hk = GDN_HEADS * GDN_DK
    return (conv[:, h * GDN_DK:(h + 1) * GDN_DK], conv[:, hk + h * GDN_DK: hk + (h + 1) * GDN_DK],
            conv[:, 2 * hk + h * GDN_DV: 2 * hk + (h + 1) * GDN_DV])


def _gdn_prompt_kernel(x_ref, z_ref, sm_ref, smt_ref, cw_ref, alr_ref, dtr_ref, alc_ref, dtc_ref, ng_ref,
                       y_ref, so_ref, s_ref, halo_ref):
    c = pl.program_id(1)

    @pl.when(c == 0)
    def _():
        s_ref[...] = jnp.zeros_like(s_ref)
        halo_ref[...] = jnp.zeros_like(halo_ref)

    x = x_ref[...]
    rows = x.shape[0]
    xcat = jnp.concatenate([halo_ref[...], x], axis=0)
    halo_ref[...] = x[rows - SUBLANES:, :]
    conv = _gdn_conv(xcat, cw_ref[...])
    cum, cum_t, beta = _gdn_gates(sm_ref[...], smt_ref[...], alr_ref[...], dtr_ref[...],
                                  alc_ref[...], dtc_ref[...], CHUNK)
    hs = range(GDN_HEADS)
    order = [(ch, h) for ch in range(rows // CHUNK) for h in hs]

    def rws(a, ch):
        return a[ch * CHUNK:(ch + 1) * CHUNK]

    qkv = [_gdn_split(rws(conv, ch), h) for ch, h in order]
    prep = _gdn_prep([t[0] for t in qkv], [t[1] for t in qkv], [t[2] for t in qkv],
                     [rws(cum, ch)[:, S_DA + h:S_DA + h + 1] for ch, h in order],
                     [cum_t[S_DA + h:S_DA + h + 1, ch * CHUNK:(ch + 1) * CHUNK] for ch, h in order],
                     [rws(beta, ch)[:, S_DB + h:S_DB + h + 1] for ch, h in order], CHUNK)
    ss = [s_ref[h] for h in hs]
    for ch in range(rows // CHUNK):
        ys, ss = _gdn_seq(prep[ch * GDN_HEADS:(ch + 1) * GDN_HEADS],
                          [z_ref[ch * CHUNK:(ch + 1) * CHUNK, h * GDN_DV:(h + 1) * GDN_DV] for h in hs],
                          ss, ng_ref[...])
        for h in hs:
            y_ref[ch * CHUNK:(ch + 1) * CHUNK, h * GDN_DV:(h + 1) * GDN_DV] = ys[h].astype(BF16)
    for h in hs:
        s_ref[h] = ss[h]

    @pl.when(c == pl.num_programs(1) - 1)
    def _():
        so_ref[...] = s_ref[...]


def _gdn_prompt(p_main, p_small, p_small_t, cw, alr, dtr, alc, dtc, ng, layer, nb, nc):
    zw = GDN_HEADS * GDN_DV
    rb = MIX_CHUNKS * CHUNK
    return pl.pallas_call(
        _gdn_prompt_kernel,
        grid=(nb, nc),
        in_specs=[
            pl.BlockSpec((rb, GDN_QKV), lambda b, c: (b * nc + c, P_DQKV // GDN_QKV)),
            pl.BlockSpec((rb, zw), lambda b, c: (b * nc + c, P_DZ // zw)),
            pl.BlockSpec((rb, LANES), lambda b, c: (b * nc + c, 0)),
            pl.BlockSpec((None, LANES, rb), lambda b, c: (b * nc + c, 0, 0)),
            pl.BlockSpec((None, GDN_CONV, GDN_QKV), lambda b, c: (layer, 0, 0)),
            pl.BlockSpec((None, 1, LANES), lambda b, c: (layer, 0, 0)),
            pl.BlockSpec((None, 1, LANES), lambda b, c: (layer, 0, 0)),
            pl.BlockSpec((None, LANES, 1), lambda b, c: (layer, 0, 0)),
            pl.BlockSpec((None, LANES, 1), lambda b, c: (layer, 0, 0)),
            pl.BlockSpec((None, 1, GDN_DV), lambda b, c: (layer, 0, 0)),
        ],
        out_specs=[
            pl.BlockSpec((rb, zw), lambda b, c: (b * nc + c, 0)),
            pl.BlockSpec((None, GDN_HEADS, GDN_DK, GDN_DV), lambda b, c: (b, 0, 0, 0)),
        ],
        out_shape=[jax.ShapeDtypeStruct((p_main.shape[0], zw), BF16),
                   jax.ShapeDtypeStruct((nb, GDN_HEADS, GDN_DK, GDN_DV), F32)],
        scratch_shapes=[pltpu.VMEM((GDN_HEADS, GDN_DK, GDN_DV), F32), pltpu.VMEM((SUBLANES, GDN_QKV), F32)],
        compiler_params=_cparams(("arbitrary", "arbitrary")),
        name="gdn_prompt",
    )(p_main, p_main, p_small, p_small_t, cw, alr, dtr, alc, dtc, ng)


def _gdn_sample_kernel(x_ref, z_ref, sm_ref, smt_ref, sc_ref, cw_ref, alr_ref, dtr_ref, alc_ref, dtc_ref, ng_ref,
                       si_ref, *rest, lseg):
    y_ref, so_ref = rest[-2:]
    npair = x_ref.shape[0] // SAMPLE_SLAB
    cw = cw_ref[...]
    hs = range(GDN_HEADS)

    seqs = [(u, t) for u in range(SAMPLE_PAIRS) for t in range(2)]
    order = [(u, t, h) for u, t in seqs for h in hs]

    def pairs(it, carry):
        ps = [it * SAMPLE_PAIRS + u for u in range(SAMPLE_PAIRS)]
        rows = [pl.ds(pl.multiple_of(p * SAMPLE_SLAB, SAMPLE_SLAB), SAMPLE_SLAB) for p in ps]
        xs = [_slab_pair(x_ref[r, :]) for r in rows]
        sms = [_slab_pair(sm_ref[r, :]) for r in rows]
        zs = [[_slab_pair(z_ref[r, h * GDN_DV:(h + 1) * GDN_DV]) for h in hs] for r in rows]
        convs, gates = {}, {}
        for u, t in seqs:
            xcat = jnp.concatenate([sc_ref[2 * ps[u] + t], xs[u][t]], axis=0)
            convs[u, t] = _gdn_conv(xcat, cw)
            gates[u, t] = _gdn_gates(sms[u][t], smt_ref[2 * ps[u] + t], alr_ref[...], dtr_ref[...],
                                     alc_ref[...], dtc_ref[...], SAMPLE_SLAB)
        qkv = [_gdn_split(convs[u, t], h) for u, t, h in order]
        prep = _gdn_prep([a[0] for a in qkv], [a[1] for a in qkv], [a[2] for a in qkv],
                         [gates[u, t][0][:, S_DA + h:S_DA + h + 1] for u, t, h in order],
                         [gates[u, t][1][S_DA + h:S_DA + h + 1, :] for u, t, h in order],
                         [gates[u, t][2][:, S_DB + h:S_DB + h + 1] for u, t, h in order], lseg)
        ys, s_new = _gdn_seq(prep, [zs[u][h][t] for u, t, h in order],
                             [si_ref[2 * ps[u] + t, h] for u, t, h in order], ng_ref[...])
        for idx, (u, t, h) in enumerate(order):
            so_ref[2 * ps[u] + t, h] = s_new[idx]
        for u in range(SAMPLE_PAIRS):
            base = u * 2 * GDN_HEADS
            for h in hs:
                y_ref[rows[u], h * GDN_DV:(h + 1) * GDN_DV] = _slab_merge(
                    ys[base + h], ys[base + GDN_HEADS + h]).astype(BF16)
        return carry

    lax.fori_loop(0, npair // SAMPLE_PAIRS, pairs, 0)


def _gdn_sample(p_main, p_small, p_small_t, sconv, cw, alr, dtr, alc, dtc, ng, state, so_prev, y_full, layer, row0,
                nseq, lseg):
    zw = GDN_HEADS * GDN_DV
    rb = SAMPLE_SEQS * lseg
    blk0 = row0 // rb
    st_in, st_specs, st_out_spec, out_shapes, aliases = _sample_io(state, so_prev, y_full, layer, GDN_HEADS,
                                                                   GDN_DK, GDN_DV, 11)
    return pl.pallas_call(
        functools.partial(_gdn_sample_kernel, lseg=lseg),
        grid=(nseq // SAMPLE_SEQS,),
        in_specs=[
            pl.BlockSpec((rb, GDN_QKV), lambda i: (blk0 + i, P_DQKV // GDN_QKV)),
            pl.BlockSpec((rb, zw), lambda i: (blk0 + i, P_DZ // zw)),
            pl.BlockSpec((rb, LANES), lambda i: (blk0 + i, 0)),
            pl.BlockSpec((SAMPLE_SEQS, LANES, SAMPLE_SLAB), lambda i: (i, 0, 0)),
            pl.BlockSpec((SAMPLE_SEQS, SUBLANES, GDN_QKV), lambda i: (i, 0, 0)),
            pl.BlockSpec((None, GDN_CONV, GDN_QKV), lambda i: (layer, 0, 0)),
            pl.BlockSpec((None, 1, LANES), lambda i: (layer, 0, 0)),
            pl.BlockSpec((None, 1, LANES), lambda i: (layer, 0, 0)),
            pl.BlockSpec((None, LANES, 1), lambda i: (layer, 0, 0)),
            pl.BlockSpec((None, LANES, 1), lambda i: (layer, 0, 0)),
            pl.BlockSpec((None, 1, GDN_DV), lambda i: (layer, 0, 0)),
        ] + st_specs,
        out_specs=[pl.BlockSpec((rb, zw), lambda i: (blk0 + i, 0)), st_out_spec],
        out_shape=out_shapes,
        input_output_aliases=aliases,
        compiler_params=_cparams(("arbitrary",)),
        name="gdn_sample",
    )(p_main, p_main, p_small, p_small_t, sconv, cw, alr, dtr, alc, dtc, ng, *st_in)


def _gla_scores(q, k, b, sub):
    c = q.shape[0]
    irow = _iota((sub, 1), 0)
    jrow_all = _iota((c, 1), 0)
    lane = _iota((sub, c), 1)
    blocks = []
    for blk in range(c // sub):
        r0 = blk * sub
        qi, ki, bi = q[r0:r0 + sub], k[r0:r0 + sub], b[r0:r0 + sub]
        if blk > 0:
            b0 = b[r0 - 1:r0, :]
            qt = qi * jnp.exp(bi - b0)
            kt = k * jnp.exp(jnp.where(jrow_all < r0, b0 - b, -jnp.inf))
            a = _dot_nt(qt, kt)
        else:
            a = jnp.zeros((sub, c), F32)
        for jj in range(sub):
            e = jnp.exp(jnp.where(irow >= jj, bi - bi[jj:jj + 1, :], -jnp.inf))
            col = jnp.sum(qi * e * ki[jj:jj + 1, :], axis=1, keepdims=True)
            a = jnp.where(lane == r0 + jj, col, a)
        blocks.append(a)
    return jnp.concatenate(blocks, axis=0) if len(blocks) > 1 else blocks[0]


def _gla_prep(qs, ks, vs, bs, nvalid, sub):
    n = range(len(qs))
    c = qs[0].shape[0]
    rowv = _iota((c, 1), 0) < nvalid
    qsc = [q * (GLA_DK ** -0.5) for q in qs]
    b_last = [bs[p][nvalid - 1:nvalid, :] for p in n]
    kv = [_dot_tn(ks[p] * jnp.where(rowv, jnp.exp(b_last[p] - bs[p]), 0.0), vs[p]) for p in n]
    a = [_gla_scores(qsc[p], ks[p], bs[p], sub) for p in n]
    o_in = [_dot(a[p], vs[p]) for p in n]
    return [((qsc[p] * jnp.exp(bs[p])).astype(BF16), o_in[p], kv[p], _row_to_col(jnp.exp(b_last[p]))) for p in n]


def _gla_seq(prep, zs, ss, norm_g):
    n = range(len(prep))
    o = [prep[p][1] + _mm(prep[p][0], ss[p].astype(BF16)) for p in n]
    s_new = [prep[p][3] * ss[p] + prep[p][2] for p in n]
    ys = [_rms(o[p]) * norm_g * _silu(zs[p]) for p in n]
    return ys, s_new


def _gla_cumsum(gk, c):
    r = gk.shape[0]
    i = _iota((r, r), 0)
    j = _iota((r, r), 1)
    sh = c.bit_length() - 1
    same = ((i >> sh) == (j >> sh)) if r > c else True
    return _dot_mask_l((same & (j <= i)).astype(F32), gk)


def _gla_gk(small, wup, bup):
    return jax.nn.log_sigmoid(_mm(small.astype(BF16), wup) + bup) / GLA_GATE_NORM


def _gla_prompt_kernel(q_ref, k_ref, v_ref, z_ref, sm_ref, wup_ref, bup_ref, ng_ref, y_ref, so_ref, s_ref):
    c = pl.program_id(1)

    @pl.when(c == 0)
    def _():
        s_ref[...] = jnp.zeros_like(s_ref)

    rows = q_ref.shape[0]
    b_all = _gla_cumsum(_gla_gk(sm_ref[...], wup_ref[...], bup_ref[...]), CHUNK)
    hs = range(GLA_HEADS)
    order = [(ch, h) for ch in range(rows // CHUNK) for h in hs]

    def blk(ref, ch, h, w):
        return ref[ch * CHUNK:(ch + 1) * CHUNK, h * w:(h + 1) * w]

    prep = _gla_prep([blk(q_ref, ch, h, GLA_DK) for ch, h in order], [blk(k_ref, ch, h, GLA_DK) for ch, h in order],
                     [blk(v_ref, ch, h, GLA_DV) for ch, h in order], [blk(b_all, ch, h, GLA_DK) for ch, h in order],
                     CHUNK, GLA_SUB)
    ss = [s_ref[h] for h in hs]
    for ch in range(rows // CHUNK):
        ys, ss = _gla_seq(prep[ch * GLA_HEADS:(ch + 1) * GLA_HEADS], [blk(z_ref, ch, h, GLA_DV) for h in hs],
                          ss, ng_ref[...])
        for h in hs:
            y_ref[ch * CHUNK:(ch + 1) * CHUNK, h * GLA_DV:(h + 1) * GLA_DV] = ys[h].astype(BF16)
    for h in hs:
        s_ref[h] = ss[h]

    @pl.when(c == pl.num_programs(1) - 1)
    def _():
        so_ref[...] = s_ref[...]


def _gla_prompt(p_main, p_small, wup, bup, ng, layer, nb, nc):
    qw = GLA_HEADS * GLA_DK
    vw = GLA_HEADS * GLA_DV
    rb = MIX_CHUNKS * CHUNK
    return pl.pallas_call(
        _gla_prompt_kernel,
        grid=(nb, nc),
        in_specs=[
            pl.BlockSpec((rb, qw), lambda b, c: (b * nc + c, P_LQ // qw)),
            pl.BlockSpec((rb, qw), lambda b, c: (b * nc + c, P_LK // qw)),
            pl.BlockSpec((rb, vw), lambda b, c: (b * nc + c, P_LV // vw)),
            pl.BlockSpec((rb, vw), lambda b, c: (b * nc + c, P_LGT // vw)),
            pl.BlockSpec((rb, LANES), lambda b, c: (b * nc + c, 0)),
            pl.BlockSpec((None, LANES, qw), lambda b, c: (layer, 0, 0)),
            pl.BlockSpec((None, 1, qw), lambda b, c: (layer, 0, 0)),
            pl.BlockSpec((None, 1, GLA_DV), lambda b, c: (layer, 0, 0)),
        ],
        out_specs=[
            pl.BlockSpec((rb, vw), lambda b, c: (b * nc + c, 0)),
            pl.BlockSpec((None, GLA_HEADS, GLA_DK, GLA_DV), lambda b, c: (b, 0, 0, 0)),
        ],
        out_shape=[jax.ShapeDtypeStruct((p_main.shape[0], vw), BF16),
                   jax.ShapeDtypeStruct((nb, GLA_HEADS, GLA_DK, GLA_DV), F32)],
        scratch_shapes=[pltpu.VMEM((GLA_HEADS, GLA_DK, GLA_DV), F32)],
        compiler_params=_cparams(("arbitrary", "arbitrary")),
        name="gla_prompt",
    )(p_main, p_main, p_main, p_main, p_small, wup, bup, ng)


def _gla_sample_kernel(q_ref, k_ref, v_ref, z_ref, sm_ref, wup_ref, bup_ref, ng_ref, si_ref, *rest, lseg):
    y_ref, so_ref = rest[-2:]
    npair = q_ref.shape[0] // SAMPLE_SLAB
    hs = range(GLA_HEADS)

    order = [(u, t, h) for u in range(SAMPLE_PAIRS) for t in range(2) for h in hs]

    def pairs(it, carry):
        ps = [it * SAMPLE_PAIRS + u for u in range(SAMPLE_PAIRS)]
        rows = [pl.ds(pl.multiple_of(p * SAMPLE_SLAB, SAMPLE_SLAB), SAMPLE_SLAB) for p in ps]

        def slabs(ref, w):
            return [[_slab_pair(ref[r, h * w:(h + 1) * w]) for h in hs] for r in rows]

        qs, ks, vs, zs = slabs(q_ref, GLA_DK), slabs(k_ref, GLA_DK), slabs(v_ref, GLA_DV), slabs(z_ref, GLA_DV)
        bs = [[_gla_cumsum(g, SAMPLE_SLAB) for g in _slab_pair(_gla_gk(sm_ref[r, :], wup_ref[...], bup_ref[...]))]
              for r in rows]
        prep = _gla_prep([qs[u][h][t] for u, t, h in order], [ks[u][h][t] for u, t, h in order],
                         [vs[u][h][t] for u, t, h in order],
                         [bs[u][t][:, h * GLA_DK:(h + 1) * GLA_DK] for u, t, h in order], lseg, SAMPLE_SLAB)
        ys, s_new = _gla_seq(prep, [zs[u][h][t] for u, t, h in order],
                             [si_ref[2 * ps[u] + t, h] for u, t, h in order], ng_ref[...])
        for idx, (u, t, h) in enumerate(order):
            so_ref[2 * ps[u] + t, h] = s_new[idx]
        for u in range(SAMPLE_PAIRS):
            base = u * 2 * GLA_HEADS
            for h in hs:
                y_ref[rows[u], h * GLA_DV:(h + 1) * GLA_DV] = _slab_merge(
                    ys[base + h], ys[base + GLA_HEADS + h]).astype(BF16)
        return carry

    lax.fori_loop(0, npair // SAMPLE_PAIRS, pairs, 0)


def _gla_sample(p_main, p_small, wup, bup, ng, state, so_prev, y_full, layer, row0, nseq, lseg):
    qw = GLA_HEADS * GLA_DK
    vw = GLA_HEADS * GLA_DV
    rb = SAMPLE_SEQS * lseg
    blk0 = row0 // rb
    st_in, st_specs, st_out_spec, out_shapes, aliases = _sample_io(state, so_prev, y_full, layer, GLA_HEADS,
                                                                   GLA_DK, GLA_DV, 8)
    return pl.pallas_call(
        functools.partial(_gla_sample_kernel, lseg=lseg),
        grid=(nseq // SAMPLE_SEQS,),
        in_specs=[
            pl.BlockSpec((rb, qw), lambda i: (blk0 + i, P_LQ // qw)),
            pl.BlockSpec((rb, qw), lambda i: (blk0 + i, P_LK // qw)),
            pl.BlockSpec((rb, vw), lambda i: (blk0 + i, P_LV // vw)),
            pl.BlockSpec((rb, vw), lambda i: (blk0 + i, P_LGT // vw)),
            pl.BlockSpec((rb, LANES), lambda i: (blk0 + i, 0)),
            pl.BlockSpec((None, LANES, qw), lambda i: (layer, 0, 0)),
            pl.BlockSpec((None, 1, qw), lambda i: (layer, 0, 0)),
            pl.BlockSpec((None, 1, GLA_DV), lambda i: (layer, 0, 0)),
        ] + st_specs,
        out_specs=[pl.BlockSpec((rb, vw), lambda i: (blk0 + i, 0)), st_out_spec],
        out_shape=out_shapes,
        input_output_aliases=aliases,
        compiler_params=_cparams(("arbitrary",)),
        name="gla_sample",
    )(p_main, p_main, p_main, p_main, p_small, wup, bup, ng, *st_in)


def _rope_tables(pos0, length):
    inv = 1.0 / (ROPE_BASE ** jnp.linspace(0.0, 1.0, RET_DK // 2, dtype=F32))
    ang = (jnp.arange(length, dtype=F32) + pos0)[:, None] * inv[None, :]
    cos = jnp.repeat(jnp.cos(ang), 2, axis=1)
    sin = jnp.stack([-jnp.sin(ang), jnp.sin(ang)], axis=-1).reshape(length, RET_DK)
    return cos, sin


def _lane_pad(v, off):
    n = v.shape[-1]
    return jnp.pad(v, ((0, 0), (off, LANES - off - n)))[:, None, :]


def kernel(x_prompt, x_sample, state_ret, state_gdn, state_gdn_conv, state_gla, norm_mix, norm_ffn, norm_final,
           w_in, b_merge, gdn_conv_w, gdn_a_log, gdn_dt_bias, gdn_norm, gla_w_up, gla_b_up, gla_norm, w_branch,
           w_o, w_gate_up, w_down):
    nb, seq, d = x_prompt.shape
    nsq, lseg, _ = x_sample.shape
    depth = w_in.shape[0]
    n_p = nb * seq
    n_s = nsq * lseg
    mix_rows = MIX_CHUNKS * CHUNK
    nc = seq // mix_rows
    nc_ret = seq // RET_CHUNK
    tm = _dense_tiles(n_p + n_s)
    tm_out = math.gcd(n_p, n_s)
    assert seq % mix_rows == 0 and seq % RET_CHUNK == 0
    assert nsq % SAMPLE_SEQS == 0 and n_p % (SAMPLE_SEQS * lseg) == 0
    assert 2 * lseg == SAMPLE_SLAB and lseg >= GDN_CONV - 1

    w_main = _pack_w_in(w_in, 512, 1024)
    w_small = jnp.concatenate([w_in[..., a:b] for a, b in W_SMALL]
                              + [jnp.zeros(w_in.shape[:-1] + (LANES - 32,), w_in.dtype)], axis=-1).astype(BF16)
    wb = w_branch.astype(BF16)
    wo = w_o.astype(BF16)
    wgu = w_gate_up.astype(BF16)
    wdn = w_down.astype(BF16)
    g_mix = norm_mix[:, None, :]
    g_ffn = norm_ffn[:, None, :]
    bm = b_merge.reshape(depth, N_BRANCH, 1, D_MODEL)
    alr = _lane_pad(gdn_a_log, S_DA)
    dtr = _lane_pad(gdn_dt_bias, S_DA)
    alc = jnp.swapaxes(alr, 1, 2)
    dtc = jnp.swapaxes(dtr, 1, 2)
    gdn_ng = gdn_norm[:, None, :]
    gla_ng = gla_norm[:, None, :]
    wup = jnp.pad(gla_w_up, ((0, 0), (S_LLR, LANES - S_LLR - GLA_LOWRANK), (0, 0))).astype(BF16)
    bup = gla_b_up[:, None, :]
    cos_p, sin_p = _rope_tables(0.0, seq)
    cos_s, sin_s = _rope_tables(float(PAST_LEN), lseg)
    cos_s = jnp.tile(cos_s, (SAMPLE_SLAB // lseg, 1))
    sin_s = jnp.tile(sin_s, (SAMPLE_SLAB // lseg, 1))

    x = jnp.concatenate([x_prompt.reshape(n_p, d), x_sample.reshape(n_s, d)], axis=0)

    outs = {k: [] for k in ("p_ret", "p_gdn", "p_conv", "p_gla", "s_conv")}
    s_ret = s_gdn = s_gla = None
    for layer in range(depth):
        p_main, p_small = _in_proj(x, g_mix, w_main, w_small, layer, tm, 2048)
        pst_p = jnp.swapaxes(p_small[:n_p].reshape(nb * nc, mix_rows, LANES), 1, 2)
        pst_s = jnp.swapaxes(jnp.pad(p_small[n_p:].reshape(nsq, lseg, LANES),
                                     ((0, 0), (0, SAMPLE_SLAB - lseg), (0, 0))), 1, 2)

        y_ret, st = _ret_prompt(p_main, cos_p, sin_p, nb, nc_ret)
        outs["p_ret"].append(st)
        y_ret, s_ret = _ret_sample(p_main, cos_s, sin_s, state_ret, s_ret, y_ret, layer, n_p, nsq, lseg)

        y_gdn, st = _gdn_prompt(p_main, p_small, pst_p, gdn_conv_w, alr, dtr, alc, dtc, gdn_ng, layer, nb, nc)
        outs["p_gdn"].append(st)
        sconv = jnp.pad(state_gdn_conv[layer], ((0, 0), (SUBLANES - (GDN_CONV - 1), 0), (0, 0)))
        y_gdn, s_gdn = _gdn_sample(p_main, p_small, pst_s, sconv, gdn_conv_w, alr, dtr, alc, dtc, gdn_ng,
                                   state_gdn, s_gdn, y_gdn, layer, n_p, nsq, lseg)
        outs["p_conv"].append(jnp.stack([
            lax.slice(p_main, ((b + 1) * seq - (GDN_CONV - 1), P_DQKV), ((b + 1) * seq, P_DQKV + GDN_QKV))
            for b in range(nb)]))
        dq_s = lax.slice(p_main, (n_p, P_DQKV), (n_p + n_s, P_DQKV + GDN_QKV)).reshape(nsq, lseg, GDN_QKV)
        outs["s_conv"].append(jnp.concatenate([state_gdn_conv[layer], dq_s], axis=1)[:, -(GDN_CONV - 1):])

        y_gla, st = _gla_prompt(p_main, p_small, wup, bup, gla_ng, layer, nb, nc)
        outs["p_gla"].append(st)
        y_gla, s_gla = _gla_sample(p_main, p_small, wup, bup, gla_ng, state_gla, s_gla, y_gla, layer, n_p, nsq, lseg)

        mrg = _merge((y_ret, y_gdn, y_gla), wb, p_main, bm, layer, tm, 512)
        x = _out_proj(mrg, wo, x, layer, tm, 1024)
        x = _ffn(x, g_ffn, wgu, wdn, layer, tm, 512)

    g_fin = norm_final[None, :]
    y_p = _final_norm(x, g_fin, 0, n_p, tm_out)
    y_s = _final_norm(x, g_fin, n_p, n_s, tm_out)
    st = {k: jnp.stack(v) for k, v in outs.items()}
    return (y_p.reshape(nb, seq, d), y_s.reshape(nsq, lseg, d),
            st["p_ret"], st["p_gdn"], st["p_conv"], st["p_gla"],
            s_ret, s_gdn, st["s_conv"], s_gla)
```

```python
import functools
import math

import jax
import jax.numpy as jnp
from jax import lax
from jax.experimental import pallas as pl
from jax.experimental.pallas import tpu as pltpu

F32 = jnp.float32
BF16 = jnp.bfloat16

D_MODEL = 2048
RET_HEADS, RET_DK, RET_DV = 4, 128, 256
GDN_HEADS, GDN_DK, GDN_DV, GDN_CONV = 8, 128, 128, 4
GDN_QKV = GDN_HEADS * (2 * GDN_DK + GDN_DV)
GLA_HEADS, GLA_DK, GLA_DV, GLA_LOWRANK = 4, 128, 256, 16
GLA_GATE_NORM = 16.0
N_BRANCH = 3
BRANCH_WIDTH = 1024
EPS = 1e-6
ROPE_BASE = 10000.0
PAST_LEN = 16384

LANES = 128
SUBLANES = 8
CHUNK = 64
MIX_CHUNKS = 4
RET_CHUNK = 256
GLA_SUB = 16
SAMPLE_SLAB = 8
SAMPLE_SEQS = 8
SAMPLE_PAIRS = 2
VMEM_LIMIT = 56 * 1024 * 1024

W_SEGS = ((0, 6144), (6160, 9232), (9248, 16416))
W_SMALL = ((6144, 6160), (9232, 9248))
P_RQ, P_RK, P_RV, P_RG = 0, 512, 1024, 2048
P_DQKV, P_DZ = 3072, 6144
P_LQ, P_LK, P_LV, P_LGT, P_MG = 7168, 7680, 8192, 9216, 10240
P_MAIN = 16384
S_DA, S_DB, S_LLR = 0, 8, 16


def _cparams(sem):
    return pltpu.CompilerParams(dimension_semantics=sem, vmem_limit_bytes=VMEM_LIMIT)


def _dense_tiles(m):
    best = 64
    for t in range(64, 1089, 64):
        if m % t == 0:
            best = t
    return best


_mm = functools.partial(jnp.dot, preferred_element_type=F32)


def _dot(a, b):
    return _mm(a.astype(BF16), b.astype(BF16))


def _dot_nt(a, b):
    return lax.dot_general(a.astype(BF16), b.astype(BF16), (((1,), (1,)), ((), ())),
                           preferred_element_type=F32)


def _dot_tn(a, b):
    return lax.dot_general(a.astype(BF16), b.astype(BF16), (((0,), (0,)), ((), ())),
                           preferred_element_type=F32)


def _split2(x):
    hi = x.astype(BF16)
    lo = (x - hi.astype(F32)).astype(BF16)
    return hi, lo


def _split3(x):
    hi = x.astype(BF16)
    r = x - hi.astype(F32)
    mid = r.astype(BF16)
    lo = (r - mid.astype(F32)).astype(BF16)
    return hi, mid, lo


def _dot_hi_all(a_list, b_list):
    sa = [_split2(a) for a in a_list]
    sb = [_split2(b) for b in b_list]
    r0 = [_mm(a[0], b[0]) for a, b in zip(sa, sb)]
    r1 = [_mm(a[0], b[1]) for a, b in zip(sa, sb)]
    r2 = [_mm(a[1], b[0]) for a, b in zip(sa, sb)]
    return [x + (y + z) for x, y, z in zip(r0, r1, r2)]


def _dot_mask_l(m, x):
    mb = m.astype(BF16)
    hi, mid, lo = _split3(x)
    return _mm(mb, hi) + (_mm(mb, mid) + _mm(mb, lo))


def _dot_mask_r(x, m):
    mb = m.astype(BF16)
    hi, mid, lo = _split3(x)
    return _mm(hi, mb) + (_mm(mid, mb) + _mm(lo, mb))


def _iota(shape, dim):
    return lax.broadcasted_iota(jnp.int32, shape, dim)


def _silu(x):
    return x * jax.nn.sigmoid(x)


def _rms(x, eps=EPS):
    return x * lax.rsqrt(jnp.mean(x * x, axis=-1, keepdims=True) + eps)


def _row_to_col(r):
    n = r.shape[1]
    eye = _iota((n, n), 0) == _iota((n, n), 1)
    return jnp.sum(jnp.where(eye, jnp.broadcast_to(r, (n, n)), 0.0), axis=1, keepdims=True)


def _tril_incl(c):
    return (_iota((c, c), 1) <= _iota((c, c), 0)).astype(F32)


PACK_TAIL = 32


def _pack_kernel(a_ref, b_ref, o_ref, *, edges):
    j = pl.program_id(1)
    lo = 0
    for shift, hi in edges:
        def _(shift=shift):
            src = a_ref[...]
            if shift:
                src = jnp.concatenate([src[shift:], b_ref[:shift]], axis=0)
            o_ref[...] = src.T.astype(BF16)
        pl.when((j >= lo) & (j < hi))(_)
        lo = hi


def _pack_w_in(w_t, tk, tn):
    depth, _, d = w_t.shape
    edges, out0 = [], 0
    for a, b in W_SEGS:
        assert out0 % tn == 0 and (b - a) % tn == 0 and 0 <= a - out0 <= PACK_TAIL and (a - out0) % SUBLANES == 0
        edges.append((a - out0, (out0 + b - a) // tn))
        out0 += b - a
    assert out0 == P_MAIN
    return pl.pallas_call(
        functools.partial(_pack_kernel, edges=tuple(edges)),
        grid=(depth, P_MAIN // tn, d // tk),
        in_specs=[
            pl.BlockSpec((None, tn, tk), lambda l, j, k: (l, j, k)),
            pl.BlockSpec((None, PACK_TAIL, tk), lambda l, j, k: (l, (j + 1) * (tn // PACK_TAIL), k)),
        ],
        out_specs=pl.BlockSpec((None, tk, tn), lambda l, j, k: (l, k, j)),
        out_shape=jax.ShapeDtypeStruct((depth, d, P_MAIN), BF16),
        compiler_params=_cparams(("arbitrary", "arbitrary", "arbitrary")),
        name="pack_w_in",
    )(w_t, w_t)


def _pack_small_kernel(a_ref, b_ref, o_ref):
    rows = jnp.concatenate([a_ref[...], b_ref[...],
                            jnp.zeros((LANES - a_ref.shape[0] - b_ref.shape[0], a_ref.shape[1]), F32)], axis=0)
    o_ref[...] = rows.T.astype(BF16)


def _pack_w_small(w_t, tk):
    depth, _, d = w_t.shape
    (a0, a1), (b0, b1) = W_SMALL
    na, nb = a1 - a0, b1 - b0
    assert a0 % na == 0 and b0 % nb == 0 and na % SUBLANES == 0 and nb % SUBLANES == 0
    return pl.pallas_call(
        _pack_small_kernel,
        grid=(depth, d // tk),
        in_specs=[pl.BlockSpec((None, na, tk), lambda l, k: (l, a0 // na, k)),
                  pl.BlockSpec((None, nb, tk), lambda l, k: (l, b0 // nb, k))],
        out_specs=pl.BlockSpec((None, tk, LANES), lambda l, k: (l, k, 0)),
        out_shape=jax.ShapeDtypeStruct((depth, d, LANES), BF16),
        compiler_params=_cparams(("arbitrary", "arbitrary")),
        name="pack_w_small",
    )(w_t, w_t)


def _in_proj_kernel(x_ref, g_ref, w_ref, ws_ref, p_ref, ps_ref, h_ref):
    j = pl.program_id(1)

    @pl.when(j == 0)
    def _():
        h = (_rms(x_ref[...]) * g_ref[...]).astype(BF16)
        h_ref[...] = h
        ps_ref[...] = _mm(h, ws_ref[...])

    p_ref[...] = _mm(h_ref[...], w_ref[...])


def _in_proj(x, g, w_main, w_small, layer, tm, tn):
    m, d = x.shape
    n = w_main.shape[2]
    return pl.pallas_call(
        _in_proj_kernel,
        grid=(m // tm, n // tn),
        in_specs=[
            pl.BlockSpec((tm, d), lambda i, j: (i, 0), pipeline_mode=pl.Buffered(1)),
            pl.BlockSpec((None, 1, d), lambda i, j: (layer, 0, 0)),
            pl.BlockSpec((None, d, tn), lambda i, j: (layer, 0, j)),
            pl.BlockSpec((None, d, LANES), lambda i, j: (layer, 0, 0)),
        ],
        out_specs=[
            pl.BlockSpec((tm, tn), lambda i, j: (i, j)),
            pl.BlockSpec((tm, LANES), lambda i, j: (i, 0)),
        ],
        out_shape=[jax.ShapeDtypeStruct((m, n), F32), jax.ShapeDtypeStruct((m, LANES), F32)],
        scratch_shapes=[pltpu.VMEM((tm, d), BF16)],
        compiler_params=_cparams(("arbitrary", "arbitrary")),
        name="in_proj",
    )(x, g, w_main, w_small)


def _merge_kernel(y0_ref, y1_ref, y2_ref, w0_ref, w1_ref, w2_ref, g0_ref, g1_ref, g2_ref,
                  b0_ref, b1_ref, b2_ref, o_ref):
    acc = None
    for y_ref, w_ref, g_ref, b_ref in ((y0_ref, w0_ref, g0_ref, b0_ref), (y1_ref, w1_ref, g1_ref, b1_ref),
                                       (y2_ref, w2_ref, g2_ref, b2_ref)):
        t = jax.nn.sigmoid(g_ref[...] + b_ref[...]) * _mm(y_ref[...], w_ref[...])
        acc = t if acc is None else acc + t
    o_ref[...] = acc.astype(BF16)


def _merge(ys, wb, p_main, b_merge, layer, tm, tn):
    m, bw = ys[0].shape
    d = wb.shape[3]
    nj = d // tn
    mg0 = P_MG // tn

    def y_spec():
        return pl.BlockSpec((tm, bw), lambda i, j: (i, 0), pipeline_mode=pl.Buffered(1))

    def w_spec(b):
        return pl.BlockSpec((None, None, bw, tn), lambda i, j: (layer, b, 0, j))

    def g_spec(b):
        return pl.BlockSpec((tm, tn), lambda i, j: (i, mg0 + b * nj + j))

    def b_spec(b):
        return pl.BlockSpec((None, None, 1, tn), lambda i, j: (layer, b, 0, j))

    rng = range(N_BRANCH)
    return pl.pallas_call(
        _merge_kernel,
        grid=(m // tm, nj),
        in_specs=[y_spec() for _ in rng] + [w_spec(b) for b in rng] + [g_spec(b) for b in rng]
        + [b_spec(b) for b in rng],
        out_specs=pl.BlockSpec((tm, tn), lambda i, j: (i, j)),
        out_shape=jax.ShapeDtypeStruct((m, d), BF16),
        compiler_params=_cparams(("arbitrary", "arbitrary")),
        name="merge",
    )(*ys, wb, wb, wb, p_main, p_main, p_main, b_merge, b_merge, b_merge)


def _out_proj_kernel(m_ref, w_ref, x_ref, o_ref):
    o_ref[...] = x_ref[...] + _mm(m_ref[...], w_ref[...])


def _out_proj(mrg, w_o, x, layer, tm, tn):
    m, d = x.shape
    return pl.pallas_call(
        _out_proj_kernel,
        grid=(m // tm, d // tn),
        in_specs=[
            pl.BlockSpec((tm, d), lambda i, j: (i, 0)),
            pl.BlockSpec((None, d, tn), lambda i, j: (layer, 0, j)),
            pl.BlockSpec((tm, tn), lambda i, j: (i, j)),
        ],
        out_specs=pl.BlockSpec((tm, tn), lambda i, j: (i, j)),
        out_shape=jax.ShapeDtypeStruct((m, d), F32),
        compiler_params=_cparams(("arbitrary", "arbitrary")),
        name="out_proj",
    )(mrg, w_o, x)


def _ffn_kernel(x_ref, g_ref, wg_ref, wu_ref, wd_ref, o_ref, h_ref):
    j = pl.program_id(1)

    @pl.when(j == 0)
    def _():
        x = x_ref[...]
        h_ref[...] = (_rms(x) * g_ref[...]).astype(BF16)
        o_ref[...] = x

    h = h_ref[...]
    act = (_silu(_mm(h, wg_ref[...])) * _mm(h, wu_ref[...])).astype(BF16)
    o_ref[...] += _mm(act, wd_ref[...])


def _ffn(x, g, w_gu, w_dn, layer, tm, tf):
    m, d = x.shape
    nf = w_dn.shape[1] // tf
    return pl.pallas_call(
        _ffn_kernel,
        grid=(m // tm, nf),
        in_specs=[
            pl.BlockSpec((tm, d), lambda i, j: (i, 0), pipeline_mode=pl.Buffered(1)),
            pl.BlockSpec((None, 1, d), lambda i, j: (layer, 0, 0)),
            pl.BlockSpec((None, d, tf), lambda i, j: (layer, 0, j)),
            pl.BlockSpec((None, d, tf), lambda i, j: (layer, 0, nf + j)),
            pl.BlockSpec((None, tf, d), lambda i, j: (layer, j, 0)),
        ],
        out_specs=pl.BlockSpec((tm, d), lambda i, j: (i, 0)),
        out_shape=jax.ShapeDtypeStruct((m, d), F32),
        scratch_shapes=[pltpu.VMEM((tm, d), BF16)],
        compiler_params=_cparams(("arbitrary", "arbitrary")),
        name="ffn",
    )(x, g, w_gu, w_gu, w_dn)


def _final_norm_kernel(x_ref, g_ref, o_ref):
    o_ref[...] = _rms(x_ref[...]) * g_ref[...]


def _final_norm(x, g, row0, rows, tm):
    d = x.shape[1]
    blk0 = row0 // tm
    return pl.pallas_call(
        _final_norm_kernel,
        grid=(rows // tm,),
        in_specs=[pl.BlockSpec((tm, d), lambda i: (blk0 + i, 0)), pl.BlockSpec((1, d), lambda i: (0, 0))],
        out_specs=pl.BlockSpec((tm, d), lambda i: (i, 0)),
        out_shape=jax.ShapeDtypeStruct((rows, d), F32),
        compiler_params=_cparams(("arbitrary",)),
        name="final_norm",
    )(x, g)


def _slab_pair(x):
    return x, pltpu.roll(x, SAMPLE_SLAB // 2, axis=0)


def _slab_merge(ya, yb):
    top = _iota(ya.shape, 0) < SAMPLE_SLAB // 2
    return jnp.where(top, ya, pltpu.roll(yb, SAMPLE_SLAB // 2, axis=0))


def _sample_io(state, so_prev, y_full, layer, heads, dk, dv, n_fixed):
    blk = pl.BlockSpec((None, SAMPLE_SEQS, heads, dk, dv), lambda i: (layer, i, 0, 0, 0))
    ins, specs = [state, y_full], [blk, pl.BlockSpec(memory_space=pl.ANY)]
    aliases = {n_fixed + 1: 0}
    if so_prev is not None:
        ins.append(so_prev)
        specs.append(pl.BlockSpec(memory_space=pl.ANY))
        aliases[n_fixed + 2] = 1
    out_shapes = [jax.ShapeDtypeStruct(y_full.shape, y_full.dtype), jax.ShapeDtypeStruct(state.shape, F32)]
    return ins, specs, blk, out_shapes, aliases


def _rotary(t, cos, sin_signed):
    even = (_iota(t.shape, 1) & 1) == 0
    nxt = pltpu.roll(t, LANES - 1, axis=1)
    prv = pltpu.roll(t, 1, axis=1)
    return t * cos + jnp.where(even, nxt, prv) * sin_signed


def _ret_decays(c, nvalid, lg):
    i = _iota((c, c), 0)
    j = _iota((c, c), 1)
    diff = i - j
    d_intra = jnp.where(diff >= 0, jnp.exp(lg * jnp.maximum(diff, 0).astype(F32)), 0.0)
    pos_i = _iota((c, 1), 0)
    pos = pos_i.astype(F32)
    d_q = jnp.exp(lg * (pos + 1.0))
    d_k = jnp.where(pos_i < nvalid, jnp.exp(lg * (nvalid - 1.0 - pos)), 0.0)
    d_c = math.exp(lg * nvalid)
    return d_intra, d_q, d_k, d_c


def _ret_lg(h):
    return math.log1p(-(2.0 ** (-5.0 - h)))


def _ret_chunks(qs, ks, vs, gs, ss, cos, sin, decs):
    n = range(len(qs))
    qr = [_rotary(q, cos, sin) for q in qs]
    kr = [_rotary(k, cos, sin) * (RET_DK ** -0.5) for k in ks]
    sc = [_dot_nt(qr[p], kr[p]) * decs[p][0] for p in n]
    o_st = [_dot(qr[p] * decs[p][1], ss[p]) for p in n]
    kv = [_dot_tn(kr[p] * decs[p][2], vs[p]) for p in n]
    o = [_dot(sc[p], vs[p]) + o_st[p] for p in n]
    s_new = [decs[p][3] * ss[p] + kv[p] for p in n]
    ys = [_rms(o[p]) * _silu(gs[p]) for p in n]
    return ys, s_new


def _ret_prompt_kernel(q_ref, k_ref, v_ref, g_ref, cos_ref, sin_ref, y_ref, so_ref, s_ref):
    c = pl.program_id(1)

    @pl.when(c == 0)
    def _():
        s_ref[...] = jnp.zeros_like(s_ref)

    hs = range(RET_HEADS)
    decs = [_ret_decays(RET_CHUNK, RET_CHUNK, _ret_lg(h)) for h in hs]
    ys, s_new = _ret_chunks([q_ref[:, h * RET_DK:(h + 1) * RET_DK] for h in hs],
                            [k_ref[:, h * RET_DK:(h + 1) * RET_DK] for h in hs],
                            [v_ref[:, h * RET_DV:(h + 1) * RET_DV] for h in hs],
                            [g_ref[:, h * RET_DV:(h + 1) * RET_DV] for h in hs],
                            [s_ref[h] for h in hs], cos_ref[...], sin_ref[...], decs)
    for h in hs:
        y_ref[:, h * RET_DV:(h + 1) * RET_DV] = ys[h].astype(BF16)
        s_ref[h] = s_new[h]

    @pl.when(c == pl.num_programs(1) - 1)
    def _():
        so_ref[...] = s_ref[...]


def _ret_prompt(p_main, cos, sin, nb, nc):
    qw = RET_HEADS * RET_DK
    vw = RET_HEADS * RET_DV
    return pl.pallas_call(
        _ret_prompt_kernel,
        grid=(nb, nc),
        in_specs=[
            pl.BlockSpec((RET_CHUNK, qw), lambda b, c: (b * nc + c, P_RQ // qw)),
            pl.BlockSpec((RET_CHUNK, qw), lambda b, c: (b * nc + c, P_RK // qw)),
            pl.BlockSpec((RET_CHUNK, vw), lambda b, c: (b * nc + c, P_RV // vw)),
            pl.BlockSpec((RET_CHUNK, vw), lambda b, c: (b * nc + c, P_RG // vw)),
            pl.BlockSpec((RET_CHUNK, LANES), lambda b, c: (c, 0)),
            pl.BlockSpec((RET_CHUNK, LANES), lambda b, c: (c, 0)),
        ],
        out_specs=[
            pl.BlockSpec((RET_CHUNK, vw), lambda b, c: (b * nc + c, 0)),
            pl.BlockSpec((None, RET_HEADS, RET_DK, RET_DV), lambda b, c: (b, 0, 0, 0)),
        ],
        out_shape=[jax.ShapeDtypeStruct((p_main.shape[0], vw), BF16),
                   jax.ShapeDtypeStruct((nb, RET_HEADS, RET_DK, RET_DV), F32)],
        scratch_shapes=[pltpu.VMEM((RET_HEADS, RET_DK, RET_DV), F32)],
        compiler_params=_cparams(("arbitrary", "arbitrary")),
        name="ret_prompt",
    )(p_main, p_main, p_main, p_main, cos, sin)


def _ret_sample_kernel(q_ref, k_ref, v_ref, g_ref, cos_ref, sin_ref, si_ref, *rest, lseg):
    y_ref, so_ref = rest[-2:]
    npair = q_ref.shape[0] // SAMPLE_SLAB
    cos = cos_ref[...]
    sin = sin_ref[...]
    hs = range(RET_HEADS)
    decs = [_ret_decays(SAMPLE_SLAB, lseg, _ret_lg(h)) for h in hs] * (2 * SAMPLE_PAIRS)
    order = [(u, t, h) for u in range(SAMPLE_PAIRS) for t in range(2) for h in hs]

    def pairs(it, carry):
        ps = [it * SAMPLE_PAIRS + u for u in range(SAMPLE_PAIRS)]
        rows = [pl.ds(pl.multiple_of(p * SAMPLE_SLAB, SAMPLE_SLAB), SAMPLE_SLAB) for p in ps]

        def slabs(ref, w):
            return [[_slab_pair(ref[r, h * w:(h + 1) * w]) for h in hs] for r in rows]

        qs, ks, vs, gs = slabs(q_ref, RET_DK), slabs(k_ref, RET_DK), slabs(v_ref, RET_DV), slabs(g_ref, RET_DV)
        ys, s_new = _ret_chunks([qs[u][h][t] for u, t, h in order], [ks[u][h][t] for u, t, h in order],
                                [vs[u][h][t] for u, t, h in order], [gs[u][h][t] for u, t, h in order],
                                [si_ref[2 * ps[u] + t, h] for u, t, h in order], cos, sin, decs)
        for idx, (u, t, h) in enumerate(order):
            so_ref[2 * ps[u] + t, h] = s_new[idx]
        for u in range(SAMPLE_PAIRS):
            base = u * 2 * RET_HEADS
            for h in hs:
                y_ref[rows[u], h * RET_DV:(h + 1) * RET_DV] = _slab_merge(
                    ys[base + h], ys[base + RET_HEADS + h]).astype(BF16)
        return carry

    lax.fori_loop(0, npair // SAMPLE_PAIRS, pairs, 0)


def _ret_sample(p_main, cos, sin, state, so_prev, y_full, layer, row0, nseq, lseg):
    qw = RET_HEADS * RET_DK
    vw = RET_HEADS * RET_DV
    rb = SAMPLE_SEQS * lseg
    blk0 = row0 // rb
    st_in, st_specs, st_out_spec, out_shapes, aliases = _sample_io(state, so_prev, y_full, layer, RET_HEADS,
                                                                   RET_DK, RET_DV, 6)
    return pl.pallas_call(
        functools.partial(_ret_sample_kernel, lseg=lseg),
        grid=(nseq // SAMPLE_SEQS,),
        in_specs=[
            pl.BlockSpec((rb, qw), lambda i: (blk0 + i, P_RQ // qw)),
            pl.BlockSpec((rb, qw), lambda i: (blk0 + i, P_RK // qw)),
            pl.BlockSpec((rb, vw), lambda i: (blk0 + i, P_RV // vw)),
            pl.BlockSpec((rb, vw), lambda i: (blk0 + i, P_RG // vw)),
            pl.BlockSpec((SAMPLE_SLAB, LANES), lambda i: (0, 0)),
            pl.BlockSpec((SAMPLE_SLAB, LANES), lambda i: (0, 0)),
        ] + st_specs,
        out_specs=[pl.BlockSpec((rb, vw), lambda i: (blk0 + i, 0)), st_out_spec],
        out_shape=out_shapes,
        input_output_aliases=aliases,
        compiler_params=_cparams(("arbitrary",)),
        name="ret_sample",
    )(p_main, p_main, p_main, p_main, cos, sin, *st_in)


def _unit_lower_inverse_all(a_list, nvalid):
    c = a_list[0].shape[0]
    n = range(len(a_list))
    i = _iota((c, c), 0)
    j = _iota((c, c), 1)
    eye = (i == j).astype(F32)
    pair = (i >> 1) == (j >> 1)
    ts = [eye - jnp.where(pair, a, 0.0) for a in a_list]
    blk = 2
    while blk < min(c, nvalid):
        sh = blk.bit_length()
        m = ((i >> sh) == (j >> sh)) & ((i & blk) != 0) & ((j & blk) == 0)
        xs = [jnp.where(m, a, 0.0).astype(BF16) for a in a_list]
        tsp = [_split2(t) for t in ts]
        tx = [_mm(tsp[p][0], xs[p]) + _mm(tsp[p][1], xs[p]) for p in n]
        txs = [_split2(y) for y in tx]
        r0 = [_mm(txs[p][0], tsp[p][0]) for p in n]
        r1 = [_mm(txs[p][0], tsp[p][1]) for p in n]
        r2 = [_mm(txs[p][1], tsp[p][0]) for p in n]
        ts = [ts[p] - (r0[p] + (r1[p] + r2[p])) for p in n]
        blk *= 2
    return ts


def _gdn_gates(small, small_t, alog_r, dtb_r, alog_c, dtb_c, c):
    r = small.shape[0]
    g = -jnp.exp(alog_r) * jax.nn.softplus(small + dtb_r)
    g_t = -jnp.exp(alog_c) * jax.nn.softplus(small_t + dtb_c)
    i = _iota((r, r), 0)
    j = _iota((r, r), 1)
    sh = c.bit_length() - 1
    same = ((i >> sh) == (j >> sh)) if r > c else True
    cum = _dot_mask_l((same & (j <= i)).astype(F32), g)
    cum_t = _dot_mask_r(g_t, (same & (i <= j)).astype(F32))
    beta = jax.nn.sigmoid(small)
    return cum, cum_t, beta


def _gdn_prep(qs, ks, vs, cum_cs, cum_rs, beta_cs, nvalid):
    n = range(len(qs))
    c = qs[0].shape[0]
    i = _iota((c, c), 0)
    j = _iota((c, c), 1)
    rowv = _iota((c, 1), 0) < nvalid
    qn = [q * lax.rsqrt(jnp.sum(q * q, axis=-1, keepdims=True) + EPS) * (GDN_DK ** -0.5) for q in qs]
    kn = [k * lax.rsqrt(jnp.sum(k * k, axis=-1, keepdims=True) + EPS) for k in ks]
    dec = [jnp.exp(jnp.where(j <= i, cum_cs[p] - cum_rs[p], -jnp.inf)) for p in n]
    e_c = [jnp.exp(cum_cs[p]) for p in n]
    kb = [kn[p] * beta_cs[p] for p in n]
    kk = [_dot_nt(kb[p], kn[p]) for p in n]
    qk = [_dot_nt(qn[p], kn[p]) for p in n]
    lower = [jnp.where(j < i, kk[p] * dec[p], 0.0) for p in n]
    ts = _unit_lower_inverse_all(lower, nvalid)
    rhs = [jnp.concatenate([vs[p] * beta_cs[p], kb[p] * e_c[p]], axis=1) for p in n]
    sol = [_dot(ts[p], rhs[p]) for p in n]
    cum_last = [cum_cs[p][nvalid - 1:nvalid, :] for p in n]
    out = []
    for p in n:
        wq = jnp.concatenate([sol[p][:, GDN_DV:], qn[p] * e_c[p]], axis=0).astype(BF16)
        kd = (kn[p] * jnp.where(rowv, jnp.exp(cum_last[p] - cum_cs[p]), 0.0)).astype(BF16)
        out.append((wq, sol[p][:, :GDN_DV], (qk[p] * dec[p]).astype(BF16), kd, jnp.exp(cum_last[p])))
    return out


def _gdn_seq(prep, zs, ss, norm_g):
    n = range(len(prep))
    c = prep[0][1].shape[0]
    st = [_mm(prep[p][0], ss[p].astype(BF16)) for p in n]
    v_new = [prep[p][1] - st[p][:c] for p in n]
    o = [st[p][c:] + _dot(prep[p][2], v_new[p]) for p in n]
    kv = [_dot_tn(prep[p][3], v_new[p]) for p in n]
    s_new = [prep[p][4] * ss[p] + kv[p] for p in n]
    ys = [_rms(o[p]) * norm_g * _silu(zs[p]) for p in n]
    return ys, s_new


def _gdn_conv(xcat, cw):
    acc = xcat[SUBLANES:, :] * cw[GDN_CONV - 1:GDN_CONV, :]
    for w in range(GDN_CONV - 1):
        sh = GDN_CONV - 1 - w
        acc = acc + pltpu.roll(xcat, sh, axis=0)[SUBLANES:, :] * cw[w:w + 1, :]
    return _silu(acc)


def _gdn_split(conv, h):
    hk = GDN_HEADS * GDN_DK
    return (conv[:, h * GDN_DK:(h + 1) * GDN_DK], conv[:, hk + h * GDN_DK: hk + (h + 1) * GDN_DK],
            conv[:, 2 * hk + h * GDN_DV: 2 * hk + (h + 1) * GDN_DV])


def _gdn_prompt_kernel(x_ref, z_ref, sm_ref, smt_ref, cw_ref, alr_ref, dtr_ref, alc_ref, dtc_ref, ng_ref,
                       y_ref, so_ref, s_ref, halo_ref):
    c = pl.program_id(1)

    @pl.when(c == 0)
    def _():
        s_ref[...] = jnp.zeros_like(s_ref)
        halo_ref[...] = jnp.zeros_like(halo_ref)

    x = x_ref[...]
    rows = x.shape[0]
    xcat = jnp.concatenate([halo_ref[...], x], axis=0)
    halo_ref[...] = x[rows - SUBLANES:, :]
    conv = _gdn_conv(xcat, cw_ref[...])
    cum, cum_t, beta = _gdn_gates(sm_ref[...], smt_ref[...], alr_ref[...], dtr_ref[...],
                                  alc_ref[...], dtc_ref[...], CHUNK)
    hs = range(GDN_HEADS)
    order = [(ch, h) for ch in range(rows // CHUNK) for h in hs]

    def rws(a, ch):
        return a[ch * CHUNK:(ch + 1) * CHUNK]

    qkv = [_gdn_split(rws(conv, ch), h) for ch, h in order]
    prep = _gdn_prep([t[0] for t in qkv], [t[1] for t in qkv], [t[2] for t in qkv],
                     [rws(cum, ch)[:, S_DA + h:S_DA + h + 1] for ch, h in order],
                     [cum_t[S_DA + h:S_DA + h + 1, ch * CHUNK:(ch + 1) * CHUNK] for ch, h in order],
                     [rws(beta, ch)[:, S_DB + h:S_DB + h + 1] for ch, h in order], CHUNK)
    ss = [s_ref[h] for h in hs]
    for ch in range(rows // CHUNK):
        ys, ss = _gdn_seq(prep[ch * GDN_HEADS:(ch + 1) * GDN_HEADS],
                          [z_ref[ch * CHUNK:(ch + 1) * CHUNK, h * GDN_DV:(h + 1) * GDN_DV] for h in hs],
                          ss, ng_ref[...])
        for h in hs:
            y_ref[ch * CHUNK:(ch + 1) * CHUNK, h * GDN_DV:(h + 1) * GDN_DV] = ys[h].astype(BF16)
    for h in hs:
        s_ref[h] = ss[h]

    @pl.when(c == pl.num_programs(1) - 1)
    def _():
        so_ref[...] = s_ref[...]


def _gdn_prompt(p_main, p_small, p_small_t, cw, alr, dtr, alc, dtc, ng, layer, nb, nc):
    zw = GDN_HEADS * GDN_DV
    rb = MIX_CHUNKS * CHUNK
    return pl.pallas_call(
        _gdn_prompt_kernel,
        grid=(nb, nc),
        in_specs=[
            pl.BlockSpec((rb, GDN_QKV), lambda b, c: (b * nc + c, P_DQKV // GDN_QKV)),
            pl.BlockSpec((rb, zw), lambda b, c: (b * nc + c, P_DZ // zw)),
            pl.BlockSpec((rb, LANES), lambda b, c: (b * nc + c, 0)),
            pl.BlockSpec((None, LANES, rb), lambda b, c: (b * nc + c, 0, 0)),
            pl.BlockSpec((None, GDN_CONV, GDN_QKV), lambda b, c: (layer, 0, 0)),
            pl.BlockSpec((None, 1, LANES), lambda b, c: (layer, 0, 0)),
            pl.BlockSpec((None, 1, LANES), lambda b, c: (layer, 0, 0)),
            pl.BlockSpec((None, LANES, 1), lambda b, c: (layer, 0, 0)),
            pl.BlockSpec((None, LANES, 1), lambda b, c: (layer, 0, 0)),
            pl.BlockSpec((None, 1, GDN_DV), lambda b, c: (layer, 0, 0)),
        ],
        out_specs=[
            pl.BlockSpec((rb, zw), lambda b, c: (b * nc + c, 0)),
            pl.BlockSpec((None, GDN_HEADS, GDN_DK, GDN_DV), lambda b, c: (b, 0, 0, 0)),
        ],
        out_shape=[jax.ShapeDtypeStruct((p_main.shape[0], zw), BF16),
                   jax.ShapeDtypeStruct((nb, GDN_HEADS, GDN_DK, GDN_DV), F32)],
        scratch_shapes=[pltpu.VMEM((GDN_HEADS, GDN_DK, GDN_DV), F32), pltpu.VMEM((SUBLANES, GDN_QKV), F32)],
        compiler_params=_cparams(("arbitrary", "arbitrary")),
        name="gdn_prompt",
    )(p_main, p_main, p_small, p_small_t, cw, alr, dtr, alc, dtc, ng)


def _gdn_sample_kernel(x_ref, z_ref, sm_ref, smt_ref, sc_ref, cw_ref, alr_ref, dtr_ref, alc_ref, dtc_ref, ng_ref,
                       si_ref, *rest, lseg):
    y_ref, so_ref = rest[-2:]
    npair = x_ref.shape[0] // SAMPLE_SLAB
    cw = cw_ref[...]
    hs = range(GDN_HEADS)

    seqs = [(u, t) for u in range(SAMPLE_PAIRS) for t in range(2)]
    order = [(u, t, h) for u, t in seqs for h in hs]

    def pairs(it, carry):
        ps = [it * SAMPLE_PAIRS + u for u in range(SAMPLE_PAIRS)]
        rows = [pl.ds(pl.multiple_of(p * SAMPLE_SLAB, SAMPLE_SLAB), SAMPLE_SLAB) for p in ps]
        xs = [_slab_pair(x_ref[r, :]) for r in rows]
        sms = [_slab_pair(sm_ref[r, :]) for r in rows]
        zs = [[_slab_pair(z_ref[r, h * GDN_DV:(h + 1) * GDN_DV]) for h in hs] for r in rows]
        convs, gates = {}, {}
        for u, t in seqs:
            xcat = jnp.concatenate([sc_ref[2 * ps[u] + t], xs[u][t]], axis=0)
            convs[u, t] = _gdn_conv(xcat, cw)
            gates[u, t] = _gdn_gates(sms[u][t], smt_ref[2 * ps[u] + t], alr_ref[...], dtr_ref[...],
                                     alc_ref[...], dtc_ref[...], SAMPLE_SLAB)
        qkv = [_gdn_split(convs[u, t], h) for u, t, h in order]
        prep = _gdn_prep([a[0] for a in qkv], [a[1] for a in qkv], [a[2] for a in qkv],
                         [gates[u, t][0][:, S_DA + h:S_DA + h + 1] for u, t, h in order],
                         [gates[u, t][1][S_DA + h:S_DA + h + 1, :] for u, t, h in order],
                         [gates[u, t][2][:, S_DB + h:S_DB + h + 1] for u, t, h in order], lseg)
        ys, s_new = _gdn_seq(prep, [zs[u][h][t] for u, t, h in order],
                             [si_ref[2 * ps[u] + t, h] for u, t, h in order], ng_ref[...])
        for idx, (u, t, h) in enumerate(order):
            so_ref[2 * ps[u] + t, h] = s_new[idx]
        for u in range(SAMPLE_PAIRS):
            base = u * 2 * GDN_HEADS
            for h in hs:
                y_ref[rows[u], h * GDN_DV:(h + 1) * GDN_DV] = _slab_merge(
                    ys[base + h], ys[base + GDN_HEADS + h]).astype(BF16)
        return carry

    lax.fori_loop(0, npair // SAMPLE_PAIRS, pairs, 0)


def _gdn_sample(p_main, p_small, p_small_t, sconv, cw, alr, dtr, alc, dtc, ng, state, so_prev, y_full, layer, row0,
                nseq, lseg):
    zw = GDN_HEADS * GDN_DV
    rb = SAMPLE_SEQS * lseg
    blk0 = row0 // rb
    st_in, st_specs, st_out_spec, out_shapes, aliases = _sample_io(state, so_prev, y_full, layer, GDN_HEADS,
                                                                   GDN_DK, GDN_DV, 11)
    return pl.pallas_call(
        functools.partial(_gdn_sample_kernel, lseg=lseg),
        grid=(nseq // SAMPLE_SEQS,),
        in_specs=[
            pl.BlockSpec((rb, GDN_QKV), lambda i: (blk0 + i, P_DQKV // GDN_QKV)),
            pl.BlockSpec((rb, zw), lambda i: (blk0 + i, P_DZ // zw)),
            pl.BlockSpec((rb, LANES), lambda i: (blk0 + i, 0)),
            pl.BlockSpec((SAMPLE_SEQS, LANES, SAMPLE_SLAB), lambda i: (i, 0, 0)),
            pl.BlockSpec((SAMPLE_SEQS, SUBLANES, GDN_QKV), lambda i: (i, 0, 0)),
            pl.BlockSpec((None, GDN_CONV, GDN_QKV), lambda i: (layer, 0, 0)),
            pl.BlockSpec((None, 1, LANES), lambda i: (layer, 0, 0)),
            pl.BlockSpec((None, 1, LANES), lambda i: (layer, 0, 0)),
            pl.BlockSpec((None, LANES, 1), lambda i: (layer, 0, 0)),
            pl.BlockSpec((None, LANES, 1), lambda i: (layer, 0, 0)),
            pl.BlockSpec((None, 1, GDN_DV), lambda i: (layer, 0, 0)),
        ] + st_specs,
        out_specs=[pl.BlockSpec((rb, zw), lambda i: (blk0 + i, 0)), st_out_spec],
        out_shape=out_shapes,
        input_output_aliases=aliases,
        compiler_params=_cparams(("arbitrary",)),
        name="gdn_sample",
    )(p_main, p_main, p_small, p_small_t, sconv, cw, alr, dtr, alc, dtc, ng, *st_in)


def _gla_scores(q, k, b, sub):
    c = q.shape[0]
    irow = _iota((sub, 1), 0)
    jrow_all = _iota((c, 1), 0)
    lane = _iota((sub, c), 1)
    blocks = []
    for blk in range(c // sub):
        r0 = blk * sub
        qi, ki, bi = q[r0:r0 + sub], k[r0:r0 + sub], b[r0:r0 + sub]
        if blk > 0:
            b0 = b[r0 - 1:r0, :]
            qt = qi * jnp.exp(bi - b0)
            kt = k * jnp.exp(jnp.where(jrow_all < r0, b0 - b, -jnp.inf))
            a = _dot_nt(qt, kt)
        else:
            a = jnp.zeros((sub, c), F32)
        for jj in range(sub):
            e = jnp.exp(jnp.where(irow >= jj, bi - bi[jj:jj + 1, :], -jnp.inf))
            col = jnp.sum(qi * e * ki[jj:jj + 1, :], axis=1, keepdims=True)
            a = jnp.where(lane == r0 + jj, col, a)
        blocks.append(a)
    return jnp.concatenate(blocks, axis=0) if len(blocks) > 1 else blocks[0]


def _gla_prep(qs, ks, vs, bs, nvalid, sub):
    n = range(len(qs))
    c = qs[0].shape[0]
    rowv = _iota((c, 1), 0) < nvalid
    qsc = [q * (GLA_DK ** -0.5) for q in qs]
    b_last = [bs[p][nvalid - 1:nvalid, :] for p in n]
    kv = [_dot_tn(ks[p] * jnp.where(rowv, jnp.exp(b_last[p] - bs[p]), 0.0), vs[p]) for p in n]
    a = [_gla_scores(qsc[p], ks[p], bs[p], sub) for p in n]
    o_in = [_dot(a[p], vs[p]) for p in n]
    return [((qsc[p] * jnp.exp(bs[p])).astype(BF16), o_in[p], kv[p], _row_to_col(jnp.exp(b_last[p]))) for p in n]


def _gla_seq(prep, zs, ss, norm_g):
    n = range(len(prep))
    o = [prep[p][1] + _mm(prep[p][0], ss[p].astype(BF16)) for p in n]
    s_new = [prep[p][3] * ss[p] + prep[p][2] for p in n]
    ys = [_rms(o[p]) * norm_g * _silu(zs[p]) for p in n]
    return ys, s_new


def _gla_cumsum(gk, c):
    r = gk.shape[0]
    i = _iota((r, r), 0)
    j = _iota((r, r), 1)
    sh = c.bit_length() - 1
    same = ((i >> sh) == (j >> sh)) if r > c else True
    return _dot_mask_l((same & (j <= i)).astype(F32), gk)


def _gla_gk(small, wup, bup):
    return jax.nn.log_sigmoid(_mm(small.astype(BF16), wup) + bup) / GLA_GATE_NORM


def _gla_prompt_kernel(q_ref, k_ref, v_ref, z_ref, sm_ref, wup_ref, bup_ref, ng_ref, y_ref, so_ref, s_ref):
    c = pl.program_id(1)

    @pl.when(c == 0)
    def _():
        s_ref[...] = jnp.zeros_like(s_ref)

    rows = q_ref.shape[0]
    b_all = _gla_cumsum(_gla_gk(sm_ref[...], wup_ref[...], bup_ref[...]), CHUNK)
    hs = range(GLA_HEADS)
    order = [(ch, h) for ch in range(rows // CHUNK) for h in hs]

    def blk(ref, ch, h, w):
        return ref[ch * CHUNK:(ch + 1) * CHUNK, h * w:(h + 1) * w]

    prep = _gla_prep([blk(q_ref, ch, h, GLA_DK) for ch, h in order], [blk(k_ref, ch, h, GLA_DK) for ch, h in order],
                     [blk(v_ref, ch, h, GLA_DV) for ch, h in order], [blk(b_all, ch, h, GLA_DK) for ch, h in order],
                     CHUNK, GLA_SUB)
    ss = [s_ref[h] for h in hs]
    for ch in range(rows // CHUNK):
        ys, ss = _gla_seq(prep[ch * GLA_HEADS:(ch + 1) * GLA_HEADS], [blk(z_ref, ch, h, GLA_DV) for h in hs],
                          ss, ng_ref[...])
        for h in hs:
            y_ref[ch * CHUNK:(ch + 1) * CHUNK, h * GLA_DV:(h + 1) * GLA_DV] = ys[h].astype(BF16)
    for h in hs:
        s_ref[h] = ss[h]

    @pl.when(c == pl.num_programs(1) - 1)
    def _():
        so_ref[...] = s_ref[...]


def _gla_prompt(p_main, p_small, wup, bup, ng, layer, nb, nc):
    qw = GLA_HEADS * GLA_DK
    vw = GLA_HEADS * GLA_DV
    rb = MIX_CHUNKS * CHUNK
    return pl.pallas_call(
        _gla_prompt_kernel,
        grid=(nb, nc),
        in_specs=[
            pl.BlockSpec((rb, qw), lambda b, c: (b * nc + c, P_LQ // qw)),
            pl.BlockSpec((rb, qw), lambda b, c: (b * nc + c, P_LK // qw)),
            pl.BlockSpec((rb, vw), lambda b, c: (b * nc + c, P_LV // vw)),
            pl.BlockSpec((rb, vw), lambda b, c: (b * nc + c, P_LGT // vw)),
            pl.BlockSpec((rb, LANES), lambda b, c: (b * nc + c, 0)),
            pl.BlockSpec((None, LANES, qw), lambda b, c: (layer, 0, 0)),
            pl.BlockSpec((None, 1, qw), lambda b, c: (layer, 0, 0)),
            pl.BlockSpec((None, 1, GLA_DV), lambda b, c: (layer, 0, 0)),
        ],
        out_specs=[
            pl.BlockSpec((rb, vw), lambda b, c: (b * nc + c, 0)),
            pl.BlockSpec((None, GLA_HEADS, GLA_DK, GLA_DV), lambda b, c: (b, 0, 0, 0)),
        ],
        out_shape=[jax.ShapeDtypeStruct((p_main.shape[0], vw), BF16),
                   jax.ShapeDtypeStruct((nb, GLA_HEADS, GLA_DK, GLA_DV), F32)],
        scratch_shapes=[pltpu.VMEM((GLA_HEADS, GLA_DK, GLA_DV), F32)],
        compiler_params=_cparams(("arbitrary", "arbitrary")),
        name="gla_prompt",
    )(p_main, p_main, p_main, p_main, p_small, wup, bup, ng)


def _gla_sample_kernel(q_ref, k_ref, v_ref, z_ref, sm_ref, wup_ref, bup_ref, ng_ref, si_ref, *rest, lseg):
    y_ref, so_ref = rest[-2:]
    npair = q_ref.shape[0] // SAMPLE_SLAB
    hs = range(GLA_HEADS)

    order = [(u, t, h) for u in range(SAMPLE_PAIRS) for t in range(2) for h in hs]

    def pairs(it, carry):
        ps = [it * SAMPLE_PAIRS + u for u in range(SAMPLE_PAIRS)]
        rows = [pl.ds(pl.multiple_of(p * SAMPLE_SLAB, SAMPLE_SLAB), SAMPLE_SLAB) for p in ps]

        def slabs(ref, w):
            return [[_slab_pair(ref[r, h * w:(h + 1) * w]) for h in hs] for r in rows]

        qs, ks, vs, zs = slabs(q_ref, GLA_DK), slabs(k_ref, GLA_DK), slabs(v_ref, GLA_DV), slabs(z_ref, GLA_DV)
        bs = [[_gla_cumsum(g, SAMPLE_SLAB) for g in _slab_pair(_gla_gk(sm_ref[r, :], wup_ref[...], bup_ref[...]))]
              for r in rows]
        prep = _gla_prep([qs[u][h][t] for u, t, h in order], [ks[u][h][t] for u, t, h in order],
                         [vs[u][h][t] for u, t, h in order],
                         [bs[u][t][:, h * GLA_DK:(h + 1) * GLA_DK] for u, t, h in order], lseg, SAMPLE_SLAB)
        ys, s_new = _gla_seq(prep, [zs[u][h][t] for u, t, h in order],
                             [si_ref[2 * ps[u] + t, h] for u, t, h in order], ng_ref[...])
        for idx, (u, t, h) in enumerate(order):
            so_ref[2 * ps[u] + t, h] = s_new[idx]
        for u in range(SAMPLE_PAIRS):
            base = u * 2 * GLA_HEADS
            for h in hs:
                y_ref[rows[u], h * GLA_DV:(h + 1) * GLA_DV] = _slab_merge(
                    ys[base + h], ys[base + GLA_HEADS + h]).astype(BF16)
        return carry

    lax.fori_loop(0, npair // SAMPLE_PAIRS, pairs, 0)


def _gla_sample(p_main, p_small, wup, bup, ng, state, so_prev, y_full, layer, row0, nseq, lseg):
    qw = GLA_HEADS * GLA_DK
    vw = GLA_HEADS * GLA_DV
    rb = SAMPLE_SEQS * lseg
    blk0 = row0 // rb
    st_in, st_specs, st_out_spec, out_shapes, aliases = _sample_io(state, so_prev, y_full, layer, GLA_HEADS,
                                                                   GLA_DK, GLA_DV, 8)
    return pl.pallas_call(
        functools.partial(_gla_sample_kernel, lseg=lseg),
        grid=(nseq // SAMPLE_SEQS,),
        in_specs=[
            pl.BlockSpec((rb, qw), lambda i: (blk0 + i, P_LQ // qw)),
            pl.BlockSpec((rb, qw), lambda i: (blk0 + i, P_LK // qw)),
            pl.BlockSpec((rb, vw), lambda i: (blk0 + i, P_LV // vw)),
            pl.BlockSpec((rb, vw), lambda i: (blk0 + i, P_LGT // vw)),
            pl.BlockSpec((rb, LANES), lambda i: (blk0 + i, 0)),
            pl.BlockSpec((None, LANES, qw), lambda i: (layer, 0, 0)),
            pl.BlockSpec((None, 1, qw), lambda i: (layer, 0, 0)),
            pl.BlockSpec((None, 1, GLA_DV), lambda i: (layer, 0, 0)),
        ] + st_specs,
        out_specs=[pl.BlockSpec((rb, vw), lambda i: (blk0 + i, 0)), st_out_spec],
        out_shape=out_shapes,
        input_output_aliases=aliases,
        compiler_params=_cparams(("arbitrary",)),
        name="gla_sample",
    )(p_main, p_main, p_main, p_main, p_small, wup, bup, ng, *st_in)


def _rope_tables(pos0, length):
    inv = 1.0 / (ROPE_BASE ** jnp.linspace(0.0, 1.0, RET_DK // 2, dtype=F32))
    ang = (jnp.arange(length, dtype=F32) + pos0)[:, None] * inv[None, :]
    cos = jnp.repeat(jnp.cos(ang), 2, axis=1)
    sin = jnp.stack([-jnp.sin(ang), jnp.sin(ang)], axis=-1).reshape(length, RET_DK)
    return cos, sin


def _lane_pad(v, off):
    n = v.shape[-1]
    return jnp.pad(v, ((0, 0), (off, LANES - off - n)))[:, None, :]


def kernel(x_prompt, x_sample, state_ret, state_gdn, state_gdn_conv, state_gla, norm_mix, norm_ffn, norm_final,
           w_in, b_merge, gdn_conv_w, gdn_a_log, gdn_dt_bias, gdn_norm, gla_w_up, gla_b_up, gla_norm, w_branch,
           w_o, w_gate_up, w_down):
    nb, seq, d = x_prompt.shape
    nsq, lseg, _ = x_sample.shape
    depth = w_in.shape[0]
    n_p = nb * seq
    n_s = nsq * lseg
    mix_rows = MIX_CHUNKS * CHUNK
    nc = seq // mix_rows
    nc_ret = seq // RET_CHUNK
    tm = _dense_tiles(n_p + n_s)
    tm_out = math.gcd(n_p, n_s)
    assert seq % mix_rows == 0 and seq % RET_CHUNK == 0
    assert nsq % SAMPLE_SEQS == 0 and n_p % (SAMPLE_SEQS * lseg) == 0
    assert 2 * lseg == SAMPLE_SLAB and lseg >= GDN_CONV - 1

    w_t = jnp.swapaxes(w_in, 1, 2)
    w_main = _pack_w_in(w_t, 512, 1024)
    w_small = _pack_w_small(w_t, 512)
    wb = w_branch.astype(BF16)
    wo = w_o.astype(BF16)
    wgu = w_gate_up.astype(BF16)
    wdn = w_down.astype(BF16)
    g_mix = norm_mix[:, None, :]
    g_ffn = norm_ffn[:, None, :]
    bm = b_merge.reshape(depth, N_BRANCH, 1, D_MODEL)
    alr = _lane_pad(gdn_a_log, S_DA)
    dtr = _lane_pad(gdn_dt_bias, S_DA)
    alc = jnp.swapaxes(alr, 1, 2)
    dtc = jnp.swapaxes(dtr, 1, 2)
    gdn_ng = gdn_norm[:, None, :]
    gla_ng = gla_norm[:, None, :]
    wup = jnp.pad(gla_w_up, ((0, 0), (S_LLR, LANES - S_LLR - GLA_LOWRANK), (0, 0))).astype(BF16)
    bup = gla_b_up[:, None, :]
    cos_p, sin_p = _rope_tables(0.0, seq)
    cos_s, sin_s = _rope_tables(float(PAST_LEN), lseg)
    cos_s = jnp.tile(cos_s, (SAMPLE_SLAB // lseg, 1))
    sin_s = jnp.tile(sin_s, (SAMPLE_SLAB // lseg, 1))

    x = jnp.concatenate([x_prompt.reshape(n_p, d), x_sample.reshape(n_s, d)], axis=0)

    outs = {k: [] for k in ("p_ret", "p_gdn", "p_conv", "p_gla", "s_conv")}
    s_ret = s_gdn = s_gla = None
    for layer in range(depth):
        p_main, p_small = _in_proj(x, g_mix, w_main, w_small, layer, tm, 2048)
        pst_p = jnp.swapaxes(p_small[:n_p].reshape(nb * nc, mix_rows, LANES), 1, 2)
        pst_s = jnp.swapaxes(jnp.pad(p_small[n_p:].reshape(nsq, lseg, LANES),
                                     ((0, 0), (0, SAMPLE_SLAB - lseg), (0, 0))), 1, 2)

        y_ret, st = _ret_prompt(p_main, cos_p, sin_p, nb, nc_ret)
        outs["p_ret"].append(st)
        y_ret, s_ret = _ret_sample(p_main, cos_s, sin_s, state_ret, s_ret, y_ret, layer, n_p, nsq, lseg)

        y_gdn, st = _gdn_prompt(p_main, p_small, pst_p, gdn_conv_w, alr, dtr, alc, dtc, gdn_ng, layer, nb, nc)
        outs["p_gdn"].append(st)
        sconv = jnp.pad(state_gdn_conv[layer], ((0, 0), (SUBLANES - (GDN_CONV - 1), 0), (0, 0)))
        y_gdn, s_gdn = _gdn_sample(p_main, p_small, pst_s, sconv, gdn_conv_w, alr, dtr, alc, dtc, gdn_ng,
                                   state_gdn, s_gdn, y_gdn, layer, n_p, nsq, lseg)
        outs["p_conv"].append(jnp.stack([
            lax.slice(p_main, ((b + 1) * seq - (GDN_CONV - 1), P_DQKV), ((b + 1) * seq, P_DQKV + GDN_QKV))
            for b in range(nb)]))
        dq_s = lax.slice(p_main, (n_p, P_DQKV), (n_p + n_s, P_DQKV + GDN_QKV)).reshape(nsq, lseg, GDN_QKV)
        outs["s_conv"].append(jnp.concatenate([state_gdn_conv[layer], dq_s], axis=1)[:, -(GDN_CONV - 1):])

        y_gla, st = _gla_prompt(p_main, p_small, wup, bup, gla_ng, layer, nb, nc)
        outs["p_gla"].append(st)
        y_gla, s_gla = _gla_sample(p_main, p_small, wup, bup, gla_ng, state_gla, s_gla, y_gla, layer, n_p, nsq, lseg)

        mrg = _merge((y_ret, y_gdn, y_gla), wb, p_main, bm, layer, tm, 512)
        x = _out_proj(mrg, wo, x, layer, tm, 1024)
        x = _ffn(x, g_ffn, wgu, wdn, layer, tm, 512)

    g_fin = norm_final[None, :]
    y_p = _final_norm(x, g_fin, 0, n_p, tm_out)
    y_s = _final_norm(x, g_fin, n_p, n_s, tm_out)
    st = {k: jnp.stack(v) for k, v in outs.items()}
    return (y_p.reshape(nb, seq, d), y_s.reshape(nsq, lseg, d),
            st["p_ret"], st["p_gdn"], st["p_conv"], st["p_gla"],
            s_ret, s_gdn, st["s_conv"], s_gla)
```

```python
import functools
import math

import jax
import jax.numpy as jnp
from jax import lax
from jax.experimental import pallas as pl
from jax.experimental.pallas import tpu as pltpu

F32 = jnp.float32
BF16 = jnp.bfloat16

D_MODEL = 2048
RET_HEADS, RET_DK, RET_DV = 4, 128, 256
GDN_HEADS, GDN_DK, GDN_DV, GDN_CONV = 8, 128, 128, 4
GDN_QKV = GDN_HEADS * (2 * GDN_DK + GDN_DV)
GLA_HEADS, GLA_DK, GLA_DV, GLA_LOWRANK = 4, 128, 256, 16
GLA_GATE_NORM = 16.0
N_BRANCH = 3
BRANCH_WIDTH = 1024
EPS = 1e-6
ROPE_BASE = 10000.0
PAST_LEN = 16384

LANES = 128
SUBLANES = 8
CHUNK = 64
MIX_CHUNKS = 4
RET_CHUNK = 256
GLA_SUB = 8
SAMPLE_SLAB = 8
SAMPLE_SEQS = 8
SAMPLE_PAIRS = 2
VMEM_LIMIT = 56 * 1024 * 1024

W_SEGS = ((0, 6144), (6160, 9232), (9248, 16416))
W_SMALL = ((6144, 6160), (9232, 9248))
P_RQ, P_RK, P_RV, P_RG = 0, 512, 1024, 2048
P_DQKV, P_DZ = 3072, 6144
P_LQ, P_LK, P_LV, P_LGT, P_MG = 7168, 7680, 8192, 9216, 10240
P_MAIN = 16384
S_DA, S_DB, S_LLR = 0, 8, 16


def _cparams(sem):
    return pltpu.CompilerParams(dimension_semantics=sem, vmem_limit_bytes=VMEM_LIMIT)


def _dense_tiles(m):
    best = 64
    for t in range(64, 1089, 64):
        if m % t == 0:
            best = t
    return best


_mm = functools.partial(jnp.dot, preferred_element_type=F32)


def _dot(a, b):
    return _mm(a.astype(BF16), b.astype(BF16))


def _dot_nt(a, b):
    return lax.dot_general(a.astype(BF16), b.astype(BF16), (((1,), (1,)), ((), ())),
                           preferred_element_type=F32)


def _dot_tn(a, b):
    return lax.dot_general(a.astype(BF16), b.astype(BF16), (((0,), (0,)), ((), ())),
                           preferred_element_type=F32)


def _split2(x):
    hi = x.astype(BF16)
    lo = (x - hi.astype(F32)).astype(BF16)
    return hi, lo


def _split3(x):
    hi = x.astype(BF16)
    r = x - hi.astype(F32)
    mid = r.astype(BF16)
    lo = (r - mid.astype(F32)).astype(BF16)
    return hi, mid, lo


def _dot_hi_all(a_list, b_list):
    sa = [_split2(a) for a in a_list]
    sb = [_split2(b) for b in b_list]
    r0 = [_mm(a[0], b[0]) for a, b in zip(sa, sb)]
    r1 = [_mm(a[0], b[1]) for a, b in zip(sa, sb)]
    r2 = [_mm(a[1], b[0]) for a, b in zip(sa, sb)]
    return [x + (y + z) for x, y, z in zip(r0, r1, r2)]


def _dot_mask_l(m, x):
    mb = m.astype(BF16)
    hi, mid, lo = _split3(x)
    return _mm(mb, hi) + (_mm(mb, mid) + _mm(mb, lo))


def _dot_mask_r(x, m):
    mb = m.astype(BF16)
    hi, mid, lo = _split3(x)
    return _mm(hi, mb) + (_mm(mid, mb) + _mm(lo, mb))


def _iota(shape, dim):
    return lax.broadcasted_iota(jnp.int32, shape, dim)


def _silu(x):
    return x * jax.nn.sigmoid(x)


def _rms(x, eps=EPS):
    return x * lax.rsqrt(jnp.mean(x * x, axis=-1, keepdims=True) + eps)


def _row_to_col(r):
    n = r.shape[1]
    eye = _iota((n, n), 0) == _iota((n, n), 1)
    return jnp.sum(jnp.where(eye, jnp.broadcast_to(r, (n, n)), 0.0), axis=1, keepdims=True)


def _tril_incl(c):
    return (_iota((c, c), 1) <= _iota((c, c), 0)).astype(F32)


PACK_TAIL = 32


def _pack_kernel(a_ref, b_ref, o_ref, *, edges):
    j = pl.program_id(1)
    lo = 0
    for shift, hi in edges:
        def _(shift=shift):
            src = a_ref[...]
            if shift:
                src = jnp.concatenate([src[shift:], b_ref[:shift]], axis=0)
            o_ref[...] = src.T.astype(BF16)
        pl.when((j >= lo) & (j < hi))(_)
        lo = hi


def _pack_w_in(w_t, tk, tn):
    depth, _, d = w_t.shape
    edges, out0 = [], 0
    for a, b in W_SEGS:
        assert out0 % tn == 0 and (b - a) % tn == 0 and 0 <= a - out0 <= PACK_TAIL and (a - out0) % SUBLANES == 0
        edges.append((a - out0, (out0 + b - a) // tn))
        out0 += b - a
    assert out0 == P_MAIN
    return pl.pallas_call(
        functools.partial(_pack_kernel, edges=tuple(edges)),
        grid=(depth, P_MAIN // tn, d // tk),
        in_specs=[
            pl.BlockSpec((None, tn, tk), lambda l, j, k: (l, j, k)),
            pl.BlockSpec((None, PACK_TAIL, tk), lambda l, j, k: (l, (j + 1) * (tn // PACK_TAIL), k)),
        ],
        out_specs=pl.BlockSpec((None, tk, tn), lambda l, j, k: (l, k, j)),
        out_shape=jax.ShapeDtypeStruct((depth, d, P_MAIN), BF16),
        compiler_params=_cparams(("arbitrary", "arbitrary", "arbitrary")),
        name="pack_w_in",
    )(w_t, w_t)


def _pack_small_kernel(a_ref, b_ref, o_ref):
    rows = jnp.concatenate([a_ref[...], b_ref[...],
                            jnp.zeros((LANES - a_ref.shape[0] - b_ref.shape[0], a_ref.shape[1]), F32)], axis=0)
    o_ref[...] = rows.T.astype(BF16)


def _pack_w_small(w_t, tk):
    depth, _, d = w_t.shape
    (a0, a1), (b0, b1) = W_SMALL
    na, nb = a1 - a0, b1 - b0
    assert a0 % na == 0 and b0 % nb == 0 and na % SUBLANES == 0 and nb % SUBLANES == 0
    return pl.pallas_call(
        _pack_small_kernel,
        grid=(depth, d // tk),
        in_specs=[pl.BlockSpec((None, na, tk), lambda l, k: (l, a0 // na, k)),
                  pl.BlockSpec((None, nb, tk), lambda l, k: (l, b0 // nb, k))],
        out_specs=pl.BlockSpec((None, tk, LANES), lambda l, k: (l, k, 0)),
        out_shape=jax.ShapeDtypeStruct((depth, d, LANES), BF16),
        compiler_params=_cparams(("arbitrary", "arbitrary")),
        name="pack_w_small",
    )(w_t, w_t)


def _in_proj_kernel(x_ref, g_ref, w_ref, ws_ref, p_ref, pg_ref, ps_ref, h_ref, *, n_f32):
    j = pl.program_id(1)

    @pl.when(j == 0)
    def _():
        h = (_rms(x_ref[...]) * g_ref[...]).astype(BF16)
        h_ref[...] = h
        ps_ref[...] = _mm(h, ws_ref[...])

    @pl.when(j < n_f32)
    def _():
        p_ref[...] = _mm(h_ref[...], w_ref[...])

    @pl.when(j >= n_f32)
    def _():
        pg_ref[...] = _mm(h_ref[...], w_ref[...]).astype(BF16)


def _in_proj(x, g, w_main, w_small, layer, tm, tn):
    m, d = x.shape
    n = w_main.shape[2]
    n_f32 = P_MG // tn
    assert P_MG % tn == 0 and (n - P_MG) % tn == 0
    return pl.pallas_call(
        functools.partial(_in_proj_kernel, n_f32=n_f32),
        grid=(m // tm, n // tn),
        in_specs=[
            pl.BlockSpec((tm, d), lambda i, j: (i, 0), pipeline_mode=pl.Buffered(1)),
            pl.BlockSpec((None, 1, d), lambda i, j: (layer, 0, 0)),
            pl.BlockSpec((None, d, tn), lambda i, j: (layer, 0, j)),
            pl.BlockSpec((None, d, LANES), lambda i, j: (layer, 0, 0)),
        ],
        out_specs=[
            pl.BlockSpec((tm, tn), lambda i, j: (i, jnp.minimum(j, n_f32 - 1))),
            pl.BlockSpec((tm, tn), lambda i, j: (i, jnp.maximum(j - n_f32, 0))),
            pl.BlockSpec((tm, LANES), lambda i, j: (i, 0)),
        ],
        out_shape=[jax.ShapeDtypeStruct((m, P_MG), F32), jax.ShapeDtypeStruct((m, n - P_MG), BF16),
                   jax.ShapeDtypeStruct((m, LANES), F32)],
        scratch_shapes=[pltpu.VMEM((tm, d), BF16)],
        compiler_params=_cparams(("arbitrary", "arbitrary")),
        name="in_proj",
    )(x, g, w_main, w_small)


def _merge_kernel(y0_ref, y1_ref, y2_ref, w0_ref, w1_ref, w2_ref, g0_ref, g1_ref, g2_ref,
                  b0_ref, b1_ref, b2_ref, o_ref):
    acc = None
    for y_ref, w_ref, g_ref, b_ref in ((y0_ref, w0_ref, g0_ref, b0_ref), (y1_ref, w1_ref, g1_ref, b1_ref),
                                       (y2_ref, w2_ref, g2_ref, b2_ref)):
        t = jax.nn.sigmoid(g_ref[...].astype(F32) + b_ref[...]) * _mm(y_ref[...], w_ref[...])
        acc = t if acc is None else acc + t
    o_ref[...] = acc.astype(BF16)


def _merge(ys, wb, p_gate, b_merge, layer, tm, tn):
    m, bw = ys[0].shape
    d = wb.shape[3]
    nj = d // tn
    mg0 = 0

    def y_spec():
        return pl.BlockSpec((tm, bw), lambda i, j: (i, 0), pipeline_mode=pl.Buffered(1))

    def w_spec(b):
        return pl.BlockSpec((None, None, bw, tn), lambda i, j: (layer, b, 0, j))

    def g_spec(b):
        return pl.BlockSpec((tm, tn), lambda i, j: (i, mg0 + b * nj + j))

    def b_spec(b):
        return pl.BlockSpec((None, None, 1, tn), lambda i, j: (layer, b, 0, j))

    rng = range(N_BRANCH)
    return pl.pallas_call(
        _merge_kernel,
        grid=(m // tm, nj),
        in_specs=[y_spec() for _ in rng] + [w_spec(b) for b in rng] + [g_spec(b) for b in rng]
        + [b_spec(b) for b in rng],
        out_specs=pl.BlockSpec((tm, tn), lambda i, j: (i, j)),
        out_shape=jax.ShapeDtypeStruct((m, d), BF16),
        compiler_params=_cparams(("arbitrary", "arbitrary")),
        name="merge",
    )(*ys, wb, wb, wb, p_gate, p_gate, p_gate, b_merge, b_merge, b_merge)


def _out_proj_kernel(m_ref, w_ref, x_ref, o_ref):
    o_ref[...] = x_ref[...] + _mm(m_ref[...], w_ref[...])


def _out_proj(mrg, w_o, x, layer, tm, tn):
    m, d = x.shape
    return pl.pallas_call(
        _out_proj_kernel,
        grid=(m // tm, d // tn),
        in_specs=[
            pl.BlockSpec((tm, d), lambda i, j: (i, 0)),
            pl.BlockSpec((None, d, tn), lambda i, j: (layer, 0, j)),
            pl.BlockSpec((tm, tn), lambda i, j: (i, j)),
        ],
        out_specs=pl.BlockSpec((tm, tn), lambda i, j: (i, j)),
        out_shape=jax.ShapeDtypeStruct((m, d), F32),
        compiler_params=_cparams(("arbitrary", "arbitrary")),
        name="out_proj",
    )(mrg, w_o, x)


def _ffn_kernel(x_ref, g_ref, wg_ref, wu_ref, wd_ref, o_ref, h_ref):
    j = pl.program_id(1)

    @pl.when(j == 0)
    def _():
        x = x_ref[...]
        h_ref[...] = (_rms(x) * g_ref[...]).astype(BF16)
        o_ref[...] = x

    h = h_ref[...]
    act = (_silu(_mm(h, wg_ref[...])) * _mm(h, wu_ref[...])).astype(BF16)
    o_ref[...] += _mm(act, wd_ref[...])


def _ffn(x, g, w_gu, w_dn, layer, tm, tf):
    m, d = x.shape
    nf = w_dn.shape[1] // tf
    return pl.pallas_call(
        _ffn_kernel,
        grid=(m // tm, nf),
        in_specs=[
            pl.BlockSpec((tm, d), lambda i, j: (i, 0), pipeline_mode=pl.Buffered(1)),
            pl.BlockSpec((None, 1, d), lambda i, j: (layer, 0, 0)),
            pl.BlockSpec((None, d, tf), lambda i, j: (layer, 0, j)),
            pl.BlockSpec((None, d, tf), lambda i, j: (layer, 0, nf + j)),
            pl.BlockSpec((None, tf, d), lambda i, j: (layer, j, 0)),
        ],
        out_specs=pl.BlockSpec((tm, d), lambda i, j: (i, 0)),
        out_shape=jax.ShapeDtypeStruct((m, d), F32),
        scratch_shapes=[pltpu.VMEM((tm, d), BF16)],
        compiler_params=_cparams(("arbitrary", "arbitrary")),
        name="ffn",
    )(x, g, w_gu, w_gu, w_dn)


def _final_norm_kernel(x_ref, g_ref, o_ref):
    o_ref[...] = _rms(x_ref[...]) * g_ref[...]


def _final_norm(x, g, row0, rows, tm):
    d = x.shape[1]
    blk0 = row0 // tm
    return pl.pallas_call(
        _final_norm_kernel,
        grid=(rows // tm,),
        in_specs=[pl.BlockSpec((tm, d), lambda i: (blk0 + i, 0)), pl.BlockSpec((1, d), lambda i: (0, 0))],
        out_specs=pl.BlockSpec((tm, d), lambda i: (i, 0)),
        out_shape=jax.ShapeDtypeStruct((rows, d), F32),
        compiler_params=_cparams(("arbitrary",)),
        name="final_norm",
    )(x, g)


def _slab_pair(x):
    return x, pltpu.roll(x, SAMPLE_SLAB // 2, axis=0)


def _slab_merge(ya, yb):
    top = _iota(ya.shape, 0) < SAMPLE_SLAB // 2
    return jnp.where(top, ya, pltpu.roll(yb, SAMPLE_SLAB // 2, axis=0))


def _sample_io(state, so_prev, y_full, layer, heads, dk, dv, n_fixed):
    blk = pl.BlockSpec((None, SAMPLE_SEQS, heads, dk, dv), lambda i: (layer, i, 0, 0, 0))
    ins, specs = [state, y_full], [blk, pl.BlockSpec(memory_space=pl.ANY)]
    aliases = {n_fixed + 1: 0}
    if so_prev is not None:
        ins.append(so_prev)
        specs.append(pl.BlockSpec(memory_space=pl.ANY))
        aliases[n_fixed + 2] = 1
    out_shapes = [jax.ShapeDtypeStruct(y_full.shape, y_full.dtype), jax.ShapeDtypeStruct(state.shape, F32)]
    return ins, specs, blk, out_shapes, aliases


def _rotary(t, cos, sin_signed):
    even = (_iota(t.shape, 1) & 1) == 0
    nxt = pltpu.roll(t, LANES - 1, axis=1)
    prv = pltpu.roll(t, 1, axis=1)
    return t * cos + jnp.where(even, nxt, prv) * sin_signed


def _ret_decays(c, nvalid, lg):
    i = _iota((c, c), 0)
    j = _iota((c, c), 1)
    diff = i - j
    d_intra = jnp.where(diff >= 0, jnp.exp(lg * jnp.maximum(diff, 0).astype(F32)), 0.0)
    pos_i = _iota((c, 1), 0)
    pos = pos_i.astype(F32)
    d_q = jnp.exp(lg * (pos + 1.0))
    d_k = jnp.where(pos_i < nvalid, jnp.exp(lg * (nvalid - 1.0 - pos)), 0.0)
    d_c = math.exp(lg * nvalid)
    return d_intra, d_q, d_k, d_c


def _ret_lg(h):
    return math.log1p(-(2.0 ** (-5.0 - h)))


def _ret_chunks(qs, ks, vs, gs, ss, cos, sin, decs):
    n = range(len(qs))
    qr = [_rotary(q, cos, sin) for q in qs]
    kr = [_rotary(k, cos, sin) * (RET_DK ** -0.5) for k in ks]
    sc = [_dot_nt(qr[p], kr[p]) * decs[p][0] for p in n]
    o_st = [_dot(qr[p] * decs[p][1], ss[p]) for p in n]
    kv = [_dot_tn(kr[p] * decs[p][2], vs[p]) for p in n]
    o = [_dot(sc[p], vs[p]) + o_st[p] for p in n]
    s_new = [decs[p][3] * ss[p] + kv[p] for p in n]
    ys = [_rms(o[p]) * _silu(gs[p]) for p in n]
    return ys, s_new


def _ret_prompt_kernel(q_ref, k_ref, v_ref, g_ref, cos_ref, sin_ref, y_ref, so_ref, s_ref):
    c = pl.program_id(1)

    @pl.when(c == 0)
    def _():
        s_ref[...] = jnp.zeros_like(s_ref)

    hs = range(RET_HEADS)
    decs = [_ret_decays(RET_CHUNK, RET_CHUNK, _ret_lg(h)) for h in hs]
    ys, s_new = _ret_chunks([q_ref[:, h * RET_DK:(h + 1) * RET_DK] for h in hs],
                            [k_ref[:, h * RET_DK:(h + 1) * RET_DK] for h in hs],
                            [v_ref[:, h * RET_DV:(h + 1) * RET_DV] for h in hs],
                            [g_ref[:, h * RET_DV:(h + 1) * RET_DV] for h in hs],
                            [s_ref[h] for h in hs], cos_ref[...], sin_ref[...], decs)
    for h in hs:
        y_ref[:, h * RET_DV:(h + 1) * RET_DV] = ys[h].astype(BF16)
        s_ref[h] = s_new[h]

    @pl.when(c == pl.num_programs(1) - 1)
    def _():
        so_ref[...] = s_ref[...]


def _ret_prompt(p_main, cos, sin, nb, nc):
    qw = RET_HEADS * RET_DK
    vw = RET_HEADS * RET_DV
    return pl.pallas_call(
        _ret_prompt_kernel,
        grid=(nb, nc),
        in_specs=[
            pl.BlockSpec((RET_CHUNK, qw), lambda b, c: (b * nc + c, P_RQ // qw)),
            pl.BlockSpec((RET_CHUNK, qw), lambda b, c: (b * nc + c, P_RK // qw)),
            pl.BlockSpec((RET_CHUNK, vw), lambda b, c: (b * nc + c, P_RV // vw)),
            pl.BlockSpec((RET_CHUNK, vw), lambda b, c: (b * nc + c, P_RG // vw)),
            pl.BlockSpec((RET_CHUNK, LANES), lambda b, c: (c, 0)),
            pl.BlockSpec((RET_CHUNK, LANES), lambda b, c: (c, 0)),
        ],
        out_specs=[
            pl.BlockSpec((RET_CHUNK, vw), lambda b, c: (b * nc + c, 0)),
            pl.BlockSpec((None, RET_HEADS, RET_DK, RET_DV), lambda b, c: (b, 0, 0, 0)),
        ],
        out_shape=[jax.ShapeDtypeStruct((p_main.shape[0], vw), BF16),
                   jax.ShapeDtypeStruct((nb, RET_HEADS, RET_DK, RET_DV), F32)],
        scratch_shapes=[pltpu.VMEM((RET_HEADS, RET_DK, RET_DV), F32)],
        compiler_params=_cparams(("arbitrary", "arbitrary")),
        name="ret_prompt",
    )(p_main, p_main, p_main, p_main, cos, sin)


def _ret_sample_kernel(q_ref, k_ref, v_ref, g_ref, cos_ref, sin_ref, si_ref, *rest, lseg):
    y_ref, so_ref = rest[-2:]
    npair = q_ref.shape[0] // SAMPLE_SLAB
    cos = cos_ref[...]
    sin = sin_ref[...]
    hs = range(RET_HEADS)
    decs = [_ret_decays(SAMPLE_SLAB, lseg, _ret_lg(h)) for h in hs] * (2 * SAMPLE_PAIRS)
    order = [(u, t, h) for u in range(SAMPLE_PAIRS) for t in range(2) for h in hs]

    def pairs(it, carry):
        ps = [it * SAMPLE_PAIRS + u for u in range(SAMPLE_PAIRS)]
        rows = [pl.ds(pl.multiple_of(p * SAMPLE_SLAB, SAMPLE_SLAB), SAMPLE_SLAB) for p in ps]

        def slabs(ref, w):
            return [[_slab_pair(ref[r, h * w:(h + 1) * w]) for h in hs] for r in rows]

        qs, ks, vs, gs = slabs(q_ref, RET_DK), slabs(k_ref, RET_DK), slabs(v_ref, RET_DV), slabs(g_ref, RET_DV)
        ys, s_new = _ret_chunks([qs[u][h][t] for u, t, h in order], [ks[u][h][t] for u, t, h in order],
                                [vs[u][h][t] for u, t, h in order], [gs[u][h][t] for u, t, h in order],
                                [si_ref[2 * ps[u] + t, h] for u, t, h in order], cos, sin, decs)
        for idx, (u, t, h) in enumerate(order):
            so_ref[2 * ps[u] + t, h] = s_new[idx]
        for u in range(SAMPLE_PAIRS):
            base = u * 2 * RET_HEADS
            for h in hs:
                y_ref[rows[u], h * RET_DV:(h + 1) * RET_DV] = _slab_merge(
                    ys[base + h], ys[base + RET_HEADS + h]).astype(BF16)
        return carry

    lax.fori_loop(0, npair // SAMPLE_PAIRS, pairs, 0)


def _ret_sample(p_main, cos, sin, state, so_prev, y_full, layer, row0, nseq, lseg):
    qw = RET_HEADS * RET_DK
    vw = RET_HEADS * RET_DV
    rb = SAMPLE_SEQS * lseg
    blk0 = row0 // rb
    st_in, st_specs, st_out_spec, out_shapes, aliases = _sample_io(state, so_prev, y_full, layer, RET_HEADS,
                                                                   RET_DK, RET_DV, 6)
    return pl.pallas_call(
        functools.partial(_ret_sample_kernel, lseg=lseg),
        grid=(nseq // SAMPLE_SEQS,),
        in_specs=[
            pl.BlockSpec((rb, qw), lambda i: (blk0 + i, P_RQ // qw)),
            pl.BlockSpec((rb, qw), lambda i: (blk0 + i, P_RK // qw)),
            pl.BlockSpec((rb, vw), lambda i: (blk0 + i, P_RV // vw)),
            pl.BlockSpec((rb, vw), lambda i: (blk0 + i, P_RG // vw)),
            pl.BlockSpec((SAMPLE_SLAB, LANES), lambda i: (0, 0)),
            pl.BlockSpec((SAMPLE_SLAB, LANES), lambda i: (0, 0)),
        ] + st_specs,
        out_specs=[pl.BlockSpec((rb, vw), lambda i: (blk0 + i, 0)), st_out_spec],
        out_shape=out_shapes,
        input_output_aliases=aliases,
        compiler_params=_cparams(("arbitrary",)),
        name="ret_sample",
    )(p_main, p_main, p_main, p_main, cos, sin, *st_in)


def _unit_lower_inverse_all(a_list, nvalid):
    c = a_list[0].shape[0]
    n = range(len(a_list))
    i = _iota((c, c), 0)
    j = _iota((c, c), 1)
    eye = (i == j).astype(F32)
    pair = (i >> 1) == (j >> 1)
    ts = [eye - jnp.where(pair, a, 0.0) for a in a_list]
    blk = 2
    while blk < min(c, nvalid):
        sh = blk.bit_length()
        m = ((i >> sh) == (j >> sh)) & ((i & blk) != 0) & ((j & blk) == 0)
        xs = [jnp.where(m, a, 0.0).astype(BF16) for a in a_list]
        tsp = [_split2(t) for t in ts]
        tx = [_mm(tsp[p][0], xs[p]) + _mm(tsp[p][1], xs[p]) for p in n]
        txs = [_split2(y) for y in tx]
        r0 = [_mm(txs[p][0], tsp[p][0]) for p in n]
        r1 = [_mm(txs[p][0], tsp[p][1]) for p in n]
        r2 = [_mm(txs[p][1], tsp[p][0]) for p in n]
        ts = [ts[p] - (r0[p] + (r1[p] + r2[p])) for p in n]
        blk *= 2
    return ts


def _gdn_gates(small, small_t, alog_r, dtb_r, alog_c, dtb_c, c):
    r = small.shape[0]
    g = -jnp.exp(alog_r) * jax.nn.softplus(small + dtb_r)
    g_t = -jnp.exp(alog_c) * jax.nn.softplus(small_t + dtb_c)
    i = _iota((r, r), 0)
    j = _iota((r, r), 1)
    sh = c.bit_length() - 1
    same = ((i >> sh) == (j >> sh)) if r > c else True
    cum = _dot_mask_l((same & (j <= i)).astype(F32), g)
    cum_t = _dot_mask_r(g_t, (same & (i <= j)).astype(F32))
    beta = jax.nn.sigmoid(small)
    return cum, cum_t, beta


def _gdn_prep(qs, ks, vs, cum_cs, cum_rs, beta_cs, nvalid):
    n = range(len(qs))
    c = qs[0].shape[0]
    i = _iota((c, c), 0)
    j = _iota((c, c), 1)
    rowv = _iota((c, 1), 0) < nvalid
    qn = [q * lax.rsqrt(jnp.sum(q * q, axis=-1, keepdims=True) + EPS) * (GDN_DK ** -0.5) for q in qs]
    kn = [k * lax.rsqrt(jnp.sum(k * k, axis=-1, keepdims=True) + EPS) for k in ks]
    dec = [jnp.exp(jnp.where(j <= i, cum_cs[p] - cum_rs[p], -jnp.inf)) for p in n]
    e_c = [jnp.exp(cum_cs[p]) for p in n]
    kb = [kn[p] * beta_cs[p] for p in n]
    kk = [_dot_nt(kb[p], kn[p]) for p in n]
    qk = [_dot_nt(qn[p], kn[p]) for p in n]
    lower = [jnp.where(j < i, kk[p] * dec[p], 0.0) for p in n]
    ts = _unit_lower_inverse_all(lower, nvalid)
    rhs = [jnp.concatenate([vs[p] * beta_cs[p], kb[p] * e_c[p]], axis=1) for p in n]
    sol = [_dot(ts[p], rhs[p]) for p in n]
    cum_last = [cum_cs[p][nvalid - 1:nvalid, :] for p in n]
    out = []
    for p in n:
        wq = jnp.concatenate([sol[p][:, GDN_DV:], qn[p] * e_c[p]], axis=0).astype(BF16)
        kd = (kn[p] * jnp.where(rowv, jnp.exp(cum_last[p] - cum_cs[p]), 0.0)).astype(BF16)
        out.append((wq, sol[p][:, :GDN_DV], (qk[p] * dec[p]).astype(BF16), kd, jnp.exp(cum_last[p])))
    return out


def _gdn_seq(prep, zs, ss, norm_g):
    n = range(len(prep))
    c = prep[0][1].shape[0]
    st = [_mm(prep[p][0], ss[p].astype(BF16)) for p in n]
    v_new = [prep[p][1] - st[p][:c] for p in n]
    o = [st[p][c:] + _dot(prep[p][2], v_new[p]) for p in n]
    kv = [_dot_tn(prep[p][3], v_new[p]) for p in n]
    s_new = [prep[p][4] * ss[p] + kv[p] for p in n]
    ys = [_rms(o[p]) * norm_g * _silu(zs[p]) for p in n]
    return ys, s_new


def _gdn_conv(xcat, cw):
    acc = xcat[SUBLANES:, :] * cw[GDN_CONV - 1:GDN_CONV, :]
    for w in range(GDN_CONV - 1):
        sh = GDN_CONV - 1 - w
        acc = acc + pltpu.roll(xcat, sh, axis=0)[SUBLANES:, :] * cw[w:w + 1, :]
    return _silu(acc)


def _gdn_split(conv, h):
    hk = GDN_HEADS * GDN_DK
    return (conv[:, h * GDN_DK:(h + 1) * GDN_DK], conv[:, hk + h * GDN_DK: hk + (h + 1) * GDN_DK],
            conv[:, 2 * hk + h * GDN_DV: 2 * hk + (h + 1) * GDN_DV])


def _gdn_prompt_kernel(x_ref, z_ref, sm_ref, smt_ref, cw_ref, alr_ref, dtr_ref, alc_ref, dtc_ref, ng_ref,
                       y_ref, so_ref, s_ref, halo_ref):
    c = pl.program_id(1)

    @pl.when(c == 0)
    def _():
        s_ref[...] = jnp.zeros_like(s_ref)
        halo_ref[...] = jnp.zeros_like(halo_ref)

    x = x_ref[...]
    rows = x.shape[0]
    xcat = jnp.concatenate([halo_ref[...], x], axis=0)
    halo_ref[...] = x[rows - SUBLANES:, :]
    conv = _gdn_conv(xcat, cw_ref[...])
    cum, cum_t, beta = _gdn_gates(sm_ref[...], smt_ref[...], alr_ref[...], dtr_ref[...],
                                  alc_ref[...], dtc_ref[...], CHUNK)
    hs = range(GDN_HEADS)
    order = [(ch, h) for ch in range(rows // CHUNK) for h in hs]

    def rws(a, ch):
        return a[ch * CHUNK:(ch + 1) * CHUNK]

    qkv = [_gdn_split(rws(conv, ch), h) for ch, h in order]
    prep = _gdn_prep([t[0] for t in qkv], [t[1] for t in qkv], [t[2] for t in qkv],
                     [rws(cum, ch)[:, S_DA + h:S_DA + h + 1] for ch, h in order],
                     [cum_t[S_DA + h:S_DA + h + 1, ch * CHUNK:(ch + 1) * CHUNK] for ch, h in order],
                     [rws(beta, ch)[:, S_DB + h:S_DB + h + 1] for ch, h in order], CHUNK)
    ss = [s_ref[h] for h in hs]
    for ch in range(rows // CHUNK):
        ys, ss = _gdn_seq(prep[ch * GDN_HEADS:(ch + 1) * GDN_HEADS],
                          [z_ref[ch * CHUNK:(ch + 1) * CHUNK, h * GDN_DV:(h + 1) * GDN_DV] for h in hs],
                          ss, ng_ref[...])
        for h in hs:
            y_ref[ch * CHUNK:(ch + 1) * CHUNK, h * GDN_DV:(h + 1) * GDN_DV] = ys[h].astype(BF16)
    for h in hs:
        s_ref[h] = ss[h]

    @pl.when(c == pl.num_programs(1) - 1)
    def _():
        so_ref[...] = s_ref[...]


def _gdn_prompt(p_main, p_small, p_small_t, cw, alr, dtr, alc, dtc, ng, layer, nb, nc):
    zw = GDN_HEADS * GDN_DV
    rb = MIX_CHUNKS * CHUNK
    return pl.pallas_call(
        _gdn_prompt_kernel,
        grid=(nb, nc),
        in_specs=[
            pl.BlockSpec((rb, GDN_QKV), lambda b, c: (b * nc + c, P_DQKV // GDN_QKV)),
            pl.BlockSpec((rb, zw), lambda b, c: (b * nc + c, P_DZ // zw)),
            pl.BlockSpec((rb, LANES), lambda b, c: (b * nc + c, 0)),
            pl.BlockSpec((None, LANES, rb), lambda b, c: (b * nc + c, 0, 0)),
            pl.BlockSpec((None, GDN_CONV, GDN_QKV), lambda b, c: (layer, 0, 0)),
            pl.BlockSpec((None, 1, LANES), lambda b, c: (layer, 0, 0)),
            pl.BlockSpec((None, 1, LANES), lambda b, c: (layer, 0, 0)),
            pl.BlockSpec((None, LANES, 1), lambda b, c: (layer, 0, 0)),
            pl.BlockSpec((None, LANES, 1), lambda b, c: (layer, 0, 0)),
            pl.BlockSpec((None, 1, GDN_DV), lambda b, c: (layer, 0, 0)),
        ],
        out_specs=[
            pl.BlockSpec((rb, zw), lambda b, c: (b * nc + c, 0)),
            pl.BlockSpec((None, GDN_HEADS, GDN_DK, GDN_DV), lambda b, c: (b, 0, 0, 0)),
        ],
        out_shape=[jax.ShapeDtypeStruct((p_main.shape[0], zw), BF16),
                   jax.ShapeDtypeStruct((nb, GDN_HEADS, GDN_DK, GDN_DV), F32)],
        scratch_shapes=[pltpu.VMEM((GDN_HEADS, GDN_DK, GDN_DV), F32), pltpu.VMEM((SUBLANES, GDN_QKV), F32)],
        compiler_params=_cparams(("arbitrary", "arbitrary")),
        name="gdn_prompt",
    )(p_main, p_main, p_small, p_small_t, cw, alr, dtr, alc, dtc, ng)


def _gdn_sample_kernel(x_ref, z_ref, sm_ref, smt_ref, sc_ref, cw_ref, alr_ref, dtr_ref, alc_ref, dtc_ref, ng_ref,
                       si_ref, *rest, lseg):
    y_ref, so_ref = rest[-2:]
    npair = x_ref.shape[0] // SAMPLE_SLAB
    cw = cw_ref[...]
    hs = range(GDN_HEADS)

    seqs = [(u, t) for u in range(SAMPLE_PAIRS) for t in range(2)]
    order = [(u, t, h) for u, t in seqs for h in hs]

    def pairs(it, carry):
        ps = [it * SAMPLE_PAIRS + u for u in range(SAMPLE_PAIRS)]
        rows = [pl.ds(pl.multiple_of(p * SAMPLE_SLAB, SAMPLE_SLAB), SAMPLE_SLAB) for p in ps]
        xs = [_slab_pair(x_ref[r, :]) for r in rows]
        sms = [_slab_pair(sm_ref[r, :]) for r in rows]
        zs = [[_slab_pair(z_ref[r, h * GDN_DV:(h + 1) * GDN_DV]) for h in hs] for r in rows]
        convs, gates = {}, {}
        for u, t in seqs:
            xcat = jnp.concatenate([sc_ref[2 * ps[u] + t], xs[u][t]], axis=0)
            convs[u, t] = _gdn_conv(xcat, cw)
            gates[u, t] = _gdn_gates(sms[u][t], smt_ref[2 * ps[u] + t], alr_ref[...], dtr_ref[...],
                                     alc_ref[...], dtc_ref[...], SAMPLE_SLAB)
        qkv = [_gdn_split(convs[u, t], h) for u, t, h in order]
        prep = _gdn_prep([a[0] for a in qkv], [a[1] for a in qkv], [a[2] for a in qkv],
                         [gates[u, t][0][:, S_DA + h:S_DA + h + 1] for u, t, h in order],
                         [gates[u, t][1][S_DA + h:S_DA + h + 1, :] for u, t, h in order],
                         [gates[u, t][2][:, S_DB + h:S_DB + h + 1] for u, t, h in order], lseg)
        ys, s_new = _gdn_seq(prep, [zs[u][h][t] for u, t, h in order],
                             [si_ref[2 * ps[u] + t, h] for u, t, h in order], ng_ref[...])
        for idx, (u, t, h) in enumerate(order):
            so_ref[2 * ps[u] + t, h] = s_new[idx]
        for u in range(SAMPLE_PAIRS):
            base = u * 2 * GDN_HEADS
            for h in hs:
                y_ref[rows[u], h * GDN_DV:(h + 1) * GDN_DV] = _slab_merge(
                    ys[base + h], ys[base + GDN_HEADS + h]).astype(BF16)
        return carry

    lax.fori_loop(0, npair // SAMPLE_PAIRS, pairs, 0)


def _gdn_sample(p_main, p_small, p_small_t, sconv, cw, alr, dtr, alc, dtc, ng, state, so_prev, y_full, layer, row0,
                nseq, lseg):
    zw = GDN_HEADS * GDN_DV
    rb = SAMPLE_SEQS * lseg
    blk0 = row0 // rb
    st_in, st_specs, st_out_spec, out_shapes, aliases = _sample_io(state, so_prev, y_full, layer, GDN_HEADS,
                                                                   GDN_DK, GDN_DV, 11)
    return pl.pallas_call(
        functools.partial(_gdn_sample_kernel, lseg=lseg),
        grid=(nseq // SAMPLE_SEQS,),
        in_specs=[
            pl.BlockSpec((rb, GDN_QKV), lambda i: (blk0 + i, P_DQKV // GDN_QKV)),
            pl.BlockSpec((rb, zw), lambda i: (blk0 + i, P_DZ // zw)),
            pl.BlockSpec((rb, LANES), lambda i: (blk0 + i, 0)),
            pl.BlockSpec((SAMPLE_SEQS, LANES, SAMPLE_SLAB), lambda i: (i, 0, 0)),
            pl.BlockSpec((SAMPLE_SEQS, SUBLANES, GDN_QKV), lambda i: (i, 0, 0)),
            pl.BlockSpec((None, GDN_CONV, GDN_QKV), lambda i: (layer, 0, 0)),
            pl.BlockSpec((None, 1, LANES), lambda i: (layer, 0, 0)),
            pl.BlockSpec((None, 1, LANES), lambda i: (layer, 0, 0)),
            pl.BlockSpec((None, LANES, 1), lambda i: (layer, 0, 0)),
            pl.BlockSpec((None, LANES, 1), lambda i: (layer, 0, 0)),
            pl.BlockSpec((None, 1, GDN_DV), lambda i: (layer, 0, 0)),
        ] + st_specs,
        out_specs=[pl.BlockSpec((rb, zw), lambda i: (blk0 + i, 0)), st_out_spec],
        out_shape=out_shapes,
        input_output_aliases=aliases,
        compiler_params=_cparams(("arbitrary",)),
        name="gdn_sample",
    )(p_main, p_main, p_small, p_small_t, sconv, cw, alr, dtr, alc, dtc, ng, *st_in)


def _gla_scores(q, k, b, sub):
    c = q.shape[0]
    irow = _iota((sub, 1), 0)
    jrow_all = _iota((c, 1), 0)
    lane = _iota((sub, c), 1)
    blocks = []
    for blk in range(c // sub):
        r0 = blk * sub
        qi, ki, bi = q[r0:r0 + sub], k[r0:r0 + sub], b[r0:r0 + sub]
        if blk > 0:
            b0 = b[r0 - 1:r0, :]
            qt = qi * jnp.exp(bi - b0)
            kt = k * jnp.exp(jnp.where(jrow_all < r0, b0 - b, -jnp.inf))
            a = _dot_nt(qt, kt)
        else:
            a = jnp.zeros((sub, c), F32)
        for jj in range(sub):
            e = jnp.exp(jnp.where(irow >= jj, bi - bi[jj:jj + 1, :], -jnp.inf))
            col = jnp.sum(qi * e * ki[jj:jj + 1, :], axis=1, keepdims=True)
            a = jnp.where(lane == r0 + jj, col, a)
        blocks.append(a)
    return jnp.concatenate(blocks, axis=0) if len(blocks) > 1 else blocks[0]


def _gla_prep(qs, ks, vs, bs, nvalid, sub):
    n = range(len(qs))
    c = qs[0].shape[0]
    rowv = _iota((c, 1), 0) < nvalid
    qsc = [q * (GLA_DK ** -0.5) for q in qs]
    b_last = [bs[p][nvalid - 1:nvalid, :] for p in n]
    kv = [_dot_tn(ks[p] * jnp.where(rowv, jnp.exp(b_last[p] - bs[p]), 0.0), vs[p]) for p in n]
    a = [_gla_scores(qsc[p], ks[p], bs[p], sub) for p in n]
    o_in = [_dot(a[p], vs[p]) for p in n]
    return [((qsc[p] * jnp.exp(bs[p])).astype(BF16), o_in[p], kv[p], _row_to_col(jnp.exp(b_last[p]))) for p in n]


def _gla_seq(prep, zs, ss, norm_g):
    n = range(len(prep))
    o = [prep[p][1] + _mm(prep[p][0], ss[p].astype(BF16)) for p in n]
    s_new = [prep[p][3] * ss[p] + prep[p][2] for p in n]
    ys = [_rms(o[p]) * norm_g * _silu(zs[p]) for p in n]
    return ys, s_new


def _gla_cumsum(gk, c):
    r = gk.shape[0]
    i = _iota((r, r), 0)
    j = _iota((r, r), 1)
    sh = c.bit_length() - 1
    same = ((i >> sh) == (j >> sh)) if r > c else True
    return _dot_mask_l((same & (j <= i)).astype(F32), gk)


def _gla_gk(small, wup, bup):
    return jax.nn.log_sigmoid(_mm(small.astype(BF16), wup) + bup) / GLA_GATE_NORM


def _gla_prompt_kernel(q_ref, k_ref, v_ref, z_ref, sm_ref, wup_ref, bup_ref, ng_ref, y_ref, so_ref, s_ref):
    c = pl.program_id(1)

    @pl.when(c == 0)
    def _():
        s_ref[...] = jnp.zeros_like(s_ref)

    rows = q_ref.shape[0]
    b_all = _gla_cumsum(_gla_gk(sm_ref[...], wup_ref[...], bup_ref[...]), CHUNK)
    hs = range(GLA_HEADS)
    order = [(ch, h) for ch in range(rows // CHUNK) for h in hs]

    def blk(ref, ch, h, w):
        return ref[ch * CHUNK:(ch + 1) * CHUNK, h * w:(h + 1) * w]

    prep = _gla_prep([blk(q_ref, ch, h, GLA_DK) for ch, h in order], [blk(k_ref, ch, h, GLA_DK) for ch, h in order],
                     [blk(v_ref, ch, h, GLA_DV) for ch, h in order], [blk(b_all, ch, h, GLA_DK) for ch, h in order],
                     CHUNK, GLA_SUB)
    ss = [s_ref[h] for h in hs]
    for ch in range(rows // CHUNK):
        ys, ss = _gla_seq(prep[ch * GLA_HEADS:(ch + 1) * GLA_HEADS], [blk(z_ref, ch, h, GLA_DV) for h in hs],
                          ss, ng_ref[...])
        for h in hs:
            y_ref[ch * CHUNK:(ch + 1) * CHUNK, h * GLA_DV:(h + 1) * GLA_DV] = ys[h].astype(BF16)
    for h in hs:
        s_ref[h] = ss[h]

    @pl.when(c == pl.num_programs(1) - 1)
    def _():
        so_ref[...] = s_ref[...]


def _gla_prompt(p_main, p_small, wup, bup, ng, layer, nb, nc):
    qw = GLA_HEADS * GLA_DK
    vw = GLA_HEADS * GLA_DV
    rb = MIX_CHUNKS * CHUNK
    return pl.pallas_call(
        _gla_prompt_kernel,
        grid=(nb, nc),
        in_specs=[
            pl.BlockSpec((rb, qw), lambda b, c: (b * nc + c, P_LQ // qw)),
            pl.BlockSpec((rb, qw), lambda b, c: (b * nc + c, P_LK // qw)),
            pl.BlockSpec((rb, vw), lambda b, c: (b * nc + c, P_LV // vw)),
            pl.BlockSpec((rb, vw), lambda b, c: (b * nc + c, P_LGT // vw)),
            pl.BlockSpec((rb, LANES), lambda b, c: (b * nc + c, 0)),
            pl.BlockSpec((None, LANES, qw), lambda b, c: (layer, 0, 0)),
            pl.BlockSpec((None, 1, qw), lambda b, c: (layer, 0, 0)),
            pl.BlockSpec((None, 1, GLA_DV), lambda b, c: (layer, 0, 0)),
        ],
        out_specs=[
            pl.BlockSpec((rb, vw), lambda b, c: (b * nc + c, 0)),
            pl.BlockSpec((None, GLA_HEADS, GLA_DK, GLA_DV), lambda b, c: (b, 0, 0, 0)),
        ],
        out_shape=[jax.ShapeDtypeStruct((p_main.shape[0], vw), BF16),
                   jax.ShapeDtypeStruct((nb, GLA_HEADS, GLA_DK, GLA_DV), F32)],
        scratch_shapes=[pltpu.VMEM((GLA_HEADS, GLA_DK, GLA_DV), F32)],
        compiler_params=_cparams(("arbitrary", "arbitrary")),
        name="gla_prompt",
    )(p_main, p_main, p_main, p_main, p_small, wup, bup, ng)


def _gla_sample_kernel(q_ref, k_ref, v_ref, z_ref, sm_ref, wup_ref, bup_ref, ng_ref, si_ref, *rest, lseg):
    y_ref, so_ref = rest[-2:]
    npair = q_ref.shape[0] // SAMPLE_SLAB
    hs = range(GLA_HEADS)

    order = [(u, t, h) for u in range(SAMPLE_PAIRS) for t in range(2) for h in hs]

    def pairs(it, carry):
        ps = [it * SAMPLE_PAIRS + u for u in range(SAMPLE_PAIRS)]
        rows = [pl.ds(pl.multiple_of(p * SAMPLE_SLAB, SAMPLE_SLAB), SAMPLE_SLAB) for p in ps]

        def slabs(ref, w):
            return [[_slab_pair(ref[r, h * w:(h + 1) * w]) for h in hs] for r in rows]

        qs, ks, vs, zs = slabs(q_ref, GLA_DK), slabs(k_ref, GLA_DK), slabs(v_ref, GLA_DV), slabs(z_ref, GLA_DV)
        bs = [[_gla_cumsum(g, SAMPLE_SLAB) for g in _slab_pair(_gla_gk(sm_ref[r, :], wup_ref[...], bup_ref[...]))]
              for r in rows]
        prep = _gla_prep([qs[u][h][t] for u, t, h in order], [ks[u][h][t] for u, t, h in order],
                         [vs[u][h][t] for u, t, h in order],
                         [bs[u][t][:, h * GLA_DK:(h + 1) * GLA_DK] for u, t, h in order], lseg, SAMPLE_SLAB)
        ys, s_new = _gla_seq(prep, [zs[u][h][t] for u, t, h in order],
                             [si_ref[2 * ps[u] + t, h] for u, t, h in order], ng_ref[...])
        for idx, (u, t, h) in enumerate(order):
            so_ref[2 * ps[u] + t, h] = s_new[idx]
        for u in range(SAMPLE_PAIRS):
            base = u * 2 * GLA_HEADS
            for h in hs:
                y_ref[rows[u], h * GLA_DV:(h + 1) * GLA_DV] = _slab_merge(
                    ys[base + h], ys[base + GLA_HEADS + h]).astype(BF16)
        return carry

    lax.fori_loop(0, npair // SAMPLE_PAIRS, pairs, 0)


def _gla_sample(p_main, p_small, wup, bup, ng, state, so_prev, y_full, layer, row0, nseq, lseg):
    qw = GLA_HEADS * GLA_DK
    vw = GLA_HEADS * GLA_DV
    rb = SAMPLE_SEQS * lseg
    blk0 = row0 // rb
    st_in, st_specs, st_out_spec, out_shapes, aliases = _sample_io(state, so_prev, y_full, layer, GLA_HEADS,
                                                                   GLA_DK, GLA_DV, 8)
    return pl.pallas_call(
        functools.partial(_gla_sample_kernel, lseg=lseg),
        grid=(nseq // SAMPLE_SEQS,),
        in_specs=[
            pl.BlockSpec((rb, qw), lambda i: (blk0 + i, P_LQ // qw)),
            pl.BlockSpec((rb, qw), lambda i: (blk0 + i, P_LK // qw)),
            pl.BlockSpec((rb, vw), lambda i: (blk0 + i, P_LV // vw)),
            pl.BlockSpec((rb, vw), lambda i: (blk0 + i, P_LGT // vw)),
            pl.BlockSpec((rb, LANES), lambda i: (blk0 + i, 0)),
            pl.BlockSpec((None, LANES, qw), lambda i: (layer, 0, 0)),
            pl.BlockSpec((None, 1, qw), lambda i: (layer, 0, 0)),
            pl.BlockSpec((None, 1, GLA_DV), lambda i: (layer, 0, 0)),
        ] + st_specs,
        out_specs=[pl.BlockSpec((rb, vw), lambda i: (blk0 + i, 0)), st_out_spec],
        out_shape=out_shapes,
        input_output_aliases=aliases,
        compiler_params=_cparams(("arbitrary",)),
        name="gla_sample",
    )(p_main, p_main, p_main, p_main, p_small, wup, bup, ng, *st_in)


def _rope_tables(pos0, length):
    inv = 1.0 / (ROPE_BASE ** jnp.linspace(0.0, 1.0, RET_DK // 2, dtype=F32))
    ang = (jnp.arange(length, dtype=F32) + pos0)[:, None] * inv[None, :]
    cos = jnp.repeat(jnp.cos(ang), 2, axis=1)
    sin = jnp.stack([-jnp.sin(ang), jnp.sin(ang)], axis=-1).reshape(length, RET_DK)
    return cos, sin


def _lane_pad(v, off):
    n = v.shape[-1]
    return jnp.pad(v, ((0, 0), (off, LANES - off - n)))[:, None, :]


def kernel(x_prompt, x_sample, state_ret, state_gdn, state_gdn_conv, state_gla, norm_mix, norm_ffn, norm_final,
           w_in, b_merge, gdn_conv_w, gdn_a_log, gdn_dt_bias, gdn_norm, gla_w_up, gla_b_up, gla_norm, w_branch,
           w_o, w_gate_up, w_down):
    nb, seq, d = x_prompt.shape
    nsq, lseg, _ = x_sample.shape
    depth = w_in.shape[0]
    n_p = nb * seq
    n_s = nsq * lseg
    mix_rows = MIX_CHUNKS * CHUNK
    nc = seq // mix_rows
    nc_ret = seq // RET_CHUNK
    tm = _dense_tiles(n_p + n_s)
    tm_out = math.gcd(n_p, n_s)
    assert seq % mix_rows == 0 and seq % RET_CHUNK == 0
    assert nsq % SAMPLE_SEQS == 0 and n_p % (SAMPLE_SEQS * lseg) == 0
    assert 2 * lseg == SAMPLE_SLAB and lseg >= GDN_CONV - 1

    w_t = jnp.swapaxes(w_in, 1, 2)
    w_main = _pack_w_in(w_t, 1024, 1024)
    w_small = _pack_w_small(w_t, 512)
    wb = w_branch.astype(BF16)
    wo = w_o.astype(BF16)
    wgu = w_gate_up.astype(BF16)
    wdn = w_down.astype(BF16)
    g_mix = norm_mix[:, None, :]
    g_ffn = norm_ffn[:, None, :]
    bm = b_merge.reshape(depth, N_BRANCH, 1, D_MODEL)
    alr = _lane_pad(gdn_a_log, S_DA)
    dtr = _lane_pad(gdn_dt_bias, S_DA)
    alc = jnp.swapaxes(alr, 1, 2)
    dtc = jnp.swapaxes(dtr, 1, 2)
    gdn_ng = gdn_norm[:, None, :]
    gla_ng = gla_norm[:, None, :]
    wup = jnp.pad(gla_w_up, ((0, 0), (S_LLR, LANES - S_LLR - GLA_LOWRANK), (0, 0))).astype(BF16)
    bup = gla_b_up[:, None, :]
    cos_p, sin_p = _rope_tables(0.0, seq)
    cos_s, sin_s = _rope_tables(float(PAST_LEN), lseg)
    cos_s = jnp.tile(cos_s, (SAMPLE_SLAB // lseg, 1))
    sin_s = jnp.tile(sin_s, (SAMPLE_SLAB // lseg, 1))

    x = jnp.concatenate([x_prompt.reshape(n_p, d), x_sample.reshape(n_s, d)], axis=0)

    outs = {k: [] for k in ("p_ret", "p_gdn", "p_conv", "p_gla", "s_conv")}
    s_ret = s_gdn = s_gla = None
    for layer in range(depth):
        p_main, p_gate, p_small = _in_proj(x, g_mix, w_main, w_small, layer, tm, 1024)
        pst_p = jnp.swapaxes(p_small[:n_p].reshape(nb * nc, mix_rows, LANES), 1, 2)
        pst_s = jnp.swapaxes(jnp.pad(p_small[n_p:].reshape(nsq, lseg, LANES),
                                     ((0, 0), (0, SAMPLE_SLAB - lseg), (0, 0))), 1, 2)

        y_ret, st = _ret_prompt(p_main, cos_p, sin_p, nb, nc_ret)
        outs["p_ret"].append(st)
        y_ret, s_ret = _ret_sample(p_main, cos_s, sin_s, state_ret, s_ret, y_ret, layer, n_p, nsq, lseg)

        y_gdn, st = _gdn_prompt(p_main, p_small, pst_p, gdn_conv_w, alr, dtr, alc, dtc, gdn_ng, layer, nb, nc)
        outs["p_gdn"].append(st)
        sconv = jnp.pad(state_gdn_conv[layer], ((0, 0), (SUBLANES - (GDN_CONV - 1), 0), (0, 0)))
        y_gdn, s_gdn = _gdn_sample(p_main, p_small, pst_s, sconv, gdn_conv_w, alr, dtr, alc, dtc, gdn_ng,
                                   state_gdn, s_gdn, y_gdn, layer, n_p, nsq, lseg)
        outs["p_conv"].append(jnp.stack([
            lax.slice(p_main, ((b + 1) * seq - (GDN_CONV - 1), P_DQKV), ((b + 1) * seq, P_DQKV + GDN_QKV))
            for b in range(nb)]))
        dq_s = lax.slice(p_main, (n_p, P_DQKV), (n_p + n_s, P_DQKV + GDN_QKV)).reshape(nsq, lseg, GDN_QKV)
        outs["s_conv"].append(jnp.concatenate([state_gdn_conv[layer], dq_s], axis=1)[:, -(GDN_CONV - 1):])

        y_gla, st = _gla_prompt(p_main, p_small, wup, bup, gla_ng, layer, nb, nc)
        outs["p_gla"].append(st)
        y_gla, s_gla = _gla_sample(p_main, p_small, wup, bup, gla_ng, state_gla, s_gla, y_gla, layer, n_p, nsq, lseg)

        mrg = _merge((y_ret, y_gdn, y_gla), wb, p_gate, bm, layer, tm, 512)
        x = _out_proj(mrg, wo, x, layer, tm, 1024)
        x = _ffn(x, g_ffn, wgu, wdn, layer, tm, 512)

    g_fin = norm_final[None, :]
    y_p = _final_norm(x, g_fin, 0, n_p, tm_out)
    y_s = _final_norm(x, g_fin, n_p, n_s, tm_out)
    st = {k: jnp.stack(v) for k, v in outs.items()}
    return (y_p.reshape(nb, seq, d), y_s.reshape(nsq, lseg, d),
            st["p_ret"], st["p_gdn"], st["p_conv"], st["p_gla"],
            s_ret, s_gdn, st["s_conv"], s_gla)
```

```python
import functools
import math

import jax
import jax.numpy as jnp
from jax import lax
from jax.experimental import pallas as pl
from jax.experimental.pallas import tpu as pltpu

F32 = jnp.float32
BF16 = jnp.bfloat16

D_MODEL = 2048
RET_HEADS, RET_DK, RET_DV = 4, 128, 256
GDN_HEADS, GDN_DK, GDN_DV, GDN_CONV = 8, 128, 128, 4
GDN_QKV = GDN_HEADS * (2 * GDN_DK + GDN_DV)
GLA_HEADS, GLA_DK, GLA_DV, GLA_LOWRANK = 4, 128, 256, 16
GLA_GATE_NORM = 16.0
N_BRANCH = 3
BRANCH_WIDTH = 1024
EPS = 1e-6
ROPE_BASE = 10000.0
PAST_LEN = 16384

LANES = 128
SUBLANES = 8
CHUNK = 64
MIX_CHUNKS = 4
RET_CHUNK = 256
GLA_SUB = 8
SAMPLE_SLAB = 8
SAMPLE_SEQS = 8
SAMPLE_PAIRS = 2
VMEM_LIMIT = 56 * 1024 * 1024

W_SEGS = ((0, 6144), (6160, 9232), (9248, 16416))
W_SMALL = ((6144, 6160), (9232, 9248))
P_RQ, P_RK, P_RV, P_RG = 0, 512, 1024, 2048
P_DQKV, P_DZ = 3072, 6144
P_LQ, P_LK, P_LV, P_LGT, P_MG = 7168, 7680, 8192, 9216, 10240
P_MAIN = 16384
S_DA, S_DB, S_LLR = 0, 8, 16


def _cparams(sem):
    return pltpu.CompilerParams(dimension_semantics=sem, vmem_limit_bytes=VMEM_LIMIT)


def _dense_tiles(m):
    best = 64
    for t in range(64, 1089, 64):
        if m % t == 0:
            best = t
    return best


_mm = functools.partial(jnp.dot, preferred_element_type=F32)


def _dot(a, b):
    return _mm(a.astype(BF16), b.astype(BF16))


def _dot_nt(a, b):
    return lax.dot_general(a.astype(BF16), b.astype(BF16), (((1,), (1,)), ((), ())),
                           preferred_element_type=F32)


def _dot_tn(a, b):
    return lax.dot_general(a.astype(BF16), b.astype(BF16), (((0,), (0,)), ((), ())),
                           preferred_element_type=F32)


def _split3(x):
    hi = x.astype(BF16)
    r = x - hi.astype(F32)
    mid = r.astype(BF16)
    lo = (r - mid.astype(F32)).astype(BF16)
    return hi, mid, lo


def _dot_mask_l(m, x):
    mb = m.astype(BF16)
    hi, mid, lo = _split3(x)
    return _mm(mb, hi) + (_mm(mb, mid) + _mm(mb, lo))


def _dot_mask_r(x, m):
    mb = m.astype(BF16)
    hi, mid, lo = _split3(x)
    return _mm(hi, mb) + (_mm(mid, mb) + _mm(lo, mb))


def _iota(shape, dim):
    return lax.broadcasted_iota(jnp.int32, shape, dim)


def _silu(x):
    return x * jax.nn.sigmoid(x)


def _rms(x, eps=EPS):
    return x * lax.rsqrt(jnp.mean(x * x, axis=-1, keepdims=True) + eps)


def _row_to_col(r):
    n = r.shape[1]
    eye = _iota((n, n), 0) == _iota((n, n), 1)
    return jnp.sum(jnp.where(eye, jnp.broadcast_to(r, (n, n)), 0.0), axis=1, keepdims=True)


def _tril_incl(c):
    return (_iota((c, c), 1) <= _iota((c, c), 0)).astype(F32)


PACK_TAIL = 32


def _pack_kernel(a_ref, b_ref, o_ref, *, edges):
    j = pl.program_id(1)
    lo = 0
    for shift, hi in edges:
        def _(shift=shift):
            src = a_ref[...]
            if shift:
                src = jnp.concatenate([src[shift:], b_ref[:shift]], axis=0)
            o_ref[...] = src.T.astype(BF16)
        pl.when((j >= lo) & (j < hi))(_)
        lo = hi


def _pack_w_in(w_t, tk, tn):
    depth, _, d = w_t.shape
    edges, out0 = [], 0
    for a, b in W_SEGS:
        assert out0 % tn == 0 and (b - a) % tn == 0 and 0 <= a - out0 <= PACK_TAIL and (a - out0) % SUBLANES == 0
        edges.append((a - out0, (out0 + b - a) // tn))
        out0 += b - a
    assert out0 == P_MAIN
    return pl.pallas_call(
        functools.partial(_pack_kernel, edges=tuple(edges)),
        grid=(depth, P_MAIN // tn, d // tk),
        in_specs=[
            pl.BlockSpec((None, tn, tk), lambda l, j, k: (l, j, k)),
            pl.BlockSpec((None, PACK_TAIL, tk), lambda l, j, k: (l, (j + 1) * (tn // PACK_TAIL), k)),
        ],
        out_specs=pl.BlockSpec((None, tk, tn), lambda l, j, k: (l, k, j)),
        out_shape=jax.ShapeDtypeStruct((depth, d, P_MAIN), BF16),
        compiler_params=_cparams(("arbitrary", "arbitrary", "arbitrary")),
        name="pack_w_in",
    )(w_t, w_t)


def _pack_small_kernel(a_ref, b_ref, o_ref):
    rows = jnp.concatenate([a_ref[...], b_ref[...],
                            jnp.zeros((LANES - a_ref.shape[0] - b_ref.shape[0], a_ref.shape[1]), F32)], axis=0)
    o_ref[...] = rows.T.astype(BF16)


def _pack_w_small(w_t, tk):
    depth, _, d = w_t.shape
    (a0, a1), (b0, b1) = W_SMALL
    na, nb = a1 - a0, b1 - b0
    assert a0 % na == 0 and b0 % nb == 0 and na % SUBLANES == 0 and nb % SUBLANES == 0
    return pl.pallas_call(
        _pack_small_kernel,
        grid=(depth, d // tk),
        in_specs=[pl.BlockSpec((None, na, tk), lambda l, k: (l, a0 // na, k)),
                  pl.BlockSpec((None, nb, tk), lambda l, k: (l, b0 // nb, k))],
        out_specs=pl.BlockSpec((None, tk, LANES), lambda l, k: (l, k, 0)),
        out_shape=jax.ShapeDtypeStruct((depth, d, LANES), BF16),
        compiler_params=_cparams(("arbitrary", "arbitrary")),
        name="pack_w_small",
    )(w_t, w_t)


def _in_proj_kernel(x_ref, g_ref, w_ref, ws_ref, p_ref, pg_ref, ps_ref, h_ref, *, n_f32):
    j = pl.program_id(1)

    @pl.when(j == 0)
    def _():
        h = (_rms(x_ref[...]) * g_ref[...]).astype(BF16)
        h_ref[...] = h
        ps_ref[...] = _mm(h, ws_ref[...])

    @pl.when(j < n_f32)
    def _():
        p_ref[...] = _mm(h_ref[...], w_ref[...])

    @pl.when(j >= n_f32)
    def _():
        pg_ref[...] = _mm(h_ref[...], w_ref[...]).astype(BF16)


def _in_proj(x, g, w_main, w_small, layer, tm, tn):
    m, d = x.shape
    n = w_main.shape[2]
    n_f32 = P_MG // tn
    assert P_MG % tn == 0 and (n - P_MG) % tn == 0
    return pl.pallas_call(
        functools.partial(_in_proj_kernel, n_f32=n_f32),
        grid=(m // tm, n // tn),
        in_specs=[
            pl.BlockSpec((tm, d), lambda i, j: (i, 0)),
            pl.BlockSpec((None, 1, d), lambda i, j: (layer, 0, 0)),
            pl.BlockSpec((None, d, tn), lambda i, j: (layer, 0, j)),
            pl.BlockSpec((None, d, LANES), lambda i, j: (layer, 0, 0)),
        ],
        out_specs=[
            pl.BlockSpec((tm, tn), lambda i, j: (i, jnp.minimum(j, n_f32 - 1))),
            pl.BlockSpec((tm, tn), lambda i, j: (i, jnp.maximum(j - n_f32, 0))),
            pl.BlockSpec((tm, LANES), lambda i, j: (i, 0)),
        ],
        out_shape=[jax.ShapeDtypeStruct((m, P_MG), F32), jax.ShapeDtypeStruct((m, n - P_MG), BF16),
                   jax.ShapeDtypeStruct((m, LANES), F32)],
        scratch_shapes=[pltpu.VMEM((tm, d), BF16)],
        compiler_params=_cparams(("arbitrary", "arbitrary")),
        name="in_proj",
    )(x, g, w_main, w_small)


def _merge_kernel(y0_ref, y1_ref, y2_ref, w0_ref, w1_ref, w2_ref, g0_ref, g1_ref, g2_ref,
                  b0_ref, b1_ref, b2_ref, o_ref):
    acc = None
    for y_ref, w_ref, g_ref, b_ref in ((y0_ref, w0_ref, g0_ref, b0_ref), (y1_ref, w1_ref, g1_ref, b1_ref),
                                       (y2_ref, w2_ref, g2_ref, b2_ref)):
        t = jax.nn.sigmoid(g_ref[...].astype(F32) + b_ref[...]) * _mm(y_ref[...], w_ref[...])
        acc = t if acc is None else acc + t
    o_ref[...] = acc.astype(BF16)


def _merge(ys, wb, p_gate, b_merge, layer, tm, tn):
    m, bw = ys[0].shape
    d = wb.shape[3]
    nj = d // tn
    mg0 = 0

    def y_spec():
        return pl.BlockSpec((tm, bw), lambda i, j: (i, 0))

    def w_spec(b):
        return pl.BlockSpec((None, None, bw, tn), lambda i, j: (layer, b, 0, j))

    def g_spec(b):
        return pl.BlockSpec((tm, tn), lambda i, j: (i, mg0 + b * nj + j))

    def b_spec(b):
        return pl.BlockSpec((None, None, 1, tn), lambda i, j: (layer, b, 0, j))

    rng = range(N_BRANCH)
    return pl.pallas_call(
        _merge_kernel,
        grid=(m // tm, nj),
        in_specs=[y_spec() for _ in rng] + [w_spec(b) for b in rng] + [g_spec(b) for b in rng]
        + [b_spec(b) for b in rng],
        out_specs=pl.BlockSpec((tm, tn), lambda i, j: (i, j)),
        out_shape=jax.ShapeDtypeStruct((m, d), BF16),
        compiler_params=_cparams(("arbitrary", "arbitrary")),
        name="merge",
    )(*ys, wb, wb, wb, p_gate, p_gate, p_gate, b_merge, b_merge, b_merge)


def _out_proj_kernel(m_ref, w_ref, x_ref, o_ref):
    o_ref[...] = x_ref[...] + _mm(m_ref[...], w_ref[...])


def _out_proj(mrg, w_o, x, layer, tm, tn):
    m, d = x.shape
    return pl.pallas_call(
        _out_proj_kernel,
        grid=(m // tm, d // tn),
        in_specs=[
            pl.BlockSpec((tm, d), lambda i, j: (i, 0)),
            pl.BlockSpec((None, d, tn), lambda i, j: (layer, 0, j)),
            pl.BlockSpec((tm, tn), lambda i, j: (i, j)),
        ],
        out_specs=pl.BlockSpec((tm, tn), lambda i, j: (i, j)),
        out_shape=jax.ShapeDtypeStruct((m, d), F32),
        compiler_params=_cparams(("arbitrary", "arbitrary")),
        name="out_proj",
    )(mrg, w_o, x)


def _ffn_kernel(x_ref, g_ref, wg_ref, wu_ref, wd_ref, o_ref, h_ref):
    j = pl.program_id(1)

    @pl.when(j == 0)
    def _():
        x = x_ref[...]
        h_ref[...] = (_rms(x) * g_ref[...]).astype(BF16)
        o_ref[...] = x

    h = h_ref[...]
    act = (_silu(_mm(h, wg_ref[...])) * _mm(h, wu_ref[...])).astype(BF16)
    o_ref[...] += _mm(act, wd_ref[...])


def _ffn(x, g, w_gu, w_dn, layer, tm, tf):
    m, d = x.shape
    nf = w_dn.shape[1] // tf
    return pl.pallas_call(
        _ffn_kernel,
        grid=(m // tm, nf),
        in_specs=[
            pl.BlockSpec((tm, d), lambda i, j: (i, 0), pipeline_mode=pl.Buffered(1)),
            pl.BlockSpec((None, 1, d), lambda i, j: (layer, 0, 0)),
            pl.BlockSpec((None, d, tf), lambda i, j: (layer, 0, j)),
            pl.BlockSpec((None, d, tf), lambda i, j: (layer, 0, nf + j)),
            pl.BlockSpec((None, tf, d), lambda i, j: (layer, j, 0)),
        ],
        out_specs=pl.BlockSpec((tm, d), lambda i, j: (i, 0)),
        out_shape=jax.ShapeDtypeStruct((m, d), F32),
        scratch_shapes=[pltpu.VMEM((tm, d), BF16)],
        compiler_params=_cparams(("arbitrary", "arbitrary")),
        name="ffn",
    )(x, g, w_gu, w_gu, w_dn)


def _final_norm_kernel(x_ref, g_ref, o_ref):
    o_ref[...] = _rms(x_ref[...]) * g_ref[...]


def _final_norm(x, g, row0, rows, tm):
    d = x.shape[1]
    blk0 = row0 // tm
    return pl.pallas_call(
        _final_norm_kernel,
        grid=(rows // tm,),
        in_specs=[pl.BlockSpec((tm, d), lambda i: (blk0 + i, 0)), pl.BlockSpec((1, d), lambda i: (0, 0))],
        out_specs=pl.BlockSpec((tm, d), lambda i: (i, 0)),
        out_shape=jax.ShapeDtypeStruct((rows, d), F32),
        compiler_params=_cparams(("arbitrary",)),
        name="final_norm",
    )(x, g)


def _slab_pair(x):
    return x, pltpu.roll(x, SAMPLE_SLAB // 2, axis=0)


def _slab_merge(ya, yb):
    top = _iota(ya.shape, 0) < SAMPLE_SLAB // 2
    return jnp.where(top, ya, pltpu.roll(yb, SAMPLE_SLAB // 2, axis=0))


def _sample_io(state, so_prev, y_full, layer, heads, dk, dv, n_fixed):
    blk = pl.BlockSpec((None, SAMPLE_SEQS, heads, dk, dv), lambda i: (layer, i, 0, 0, 0))
    ins, specs = [state, y_full], [blk, pl.BlockSpec(memory_space=pl.ANY)]
    aliases = {n_fixed + 1: 0}
    if so_prev is not None:
        ins.append(so_prev)
        specs.append(pl.BlockSpec(memory_space=pl.ANY))
        aliases[n_fixed + 2] = 1
    out_shapes = [jax.ShapeDtypeStruct(y_full.shape, y_full.dtype), jax.ShapeDtypeStruct(state.shape, F32)]
    return ins, specs, blk, out_shapes, aliases


def _rotary(t, cos, sin_signed):
    even = (_iota(t.shape, 1) & 1) == 0
    nxt = pltpu.roll(t, LANES - 1, axis=1)
    prv = pltpu.roll(t, 1, axis=1)
    return t * cos + jnp.where(even, nxt, prv) * sin_signed


def _ret_decays(c, nvalid, lg):
    i = _iota((c, c), 0)
    j = _iota((c, c), 1)
    diff = i - j
    d_intra = jnp.where(diff >= 0, jnp.exp(lg * jnp.maximum(diff, 0).astype(F32)), 0.0)
    pos_i = _iota((c, 1), 0)
    pos = pos_i.astype(F32)
    d_q = jnp.exp(lg * (pos + 1.0))
    d_k = jnp.where(pos_i < nvalid, jnp.exp(lg * (nvalid - 1.0 - pos)), 0.0)
    d_c = math.exp(lg * nvalid)
    return d_intra, d_q, d_k, d_c


def _ret_lg(h):
    return math.log1p(-(2.0 ** (-5.0 - h)))


def _ret_chunks(qs, ks, vs, gs, ss, cos, sin, decs):
    n = range(len(qs))
    qr = [_rotary(q, cos, sin) for q in qs]
    kr = [_rotary(k, cos, sin) * (RET_DK ** -0.5) for k in ks]
    sc = [_dot_nt(qr[p], kr[p]) * decs[p][0] for p in n]
    o_st = [_dot(qr[p] * decs[p][1], ss[p]) for p in n]
    kv = [_dot_tn(kr[p] * decs[p][2], vs[p]) for p in n]
    o = [_dot(sc[p], vs[p]) + o_st[p] for p in n]
    s_new = [decs[p][3] * ss[p] + kv[p] for p in n]
    ys = [_rms(o[p]) * _silu(gs[p]) for p in n]
    return ys, s_new


def _ret_prompt_kernel(q_ref, k_ref, v_ref, g_ref, cos_ref, sin_ref, y_ref, so_ref, s_ref):
    c = pl.program_id(1)

    @pl.when(c == 0)
    def _():
        s_ref[...] = jnp.zeros_like(s_ref)

    hs = range(RET_HEADS)
    decs = [_ret_decays(RET_CHUNK, RET_CHUNK, _ret_lg(h)) for h in hs]
    ys, s_new = _ret_chunks([q_ref[:, h * RET_DK:(h + 1) * RET_DK] for h in hs],
                            [k_ref[:, h * RET_DK:(h + 1) * RET_DK] for h in hs],
                            [v_ref[:, h * RET_DV:(h + 1) * RET_DV] for h in hs],
                            [g_ref[:, h * RET_DV:(h + 1) * RET_DV] for h in hs],
                            [s_ref[h] for h in hs], cos_ref[...], sin_ref[...], decs)
    for h in hs:
        y_ref[:, h * RET_DV:(h + 1) * RET_DV] = ys[h].astype(BF16)
        s_ref[h] = s_new[h]

    @pl.when(c == pl.num_programs(1) - 1)
    def _():
        so_ref[...] = s_ref[...]


def _ret_prompt(p_main, cos, sin, nb, nc):
    qw = RET_HEADS * RET_DK
    vw = RET_HEADS * RET_DV
    return pl.pallas_call(
        _ret_prompt_kernel,
        grid=(nb, nc),
        in_specs=[
            pl.BlockSpec((RET_CHUNK, qw), lambda b, c: (b * nc + c, P_RQ // qw)),
            pl.BlockSpec((RET_CHUNK, qw), lambda b, c: (b * nc + c, P_RK // qw)),
            pl.BlockSpec((RET_CHUNK, vw), lambda b, c: (b * nc + c, P_RV // vw)),
            pl.BlockSpec((RET_CHUNK, vw), lambda b, c: (b * nc + c, P_RG // vw)),
            pl.BlockSpec((RET_CHUNK, LANES), lambda b, c: (c, 0)),
            pl.BlockSpec((RET_CHUNK, LANES), lambda b, c: (c, 0)),
        ],
        out_specs=[
            pl.BlockSpec((RET_CHUNK, vw), lambda b, c: (b * nc + c, 0)),
            pl.BlockSpec((None, RET_HEADS, RET_DK, RET_DV), lambda b, c: (b, 0, 0, 0)),
        ],
        out_shape=[jax.ShapeDtypeStruct((p_main.shape[0], vw), BF16),
                   jax.ShapeDtypeStruct((nb, RET_HEADS, RET_DK, RET_DV), F32)],
        scratch_shapes=[pltpu.VMEM((RET_HEADS, RET_DK, RET_DV), F32)],
        compiler_params=_cparams(("arbitrary", "arbitrary")),
        name="ret_prompt",
    )(p_main, p_main, p_main, p_main, cos, sin)


def _ret_sample_kernel(q_ref, k_ref, v_ref, g_ref, cos_ref, sin_ref, si_ref, *rest, lseg):
    y_ref, so_ref = rest[-2:]
    npair = q_ref.shape[0] // SAMPLE_SLAB
    cos = cos_ref[...]
    sin = sin_ref[...]
    hs = range(RET_HEADS)
    decs = [_ret_decays(SAMPLE_SLAB, lseg, _ret_lg(h)) for h in hs] * (2 * SAMPLE_PAIRS)
    order = [(u, t, h) for u in range(SAMPLE_PAIRS) for t in range(2) for h in hs]

    def pairs(it, carry):
        ps = [it * SAMPLE_PAIRS + u for u in range(SAMPLE_PAIRS)]
        rows = [pl.ds(pl.multiple_of(p * SAMPLE_SLAB, SAMPLE_SLAB), SAMPLE_SLAB) for p in ps]

        def slabs(ref, w):
            return [[_slab_pair(ref[r, h * w:(h + 1) * w]) for h in hs] for r in rows]

        qs, ks, vs, gs = slabs(q_ref, RET_DK), slabs(k_ref, RET_DK), slabs(v_ref, RET_DV), slabs(g_ref, RET_DV)
        ys, s_new = _ret_chunks([qs[u][h][t] for u, t, h in order], [ks[u][h][t] for u, t, h in order],
                                [vs[u][h][t] for u, t, h in order], [gs[u][h][t] for u, t, h in order],
                                [si_ref[2 * ps[u] + t, h] for u, t, h in order], cos, sin, decs)
        for idx, (u, t, h) in enumerate(order):
            so_ref[2 * ps[u] + t, h] = s_new[idx]
        for u in range(SAMPLE_PAIRS):
            base = u * 2 * RET_HEADS
            for h in hs:
                y_ref[rows[u], h * RET_DV:(h + 1) * RET_DV] = _slab_merge(
                    ys[base + h], ys[base + RET_HEADS + h]).astype(BF16)
        return carry

    lax.fori_loop(0, npair // SAMPLE_PAIRS, pairs, 0)


def _ret_sample(p_main, cos, sin, state, so_prev, y_full, layer, row0, nseq, lseg):
    qw = RET_HEADS * RET_DK
    vw = RET_HEADS * RET_DV
    rb = SAMPLE_SEQS * lseg
    blk0 = row0 // rb
    st_in, st_specs, st_out_spec, out_shapes, aliases = _sample_io(state, so_prev, y_full, layer, RET_HEADS,
                                                                   RET_DK, RET_DV, 6)
    return pl.pallas_call(
        functools.partial(_ret_sample_kernel, lseg=lseg),
        grid=(nseq // SAMPLE_SEQS,),
        in_specs=[
            pl.BlockSpec((rb, qw), lambda i: (blk0 + i, P_RQ // qw)),
            pl.BlockSpec((rb, qw), lambda i: (blk0 + i, P_RK // qw)),
            pl.BlockSpec((rb, vw), lambda i: (blk0 + i, P_RV // vw)),
            pl.BlockSpec((rb, vw), lambda i: (blk0 + i, P_RG // vw)),
            pl.BlockSpec((SAMPLE_SLAB, LANES), lambda i: (0, 0)),
            pl.BlockSpec((SAMPLE_SLAB, LANES), lambda i: (0, 0)),
        ] + st_specs,
        out_specs=[pl.BlockSpec((rb, vw), lambda i: (blk0 + i, 0)), st_out_spec],
        out_shape=out_shapes,
        input_output_aliases=aliases,
        compiler_params=_cparams(("arbitrary",)),
        name="ret_sample",
    )(p_main, p_main, p_main, p_main, cos, sin, *st_in)


def _unit_lower_inverse_all(a_list, nvalid):
    c = a_list[0].shape[0]
    n = range(len(a_list))
    i = _iota((c, c), 0)
    j = _iota((c, c), 1)
    eye = (i == j).astype(F32)
    pair = (i >> 1) == (j >> 1)
    ts = [eye - jnp.where(pair, a, 0.0) for a in a_list]
    blk = 2
    while blk < min(c, nvalid):
        sh = blk.bit_length()
        m = ((i >> sh) == (j >> sh)) & ((i & blk) != 0) & ((j & blk) == 0)
        xs = [jnp.where(m, a, 0.0).astype(BF16) for a in a_list]
        tb = [t.astype(BF16) for t in ts]
        tx = [_mm(tb[p], xs[p]).astype(BF16) for p in n]
        txt = [_mm(tx[p], tb[p]) for p in n]
        ts = [ts[p] - txt[p] for p in n]
        blk *= 2
    return ts


def _gdn_gates(small, small_t, alog_r, dtb_r, alog_c, dtb_c, c):
    r = small.shape[0]
    g = -jnp.exp(alog_r) * jax.nn.softplus(small + dtb_r)
    g_t = -jnp.exp(alog_c) * jax.nn.softplus(small_t + dtb_c)
    i = _iota((r, r), 0)
    j = _iota((r, r), 1)
    sh = c.bit_length() - 1
    same = ((i >> sh) == (j >> sh)) if r > c else True
    cum = _dot_mask_l((same & (j <= i)).astype(F32), g)
    cum_t = _dot_mask_r(g_t, (same & (i <= j)).astype(F32))
    beta = jax.nn.sigmoid(small)
    return cum, cum_t, beta


def _gdn_prep(qs, ks, vs, cum_cs, cum_rs, beta_cs, nvalid):
    n = range(len(qs))
    c = qs[0].shape[0]
    i = _iota((c, c), 0)
    j = _iota((c, c), 1)
    rowv = _iota((c, 1), 0) < nvalid
    qn = [q * lax.rsqrt(jnp.sum(q * q, axis=-1, keepdims=True) + EPS) * (GDN_DK ** -0.5) for q in qs]
    kn = [k * lax.rsqrt(jnp.sum(k * k, axis=-1, keepdims=True) + EPS) for k in ks]
    dec = [jnp.exp(jnp.where(j <= i, cum_cs[p] - cum_rs[p], -jnp.inf)) for p in n]
    e_c = [jnp.exp(cum_cs[p]) for p in n]
    kb = [kn[p] * beta_cs[p] for p in n]
    kk = [_dot_nt(kb[p], kn[p]) for p in n]
    qk = [_dot_nt(qn[p], kn[p]) for p in n]
    lower = [jnp.where(j < i, kk[p] * dec[p], 0.0) for p in n]
    ts = _unit_lower_inverse_all(lower, nvalid)
    rhs = [jnp.concatenate([vs[p] * beta_cs[p], kb[p] * e_c[p]], axis=1) for p in n]
    sol = [_dot(ts[p], rhs[p]) for p in n]
    cum_last = [cum_cs[p][nvalid - 1:nvalid, :] for p in n]
    out = []
    for p in n:
        wq = jnp.concatenate([sol[p][:, GDN_DV:], qn[p] * e_c[p]], axis=0).astype(BF16)
        kd = (kn[p] * jnp.where(rowv, jnp.exp(cum_last[p] - cum_cs[p]), 0.0)).astype(BF16)
        out.append((wq, sol[p][:, :GDN_DV], (qk[p] * dec[p]).astype(BF16), kd, jnp.exp(cum_last[p])))
    return out


def _gdn_seq(prep, zs, ss, norm_g):
    n = range(len(prep))
    c = prep[0][1].shape[0]
    st = [_mm(prep[p][0], ss[p].astype(BF16)) for p in n]
    v_new = [prep[p][1] - st[p][:c] for p in n]
    o = [st[p][c:] + _dot(prep[p][2], v_new[p]) for p in n]
    kv = [_dot_tn(prep[p][3], v_new[p]) for p in n]
    s_new = [prep[p][4] * ss[p] + kv[p] for p in n]
    ys = [_rms(o[p]) * norm_g * _silu(zs[p]) for p in n]
    return ys, s_new


def _gdn_conv(xcat, cw):
    acc = xcat[SUBLANES:, :] * cw[GDN_CONV - 1:GDN_CONV, :]
    for w in range(GDN_CONV - 1):
        sh = GDN_CONV - 1 - w
        acc = acc + pltpu.roll(xcat, sh, axis=0)[SUBLANES:, :] * cw[w:w + 1, :]
    return _silu(acc)


def _gdn_split(conv, h):
    hk = GDN_HEADS * GDN_DK
    return (conv[:, h * GDN_DK:(h + 1) * GDN_DK], conv[:, hk + h * GDN_DK: hk + (h + 1) * GDN_DK],
            conv[:, 2 * hk + h * GDN_DV: 2 * hk + (h + 1) * GDN_DV])


def _gdn_prompt_kernel(x_ref, z_ref, sm_ref, smt_ref, cw_ref, alr_ref, dtr_ref, alc_ref, dtc_ref, ng_ref,
                       y_ref, so_ref, s_ref, halo_ref):
    c = pl.program_id(1)

    @pl.when(c == 0)
    def _():
        s_ref[...] = jnp.zeros_like(s_ref)
        halo_ref[...] = jnp.zeros_like(halo_ref)

    x = x_ref[...]
    rows = x.shape[0]
    xcat = jnp.concatenate([halo_ref[...], x], axis=0)
    halo_ref[...] = x[rows - SUBLANES:, :]
    conv = _gdn_conv(xcat, cw_ref[...])
    cum, cum_t, beta = _gdn_gates(sm_ref[...], smt_ref[...], alr_ref[...], dtr_ref[...],
                                  alc_ref[...], dtc_ref[...], CHUNK)
    hs = range(GDN_HEADS)
    order = [(ch, h) for ch in range(rows // CHUNK) for h in hs]

    def rws(a, ch):
        return a[ch * CHUNK:(ch + 1) * CHUNK]

    qkv = [_gdn_split(rws(conv, ch), h) for ch, h in order]
    prep = _gdn_prep([t[0] for t in qkv], [t[1] for t in qkv], [t[2] for t in qkv],
                     [rws(cum, ch)[:, S_DA + h:S_DA + h + 1] for ch, h in order],
                     [cum_t[S_DA + h:S_DA + h + 1, ch * CHUNK:(ch + 1) * CHUNK] for ch, h in order],
                     [rws(beta, ch)[:, S_DB + h:S_DB + h + 1] for ch, h in order], CHUNK)
    ss = [s_ref[h] for h in hs]
    for ch in range(rows // CHUNK):
        ys, ss = _gdn_seq(prep[ch * GDN_HEADS:(ch + 1) * GDN_HEADS],
                          [z_ref[ch * CHUNK:(ch + 1) * CHUNK, h * GDN_DV:(h + 1) * GDN_DV] for h in hs],
                          ss, ng_ref[...])
        for h in hs:
            y_ref[ch * CHUNK:(ch + 1) * CHUNK, h * GDN_DV:(h + 1) * GDN_DV] = ys[h].astype(BF16)
    for h in hs:
        s_ref[h] = ss[h]

    @pl.when(c == pl.num_programs(1) - 1)
    def _():
        so_ref[...] = s_ref[...]


def _gdn_prompt(p_main, p_small, p_small_t, cw, alr, dtr, alc, dtc, ng, layer, nb, nc):
    zw = GDN_HEADS * GDN_DV
    rb = MIX_CHUNKS * CHUNK
    return pl.pallas_call(
        _gdn_prompt_kernel,
        grid=(nb, nc),
        in_specs=[
            pl.BlockSpec((rb, GDN_QKV), lambda b, c: (b * nc + c, P_DQKV // GDN_QKV)),
            pl.BlockSpec((rb, zw), lambda b, c: (b * nc + c, P_DZ // zw)),
            pl.BlockSpec((rb, LANES), lambda b, c: (b * nc + c, 0)),
            pl.BlockSpec((None, LANES, rb), lambda b, c: (b * nc + c, 0, 0)),
            pl.BlockSpec((None, GDN_CONV, GDN_QKV), lambda b, c: (layer, 0, 0)),
            pl.BlockSpec((None, 1, LANES), lambda b, c: (layer, 0, 0)),
            pl.BlockSpec((None, 1, LANES), lambda b, c: (layer, 0, 0)),
            pl.BlockSpec((None, LANES, 1), lambda b, c: (layer, 0, 0)),
            pl.BlockSpec((None, LANES, 1), lambda b, c: (layer, 0, 0)),
            pl.BlockSpec((None, 1, GDN_DV), lambda b, c: (layer, 0, 0)),
        ],
        out_specs=[
            pl.BlockSpec((rb, zw), lambda b, c: (b * nc + c, 0)),
            pl.BlockSpec((None, GDN_HEADS, GDN_DK, GDN_DV), lambda b, c: (b, 0, 0, 0)),
        ],
        out_shape=[jax.ShapeDtypeStruct((p_main.shape[0], zw), BF16),
                   jax.ShapeDtypeStruct((nb, GDN_HEADS, GDN_DK, GDN_DV), F32)],
        scratch_shapes=[pltpu.VMEM((GDN_HEADS, GDN_DK, GDN_DV), F32), pltpu.VMEM((SUBLANES, GDN_QKV), F32)],
        compiler_params=_cparams(("arbitrary", "arbitrary")),
        name="gdn_prompt",
    )(p_main, p_main, p_small, p_small_t, cw, alr, dtr, alc, dtc, ng)


def _gdn_sample_kernel(x_ref, z_ref, sm_ref, smt_ref, sc_ref, cw_ref, alr_ref, dtr_ref, alc_ref, dtc_ref, ng_ref,
                       si_ref, *rest, lseg):
    y_ref, so_ref = rest[-2:]
    npair = x_ref.shape[0] // SAMPLE_SLAB
    cw = cw_ref[...]
    hs = range(GDN_HEADS)

    seqs = [(u, t) for u in range(SAMPLE_PAIRS) for t in range(2)]
    order = [(u, t, h) for u, t in seqs for h in hs]

    def pairs(it, carry):
        ps = [it * SAMPLE_PAIRS + u for u in range(SAMPLE_PAIRS)]
        rows = [pl.ds(pl.multiple_of(p * SAMPLE_SLAB, SAMPLE_SLAB), SAMPLE_SLAB) for p in ps]
        xs = [_slab_pair(x_ref[r, :]) for r in rows]
        sms = [_slab_pair(sm_ref[r, :]) for r in rows]
        zs = [[_slab_pair(z_ref[r, h * GDN_DV:(h + 1) * GDN_DV]) for h in hs] for r in rows]
        convs, gates = {}, {}
        for u, t in seqs:
            xcat = jnp.concatenate([sc_ref[2 * ps[u] + t], xs[u][t]], axis=0)
            convs[u, t] = _gdn_conv(xcat, cw)
            gates[u, t] = _gdn_gates(sms[u][t], smt_ref[2 * ps[u] + t], alr_ref[...], dtr_ref[...],
                                     alc_ref[...], dtc_ref[...], SAMPLE_SLAB)
        qkv = [_gdn_split(convs[u, t], h) for u, t, h in order]
        prep = _gdn_prep([a[0] for a in qkv], [a[1] for a in qkv], [a[2] for a in qkv],
                         [gates[u, t][0][:, S_DA + h:S_DA + h + 1] for u, t, h in order],
                         [gates[u, t][1][S_DA + h:S_DA + h + 1, :] for u, t, h in order],
                         [gates[u, t][2][:, S_DB + h:S_DB + h + 1] for u, t, h in order], lseg)
        ys, s_new = _gdn_seq(prep, [zs[u][h][t] for u, t, h in order],
                             [si_ref[2 * ps[u] + t, h] for u, t, h in order], ng_ref[...])
        for idx, (u, t, h) in enumerate(order):
            so_ref[2 * ps[u] + t, h] = s_new[idx]
        for u in range(SAMPLE_PAIRS):
            base = u * 2 * GDN_HEADS
            for h in hs:
                y_ref[rows[u], h * GDN_DV:(h + 1) * GDN_DV] = _slab_merge(
                    ys[base + h], ys[base + GDN_HEADS + h]).astype(BF16)
        return carry

    lax.fori_loop(0, npair // SAMPLE_PAIRS, pairs, 0)


def _gdn_sample(p_main, p_small, p_small_t, sconv, cw, alr, dtr, alc, dtc, ng, state, so_prev, y_full, layer, row0,
                nseq, lseg):
    zw = GDN_HEADS * GDN_DV
    rb = SAMPLE_SEQS * lseg
    blk0 = row0 // rb
    st_in, st_specs, st_out_spec, out_shapes, aliases = _sample_io(state, so_prev, y_full, layer, GDN_HEADS,
                                                                   GDN_DK, GDN_DV, 11)
    return pl.pallas_call(
        functools.partial(_gdn_sample_kernel, lseg=lseg),
        grid=(nseq // SAMPLE_SEQS,),
        in_specs=[
            pl.BlockSpec((rb, GDN_QKV), lambda i: (blk0 + i, P_DQKV // GDN_QKV)),
            pl.BlockSpec((rb, zw), lambda i: (blk0 + i, P_DZ // zw)),
            pl.BlockSpec((rb, LANES), lambda i: (blk0 + i, 0)),
            pl.BlockSpec((SAMPLE_SEQS, LANES, SAMPLE_SLAB), lambda i: (i, 0, 0)),
            pl.BlockSpec((SAMPLE_SEQS, SUBLANES, GDN_QKV), lambda i: (i, 0, 0)),
            pl.BlockSpec((None, GDN_CONV, GDN_QKV), lambda i: (layer, 0, 0)),
            pl.BlockSpec((None, 1, LANES), lambda i: (layer, 0, 0)),
            pl.BlockSpec((None, 1, LANES), lambda i: (layer, 0, 0)),
            pl.BlockSpec((None, LANES, 1), lambda i: (layer, 0, 0)),
            pl.BlockSpec((None, LANES, 1), lambda i: (layer, 0, 0)),
            pl.BlockSpec((None, 1, GDN_DV), lambda i: (layer, 0, 0)),
        ] + st_specs,
        out_specs=[pl.BlockSpec((rb, zw), lambda i: (blk0 + i, 0)), st_out_spec],
        out_shape=out_shapes,
        input_output_aliases=aliases,
        compiler_params=_cparams(("arbitrary",)),
        name="gdn_sample",
    )(p_main, p_main, p_small, p_small_t, sconv, cw, alr, dtr, alc, dtc, ng, *st_in)


def _gla_scores(q, k, b, sub):
    c = q.shape[0]
    irow = _iota((sub, 1), 0)
    jrow_all = _iota((c, 1), 0)
    lane = _iota((sub, c), 1)
    blocks = []
    for blk in range(c // sub):
        r0 = blk * sub
        qi, ki, bi = q[r0:r0 + sub], k[r0:r0 + sub], b[r0:r0 + sub]
        if blk > 0:
            b0 = b[r0 - 1:r0, :]
            qt = qi * jnp.exp(bi - b0)
            kt = k * jnp.exp(jnp.where(jrow_all < r0, b0 - b, -jnp.inf))
            a = _dot_nt(qt, kt)
        else:
            a = jnp.zeros((sub, c), F32)
        for jj in range(sub):
            e = jnp.exp(jnp.where(irow >= jj, bi - bi[jj:jj + 1, :], -jnp.inf))
            col = jnp.sum(qi * e * ki[jj:jj + 1, :], axis=1, keepdims=True)
            a = jnp.where(lane == r0 + jj, col, a)
        blocks.append(a)
    return jnp.concatenate(blocks, axis=0) if len(blocks) > 1 else blocks[0]


def _gla_prep(qs, ks, vs, bs, nvalid, sub):
    n = range(len(qs))
    c = qs[0].shape[0]
    rowv = _iota((c, 1), 0) < nvalid
    qsc = [q * (GLA_DK ** -0.5) for q in qs]
    b_last = [bs[p][nvalid - 1:nvalid, :] for p in n]
    kv = [_dot_tn(ks[p] * jnp.where(rowv, jnp.exp(b_last[p] - bs[p]), 0.0), vs[p]) for p in n]
    a = [_gla_scores(qsc[p], ks[p], bs[p], sub) for p in n]
    o_in = [_dot(a[p], vs[p]) for p in n]
    return [((qsc[p] * jnp.exp(bs[p])).astype(BF16), o_in[p], kv[p], _row_to_col(jnp.exp(b_last[p]))) for p in n]


def _gla_seq(prep, zs, ss, norm_g):
    n = range(len(prep))
    o = [prep[p][1] + _mm(prep[p][0], ss[p].astype(BF16)) for p in n]
    s_new = [prep[p][3] * ss[p] + prep[p][2] for p in n]
    ys = [_rms(o[p]) * norm_g * _silu(zs[p]) for p in n]
    return ys, s_new


def _gla_cumsum(gk, c):
    r = gk.shape[0]
    i = _iota((r, r), 0)
    j = _iota((r, r), 1)
    sh = c.bit_length() - 1
    same = ((i >> sh) == (j >> sh)) if r > c else True
    return _dot_mask_l((same & (j <= i)).astype(F32), gk)


def _gla_gk(small, wup, bup):
    return jax.nn.log_sigmoid(_mm(small.astype(BF16), wup) + bup) / GLA_GATE_NORM


def _gla_prompt_kernel(q_ref, k_ref, v_ref, z_ref, sm_ref, wup_ref, bup_ref, ng_ref, y_ref, so_ref, s_ref):
    c = pl.program_id(1)

    @pl.when(c == 0)
    def _():
        s_ref[...] = jnp.zeros_like(s_ref)

    rows = q_ref.shape[0]
    b_all = _gla_cumsum(_gla_gk(sm_ref[...], wup_ref[...], bup_ref[...]), CHUNK)
    hs = range(GLA_HEADS)
    order = [(ch, h) for ch in range(rows // CHUNK) for h in hs]

    def blk(ref, ch, h, w):
        return ref[ch * CHUNK:(ch + 1) * CHUNK, h * w:(h + 1) * w]

    prep = _gla_prep([blk(q_ref, ch, h, GLA_DK) for ch, h in order], [blk(k_ref, ch, h, GLA_DK) for ch, h in order],
                     [blk(v_ref, ch, h, GLA_DV) for ch, h in order], [blk(b_all, ch, h, GLA_DK) for ch, h in order],
                     CHUNK, GLA_SUB)
    ss = [s_ref[h] for h in hs]
    for ch in range(rows // CHUNK):
        ys, ss = _gla_seq(prep[ch * GLA_HEADS:(ch + 1) * GLA_HEADS], [blk(z_ref, ch, h, GLA_DV) for h in hs],
                          ss, ng_ref[...])
        for h in hs:
            y_ref[ch * CHUNK:(ch + 1) * CHUNK, h * GLA_DV:(h + 1) * GLA_DV] = ys[h].astype(BF16)
    for h in hs:
        s_ref[h] = ss[h]

    @pl.when(c == pl.num_programs(1) - 1)
    def _():
        so_ref[...] = s_ref[...]


def _gla_prompt(p_main, p_small, wup, bup, ng, layer, nb, nc):
    qw = GLA_HEADS * GLA_DK
    vw = GLA_HEADS * GLA_DV
    rb = MIX_CHUNKS * CHUNK
    return pl.pallas_call(
        _gla_prompt_kernel,
        grid=(nb, nc),
        in_specs=[
            pl.BlockSpec((rb, qw), lambda b, c: (b * nc + c, P_LQ // qw)),
            pl.BlockSpec((rb, qw), lambda b, c: (b * nc + c, P_LK // qw)),
            pl.BlockSpec((rb, vw), lambda b, c: (b * nc + c, P_LV // vw)),
            pl.BlockSpec((rb, vw), lambda b, c: (b * nc + c, P_LGT // vw)),
            pl.BlockSpec((rb, LANES), lambda b, c: (b * nc + c, 0)),
            pl.BlockSpec((None, LANES, qw), lambda b, c: (layer, 0, 0)),
            pl.BlockSpec((None, 1, qw), lambda b, c: (layer, 0, 0)),
            pl.BlockSpec((None, 1, GLA_DV), lambda b, c: (layer, 0, 0)),
        ],
        out_specs=[
            pl.BlockSpec((rb, vw), lambda b, c: (b * nc + c, 0)),
            pl.BlockSpec((None, GLA_HEADS, GLA_DK, GLA_DV), lambda b, c: (b, 0, 0, 0)),
        ],
        out_shape=[jax.ShapeDtypeStruct((p_main.shape[0], vw), BF16),
                   jax.ShapeDtypeStruct((nb, GLA_HEADS, GLA_DK, GLA_DV), F32)],
        scratch_shapes=[pltpu.VMEM((GLA_HEADS, GLA_DK, GLA_DV), F32)],
        compiler_params=_cparams(("arbitrary", "arbitrary")),
        name="gla_prompt",
    )(p_main, p_main, p_main, p_main, p_small, wup, bup, ng)


def _gla_sample_kernel(q_ref, k_ref, v_ref, z_ref, sm_ref, wup_ref, bup_ref, ng_ref, si_ref, *rest, lseg):
    y_ref, so_ref = rest[-2:]
    npair = q_ref.shape[0] // SAMPLE_SLAB
    hs = range(GLA_HEADS)

    order = [(u, t, h) for u in range(SAMPLE_PAIRS) for t in range(2) for h in hs]

    def pairs(it, carry):
        ps = [it * SAMPLE_PAIRS + u for u in range(SAMPLE_PAIRS)]
        rows = [pl.ds(pl.multiple_of(p * SAMPLE_SLAB, SAMPLE_SLAB), SAMPLE_SLAB) for p in ps]

        def slabs(ref, w):
            return [[_slab_pair(ref[r, h * w:(h + 1) * w]) for h in hs] for r in rows]

        qs, ks, vs, zs = slabs(q_ref, GLA_DK), slabs(k_ref, GLA_DK), slabs(v_ref, GLA_DV), slabs(z_ref, GLA_DV)
        bs = [[_gla_cumsum(g, SAMPLE_SLAB) for g in _slab_pair(_gla_gk(sm_ref[r, :], wup_ref[...], bup_ref[...]))]
              for r in rows]
        prep = _gla_prep([qs[u][h][t] for u, t, h in order], [ks[u][h][t] for u, t, h in order],
                         [vs[u][h][t] for u, t, h in order],
                         [bs[u][t][:, h * GLA_DK:(h + 1) * GLA_DK] for u, t, h in order], lseg, SAMPLE_SLAB)
        ys, s_new = _gla_seq(prep, [zs[u][h][t] for u, t, h in order],
                             [si_ref[2 * ps[u] + t, h] for u, t, h in order], ng_ref[...])
        for idx, (u, t, h) in enumerate(order):
            so_ref[2 * ps[u] + t, h] = s_new[idx]
        for u in range(SAMPLE_PAIRS):
            base = u * 2 * GLA_HEADS
            for h in hs:
                y_ref[rows[u], h * GLA_DV:(h + 1) * GLA_DV] = _slab_merge(
                    ys[base + h], ys[base + GLA_HEADS + h]).astype(BF16)
        return carry

    lax.fori_loop(0, npair // SAMPLE_PAIRS, pairs, 0)


def _gla_sample(p_main, p_small, wup, bup, ng, state, so_prev, y_full, layer, row0, nseq, lseg):
    qw = GLA_HEADS * GLA_DK
    vw = GLA_HEADS * GLA_DV
    rb = SAMPLE_SEQS * lseg
    blk0 = row0 // rb
    st_in, st_specs, st_out_spec, out_shapes, aliases = _sample_io(state, so_prev, y_full, layer, GLA_HEADS,
                                                                   GLA_DK, GLA_DV, 8)
    return pl.pallas_call(
        functools.partial(_gla_sample_kernel, lseg=lseg),
        grid=(nseq // SAMPLE_SEQS,),
        in_specs=[
            pl.BlockSpec((rb, qw), lambda i: (blk0 + i, P_LQ // qw)),
            pl.BlockSpec((rb, qw), lambda i: (blk0 + i, P_LK // qw)),
            pl.BlockSpec((rb, vw), lambda i: (blk0 + i, P_LV // vw)),
            pl.BlockSpec((rb, vw), lambda i: (blk0 + i, P_LGT // vw)),
            pl.BlockSpec((rb, LANES), lambda i: (blk0 + i, 0)),
            pl.BlockSpec((None, LANES, qw), lambda i: (layer, 0, 0)),
            pl.BlockSpec((None, 1, qw), lambda i: (layer, 0, 0)),
            pl.BlockSpec((None, 1, GLA_DV), lambda i: (layer, 0, 0)),
        ] + st_specs,
        out_specs=[pl.BlockSpec((rb, vw), lambda i: (blk0 + i, 0)), st_out_spec],
        out_shape=out_shapes,
        input_output_aliases=aliases,
        compiler_params=_cparams(("arbitrary",)),
        name="gla_sample",
    )(p_main, p_main, p_main, p_main, p_small, wup, bup, ng, *st_in)


def _rope_tables(pos0, length):
    inv = 1.0 / (ROPE_BASE ** jnp.linspace(0.0, 1.0, RET_DK // 2, dtype=F32))
    ang = (jnp.arange(length, dtype=F32) + pos0)[:, None] * inv[None, :]
    cos = jnp.repeat(jnp.cos(ang), 2, axis=1)
    sin = jnp.stack([-jnp.sin(ang), jnp.sin(ang)], axis=-1).reshape(length, RET_DK)
    return cos, sin


def _lane_pad(v, off):
    n = v.shape[-1]
    return jnp.pad(v, ((0, 0), (off, LANES - off - n)))[:, None, :]


def kernel(x_prompt, x_sample, state_ret, state_gdn, state_gdn_conv, state_gla, norm_mix, norm_ffn, norm_final,
           w_in, b_merge, gdn_conv_w, gdn_a_log, gdn_dt_bias, gdn_norm, gla_w_up, gla_b_up, gla_norm, w_branch,
           w_o, w_gate_up, w_down):
    nb, seq, d = x_prompt.shape
    nsq, lseg, _ = x_sample.shape
    depth = w_in.shape[0]
    n_p = nb * seq
    n_s = nsq * lseg
    mix_rows = MIX_CHUNKS * CHUNK
    nc = seq // mix_rows
    nc_ret = seq // RET_CHUNK
    tm = _dense_tiles(n_p + n_s)
    tm_out = math.gcd(n_p, n_s)
    assert seq % mix_rows == 0 and seq % RET_CHUNK == 0
    assert nsq % SAMPLE_SEQS == 0 and n_p % (SAMPLE_SEQS * lseg) == 0
    assert 2 * lseg == SAMPLE_SLAB and lseg >= GDN_CONV - 1

    w_t = jnp.swapaxes(w_in, 1, 2)
    w_main = _pack_w_in(w_t, 1024, 1024)
    w_small = _pack_w_small(w_t, 512)
    wb = w_branch.astype(BF16)
    wo = w_o.astype(BF16)
    wgu = w_gate_up.astype(BF16)
    wdn = w_down.astype(BF16)
    g_mix = norm_mix[:, None, :]
    g_ffn = norm_ffn[:, None, :]
    bm = b_merge.reshape(depth, N_BRANCH, 1, D_MODEL)
    alr = _lane_pad(gdn_a_log, S_DA)
    dtr = _lane_pad(gdn_dt_bias, S_DA)
    alc = jnp.swapaxes(alr, 1, 2)
    dtc = jnp.swapaxes(dtr, 1, 2)
    gdn_ng = gdn_norm[:, None, :]
    gla_ng = gla_norm[:, None, :]
    wup = jnp.pad(gla_w_up, ((0, 0), (S_LLR, LANES - S_LLR - GLA_LOWRANK), (0, 0))).astype(BF16)
    bup = gla_b_up[:, None, :]
    cos_p, sin_p = _rope_tables(0.0, seq)
    cos_s, sin_s = _rope_tables(float(PAST_LEN), lseg)
    cos_s = jnp.tile(cos_s, (SAMPLE_SLAB // lseg, 1))
    sin_s = jnp.tile(sin_s, (SAMPLE_SLAB // lseg, 1))

    x = jnp.concatenate([x_prompt.reshape(n_p, d), x_sample.reshape(n_s, d)], axis=0)

    outs = {k: [] for k in ("p_ret", "p_gdn", "p_conv", "p_gla", "s_conv")}
    s_ret = s_gdn = s_gla = None
    for layer in range(depth):
        p_main, p_gate, p_small = _in_proj(x, g_mix, w_main, w_small, layer, tm, 1024)
        pst_p = jnp.swapaxes(p_small[:n_p].reshape(nb * nc, mix_rows, LANES), 1, 2)
        pst_s = jnp.swapaxes(jnp.pad(p_small[n_p:].reshape(nsq, lseg, LANES),
                                     ((0, 0), (0, SAMPLE_SLAB - lseg), (0, 0))), 1, 2)

        y_ret, st = _ret_prompt(p_main, cos_p, sin_p, nb, nc_ret)
        outs["p_ret"].append(st)
        y_ret, s_ret = _ret_sample(p_main, cos_s, sin_s, state_ret, s_ret, y_ret, layer, n_p, nsq, lseg)

        y_gdn, st = _gdn_prompt(p_main, p_small, pst_p, gdn_conv_w, alr, dtr, alc, dtc, gdn_ng, layer, nb, nc)
        outs["p_gdn"].append(st)
        sconv = jnp.pad(state_gdn_conv[layer], ((0, 0), (SUBLANES - (GDN_CONV - 1), 0), (0, 0)))
        y_gdn, s_gdn = _gdn_sample(p_main, p_small, pst_s, sconv, gdn_conv_w, alr, dtr, alc, dtc, gdn_ng,
                                   state_gdn, s_gdn, y_gdn, layer, n_p, nsq, lseg)
        outs["p_conv"].append(jnp.stack([
            lax.slice(p_main, ((b + 1) * seq - (GDN_CONV - 1), P_DQKV), ((b + 1) * seq, P_DQKV + GDN_QKV))
            for b in range(nb)]))
        dq_s = lax.slice(p_main, (n_p, P_DQKV), (n_p + n_s, P_DQKV + GDN_QKV)).reshape(nsq, lseg, GDN_QKV)
        outs["s_conv"].append(jnp.concatenate([state_gdn_conv[layer], dq_s], axis=1)[:, -(GDN_CONV - 1):])

        y_gla, st = _gla_prompt(p_main, p_small, wup, bup, gla_ng, layer, nb, nc)
        outs["p_gla"].append(st)
        y_gla, s_gla = _gla_sample(p_main, p_small, wup, bup, gla_ng, state_gla, s_gla, y_gla, layer, n_p, nsq, lseg)

        mrg = _merge((y_ret, y_gdn, y_gla), wb, p_gate, bm, layer, tm, 512)
        x = _out_proj(mrg, wo, x, layer, tm, 1024)
        x = _ffn(x, g_ffn, wgu, wdn, layer, tm, 512)

    g_fin = norm_final[None, :]
    y_p = _final_norm(x, g_fin, 0, n_p, tm_out)
    y_s = _final_norm(x, g_fin, n_p, n_s, tm_out)
    st = {k: jnp.stack(v) for k, v in outs.items()}
    return (y_p.reshape(nb, seq, d), y_s.reshape(nsq, lseg, d),
            st["p_ret"], st["p_gdn"], st["p_conv"], st["p_gla"],
            s_ret, s_gdn, st["s_conv"], s_gla)
```

```python
import functools
import math

import jax
import jax.numpy as jnp
from jax import lax
from jax.experimental import pallas as pl
from jax.experimental.pallas import tpu as pltpu

F32 = jnp.float32
BF16 = jnp.bfloat16

D_MODEL = 2048
RET_HEADS, RET_DK, RET_DV = 4, 128, 256
GDN_HEADS, GDN_DK, GDN_DV, GDN_CONV = 8, 128, 128, 4
GDN_QKV = GDN_HEADS * (2 * GDN_DK + GDN_DV)
GLA_HEADS, GLA_DK, GLA_DV, GLA_LOWRANK = 4, 128, 256, 16
GLA_GATE_NORM = 16.0
N_BRANCH = 3
BRANCH_WIDTH = 1024
EPS = 1e-6
ROPE_BASE = 10000.0
PAST_LEN = 16384

LANES = 128
SUBLANES = 8
CHUNK = 64
MIX_CHUNKS = 4
RET_CHUNK = 256
GLA_SUB = 8
SAMPLE_SLAB = 8
SAMPLE_SEQS = 8
SAMPLE_PAIRS = 4
VMEM_LIMIT = 56 * 1024 * 1024

W_SEGS = ((0, 6144), (6160, 9232), (9248, 16416))
W_SMALL = ((6144, 6160), (9232, 9248))
P_RQ, P_RK, P_RV, P_RG = 0, 512, 1024, 2048
P_DQKV, P_DZ = 3072, 6144
P_LQ, P_LK, P_LV, P_LGT, P_MG = 7168, 7680, 8192, 9216, 10240
P_MAIN = 16384
S_DA, S_DB, S_LLR = 0, 8, 16


def _cparams(sem):
    return pltpu.CompilerParams(dimension_semantics=sem, vmem_limit_bytes=VMEM_LIMIT)


def _dense_tiles(m):
    best = 64
    for t in range(64, 1089, 64):
        if m % t == 0:
            best = t
    return best


_mm = functools.partial(jnp.dot, preferred_element_type=F32)


def _dot(a, b):
    return _mm(a.astype(BF16), b.astype(BF16))


def _dot_nt(a, b):
    return lax.dot_general(a.astype(BF16), b.astype(BF16), (((1,), (1,)), ((), ())),
                           preferred_element_type=F32)


def _dot_tn(a, b):
    return lax.dot_general(a.astype(BF16), b.astype(BF16), (((0,), (0,)), ((), ())),
                           preferred_element_type=F32)


def _split3(x):
    hi = x.astype(BF16)
    r = x - hi.astype(F32)
    mid = r.astype(BF16)
    lo = (r - mid.astype(F32)).astype(BF16)
    return hi, mid, lo


def _dot_mask_l(m, x):
    mb = m.astype(BF16)
    hi, mid, lo = _split3(x)
    return _mm(mb, hi) + (_mm(mb, mid) + _mm(mb, lo))


def _dot_mask_r(x, m):
    mb = m.astype(BF16)
    hi, mid, lo = _split3(x)
    return _mm(hi, mb) + (_mm(mid, mb) + _mm(lo, mb))


def _iota(shape, dim):
    return lax.broadcasted_iota(jnp.int32, shape, dim)


def _silu(x):
    return x * jax.nn.sigmoid(x)


def _rms(x, eps=EPS):
    return x * lax.rsqrt(jnp.mean(x * x, axis=-1, keepdims=True) + eps)


def _row_to_col(r):
    n = r.shape[1]
    eye = _iota((n, n), 0) == _iota((n, n), 1)
    return jnp.sum(jnp.where(eye, jnp.broadcast_to(r, (n, n)), 0.0), axis=1, keepdims=True)


def _tril_incl(c):
    return (_iota((c, c), 1) <= _iota((c, c), 0)).astype(F32)


PACK_TAIL = 32


def _pack_kernel(a_ref, b_ref, o_ref, *, edges):
    j = pl.program_id(1)
    lo = 0
    for shift, hi in edges:
        def _(shift=shift):
            src = a_ref[...]
            if shift:
                src = jnp.concatenate([src[shift:], b_ref[:shift]], axis=0)
            o_ref[...] = src.T.astype(BF16)
        pl.when((j >= lo) & (j < hi))(_)
        lo = hi


def _pack_w_in(w_t, tk, tn):
    depth, _, d = w_t.shape
    edges, out0 = [], 0
    for a, b in W_SEGS:
        assert out0 % tn == 0 and (b - a) % tn == 0 and 0 <= a - out0 <= PACK_TAIL and (a - out0) % SUBLANES == 0
        edges.append((a - out0, (out0 + b - a) // tn))
        out0 += b - a
    assert out0 == P_MAIN
    return pl.pallas_call(
        functools.partial(_pack_kernel, edges=tuple(edges)),
        grid=(depth, P_MAIN // tn, d // tk),
        in_specs=[
            pl.BlockSpec((None, tn, tk), lambda l, j, k: (l, j, k)),
            pl.BlockSpec((None, PACK_TAIL, tk), lambda l, j, k: (l, (j + 1) * (tn // PACK_TAIL), k)),
        ],
        out_specs=pl.BlockSpec((None, tk, tn), lambda l, j, k: (l, k, j)),
        out_shape=jax.ShapeDtypeStruct((depth, d, P_MAIN), BF16),
        compiler_params=_cparams(("arbitrary", "arbitrary", "arbitrary")),
        name="pack_w_in",
    )(w_t, w_t)


def _pack_small_kernel(a_ref, b_ref, o_ref):
    rows = jnp.concatenate([a_ref[...], b_ref[...],
                            jnp.zeros((LANES - a_ref.shape[0] - b_ref.shape[0], a_ref.shape[1]), F32)], axis=0)
    o_ref[...] = rows.T.astype(BF16)


def _pack_w_small(w_t, tk):
    depth, _, d = w_t.shape
    (a0, a1), (b0, b1) = W_SMALL
    na, nb = a1 - a0, b1 - b0
    assert a0 % na == 0 and b0 % nb == 0 and na % SUBLANES == 0 and nb % SUBLANES == 0
    return pl.pallas_call(
        _pack_small_kernel,
        grid=(depth, d // tk),
        in_specs=[pl.BlockSpec((None, na, tk), lambda l, k: (l, a0 // na, k)),
                  pl.BlockSpec((None, nb, tk), lambda l, k: (l, b0 // nb, k))],
        out_specs=pl.BlockSpec((None, tk, LANES), lambda l, k: (l, k, 0)),
        out_shape=jax.ShapeDtypeStruct((depth, d, LANES), BF16),
        compiler_params=_cparams(("arbitrary", "arbitrary")),
        name="pack_w_small",
    )(w_t, w_t)


def _in_proj_kernel(x_ref, g_ref, w_ref, ws_ref, p_ref, pg_ref, ps_ref, h_ref, *, n_f32):
    j = pl.program_id(1)

    @pl.when(j == 0)
    def _():
        h = (_rms(x_ref[...]) * g_ref[...]).astype(BF16)
        h_ref[...] = h
        ps_ref[...] = _mm(h, ws_ref[...])

    @pl.when(j < n_f32)
    def _():
        p_ref[...] = _mm(h_ref[...], w_ref[...])

    @pl.when(j >= n_f32)
    def _():
        pg_ref[...] = _mm(h_ref[...], w_ref[...]).astype(BF16)


def _in_proj(x, g, w_main, w_small, layer, tm, tn):
    m, d = x.shape
    n = w_main.shape[2]
    n_f32 = P_MG // tn
    assert P_MG % tn == 0 and (n - P_MG) % tn == 0
    return pl.pallas_call(
        functools.partial(_in_proj_kernel, n_f32=n_f32),
        grid=(m // tm, n // tn),
        in_specs=[
            pl.BlockSpec((tm, d), lambda i, j: (i, 0)),
            pl.BlockSpec((None, 1, d), lambda i, j: (layer, 0, 0)),
            pl.BlockSpec((None, d, tn), lambda i, j: (layer, 0, j)),
            pl.BlockSpec((None, d, LANES), lambda i, j: (layer, 0, 0)),
        ],
        out_specs=[
            pl.BlockSpec((tm, tn), lambda i, j: (i, jnp.minimum(j, n_f32 - 1))),
            pl.BlockSpec((tm, tn), lambda i, j: (i, jnp.maximum(j - n_f32, 0))),
            pl.BlockSpec((tm, LANES), lambda i, j: (i, 0)),
        ],
        out_shape=[jax.ShapeDtypeStruct((m, P_MG), F32), jax.ShapeDtypeStruct((m, n - P_MG), BF16),
                   jax.ShapeDtypeStruct((m, LANES), F32)],
        scratch_shapes=[pltpu.VMEM((tm, d), BF16)],
        compiler_params=_cparams(("arbitrary", "arbitrary")),
        name="in_proj",
    )(x, g, w_main, w_small)


def _merge_kernel(y0_ref, y1_ref, y2_ref, w0_ref, w1_ref, w2_ref, g0_ref, g1_ref, g2_ref,
                  b0_ref, b1_ref, b2_ref, o_ref):
    acc = None
    for y_ref, w_ref, g_ref, b_ref in ((y0_ref, w0_ref, g0_ref, b0_ref), (y1_ref, w1_ref, g1_ref, b1_ref),
                                       (y2_ref, w2_ref, g2_ref, b2_ref)):
        t = jax.nn.sigmoid(g_ref[...].astype(F32) + b_ref[...]) * _mm(y_ref[...], w_ref[...])
        acc = t if acc is None else acc + t
    o_ref[...] = acc.astype(BF16)


def _merge(ys, wb, p_gate, b_merge, layer, tm, tn):
    m, bw = ys[0].shape
    d = wb.shape[3]
    nj = d // tn
    mg0 = 0

    def y_spec():
        return pl.BlockSpec((tm, bw), lambda i, j: (i, 0))

    def w_spec(b):
        return pl.BlockSpec((None, None, bw, tn), lambda i, j: (layer, b, 0, j))

    def g_spec(b):
        return pl.BlockSpec((tm, tn), lambda i, j: (i, mg0 + b * nj + j))

    def b_spec(b):
        return pl.BlockSpec((None, None, 1, tn), lambda i, j: (layer, b, 0, j))

    rng = range(N_BRANCH)
    return pl.pallas_call(
        _merge_kernel,
        grid=(m // tm, nj),
        in_specs=[y_spec() for _ in rng] + [w_spec(b) for b in rng] + [g_spec(b) for b in rng]
        + [b_spec(b) for b in rng],
        out_specs=pl.BlockSpec((tm, tn), lambda i, j: (i, j)),
        out_shape=jax.ShapeDtypeStruct((m, d), BF16),
        compiler_params=_cparams(("arbitrary", "arbitrary")),
        name="merge",
    )(*ys, wb, wb, wb, p_gate, p_gate, p_gate, b_merge, b_merge, b_merge)


def _out_proj_kernel(m_ref, w_ref, x_ref, o_ref):
    o_ref[...] = x_ref[...] + _mm(m_ref[...], w_ref[...])


def _out_proj(mrg, w_o, x, layer, tm, tn):
    m, d = x.shape
    return pl.pallas_call(
        _out_proj_kernel,
        grid=(m // tm, d // tn),
        in_specs=[
            pl.BlockSpec((tm, d), lambda i, j: (i, 0)),
            pl.BlockSpec((None, d, tn), lambda i, j: (layer, 0, j)),
            pl.BlockSpec((tm, tn), lambda i, j: (i, j)),
        ],
        out_specs=pl.BlockSpec((tm, tn), lambda i, j: (i, j)),
        out_shape=jax.ShapeDtypeStruct((m, d), F32),
        compiler_params=_cparams(("arbitrary", "arbitrary")),
        name="out_proj",
    )(mrg, w_o, x)


def _ffn_kernel(x_ref, g_ref, wg_ref, wu_ref, wd_ref, o_ref, h_ref):
    j = pl.program_id(1)

    @pl.when(j == 0)
    def _():
        x = x_ref[...]
        h_ref[...] = (_rms(x) * g_ref[...]).astype(BF16)
        o_ref[...] = x

    h = h_ref[...]
    act = (_silu(_mm(h, wg_ref[...])) * _mm(h, wu_ref[...])).astype(BF16)
    o_ref[...] += _mm(act, wd_ref[...])


def _ffn(x, g, w_gu, w_dn, layer, tm, tf):
    m, d = x.shape
    nf = w_dn.shape[1] // tf
    return pl.pallas_call(
        _ffn_kernel,
        grid=(m // tm, nf),
        in_specs=[
            pl.BlockSpec((tm, d), lambda i, j: (i, 0), pipeline_mode=pl.Buffered(1)),
            pl.BlockSpec((None, 1, d), lambda i, j: (layer, 0, 0)),
            pl.BlockSpec((None, d, tf), lambda i, j: (layer, 0, j)),
            pl.BlockSpec((None, d, tf), lambda i, j: (layer, 0, nf + j)),
            pl.BlockSpec((None, tf, d), lambda i, j: (layer, j, 0)),
        ],
        out_specs=pl.BlockSpec((tm, d), lambda i, j: (i, 0)),
        out_shape=jax.ShapeDtypeStruct((m, d), F32),
        scratch_shapes=[pltpu.VMEM((tm, d), BF16)],
        compiler_params=_cparams(("arbitrary", "arbitrary")),
        name="ffn",
    )(x, g, w_gu, w_gu, w_dn)


def _final_norm_kernel(x_ref, g_ref, o_ref):
    o_ref[...] = _rms(x_ref[...]) * g_ref[...]


def _final_norm(x, g, row0, rows, tm):
    d = x.shape[1]
    blk0 = row0 // tm
    return pl.pallas_call(
        _final_norm_kernel,
        grid=(rows // tm,),
        in_specs=[pl.BlockSpec((tm, d), lambda i: (blk0 + i, 0)), pl.BlockSpec((1, d), lambda i: (0, 0))],
        out_specs=pl.BlockSpec((tm, d), lambda i: (i, 0)),
        out_shape=jax.ShapeDtypeStruct((rows, d), F32),
        compiler_params=_cparams(("arbitrary",)),
        name="final_norm",
    )(x, g)


def _slab_pair(x):
    return x, pltpu.roll(x, SAMPLE_SLAB // 2, axis=0)


def _slab_merge(ya, yb):
    top = _iota(ya.shape, 0) < SAMPLE_SLAB // 2
    return jnp.where(top, ya, pltpu.roll(yb, SAMPLE_SLAB // 2, axis=0))


def _sample_io(state, so_prev, y_full, layer, heads, dk, dv, n_fixed):
    blk = pl.BlockSpec((None, SAMPLE_SEQS, heads, dk, dv), lambda i: (layer, i, 0, 0, 0))
    ins, specs = [state, y_full], [blk, pl.BlockSpec(memory_space=pl.ANY)]
    aliases = {n_fixed + 1: 0}
    if so_prev is not None:
        ins.append(so_prev)
        specs.append(pl.BlockSpec(memory_space=pl.ANY))
        aliases[n_fixed + 2] = 1
    out_shapes = [jax.ShapeDtypeStruct(y_full.shape, y_full.dtype), jax.ShapeDtypeStruct(state.shape, F32)]
    return ins, specs, blk, out_shapes, aliases


def _rotary(t, cos, sin_signed):
    even = (_iota(t.shape, 1) & 1) == 0
    nxt = pltpu.roll(t, LANES - 1, axis=1)
    prv = pltpu.roll(t, 1, axis=1)
    return t * cos + jnp.where(even, nxt, prv) * sin_signed


def _ret_decays(c, nvalid, lg):
    i = _iota((c, c), 0)
    j = _iota((c, c), 1)
    diff = i - j
    d_intra = jnp.where(diff >= 0, jnp.exp(lg * jnp.maximum(diff, 0).astype(F32)), 0.0)
    pos_i = _iota((c, 1), 0)
    pos = pos_i.astype(F32)
    d_q = jnp.exp(lg * (pos + 1.0))
    d_k = jnp.where(pos_i < nvalid, jnp.exp(lg * (nvalid - 1.0 - pos)), 0.0)
    d_c = math.exp(lg * nvalid)
    return d_intra, d_q, d_k, d_c


def _ret_lg(h):
    return math.log1p(-(2.0 ** (-5.0 - h)))


def _ret_chunks(qs, ks, vs, gs, ss, cos, sin, decs):
    n = range(len(qs))
    qr = [_rotary(q, cos, sin) for q in qs]
    kr = [_rotary(k, cos, sin) * (RET_DK ** -0.5) for k in ks]
    sc = [_dot_nt(qr[p], kr[p]) * decs[p][0] for p in n]
    o_st = [_dot(qr[p] * decs[p][1], ss[p]) for p in n]
    kv = [_dot_tn(kr[p] * decs[p][2], vs[p]) for p in n]
    o = [_dot(sc[p], vs[p]) + o_st[p] for p in n]
    s_new = [decs[p][3] * ss[p] + kv[p] for p in n]
    ys = [_rms(o[p]) * _silu(gs[p]) for p in n]
    return ys, s_new


def _ret_prompt_kernel(q_ref, k_ref, v_ref, g_ref, cos_ref, sin_ref, y_ref, so_ref, s_ref):
    c = pl.program_id(1)

    @pl.when(c == 0)
    def _():
        s_ref[...] = jnp.zeros_like(s_ref)

    hs = range(RET_HEADS)
    decs = [_ret_decays(RET_CHUNK, RET_CHUNK, _ret_lg(h)) for h in hs]
    ys, s_new = _ret_chunks([q_ref[:, h * RET_DK:(h + 1) * RET_DK] for h in hs],
                            [k_ref[:, h * RET_DK:(h + 1) * RET_DK] for h in hs],
                            [v_ref[:, h * RET_DV:(h + 1) * RET_DV] for h in hs],
                            [g_ref[:, h * RET_DV:(h + 1) * RET_DV] for h in hs],
                            [s_ref[h] for h in hs], cos_ref[...], sin_ref[...], decs)
    for h in hs:
        y_ref[:, h * RET_DV:(h + 1) * RET_DV] = ys[h].astype(BF16)
        s_ref[h] = s_new[h]

    @pl.when(c == pl.num_programs(1) - 1)
    def _():
        so_ref[...] = s_ref[...]


def _ret_prompt(p_main, cos, sin, nb, nc):
    qw = RET_HEADS * RET_DK
    vw = RET_HEADS * RET_DV
    return pl.pallas_call(
        _ret_prompt_kernel,
        grid=(nb, nc),
        in_specs=[
            pl.BlockSpec((RET_CHUNK, qw), lambda b, c: (b * nc + c, P_RQ // qw)),
            pl.BlockSpec((RET_CHUNK, qw), lambda b, c: (b * nc + c, P_RK // qw)),
            pl.BlockSpec((RET_CHUNK, vw), lambda b, c: (b * nc + c, P_RV // vw)),
            pl.BlockSpec((RET_CHUNK, vw), lambda b, c: (b * nc + c, P_RG // vw)),
            pl.BlockSpec((RET_CHUNK, LANES), lambda b, c: (c, 0)),
            pl.BlockSpec((RET_CHUNK, LANES), lambda b, c: (c, 0)),
        ],
        out_specs=[
            pl.BlockSpec((RET_CHUNK, vw), lambda b, c: (b * nc + c, 0)),
            pl.BlockSpec((None, RET_HEADS, RET_DK, RET_DV), lambda b, c: (b, 0, 0, 0)),
        ],
        out_shape=[jax.ShapeDtypeStruct((p_main.shape[0], vw), BF16),
                   jax.ShapeDtypeStruct((nb, RET_HEADS, RET_DK, RET_DV), F32)],
        scratch_shapes=[pltpu.VMEM((RET_HEADS, RET_DK, RET_DV), F32)],
        compiler_params=_cparams(("arbitrary", "arbitrary")),
        name="ret_prompt",
    )(p_main, p_main, p_main, p_main, cos, sin)


def _ret_sample_kernel(q_ref, k_ref, v_ref, g_ref, cos_ref, sin_ref, si_ref, *rest, lseg):
    y_ref, so_ref = rest[-2:]
    npair = q_ref.shape[0] // SAMPLE_SLAB
    cos = cos_ref[...]
    sin = sin_ref[...]
    hs = range(RET_HEADS)
    decs = [_ret_decays(SAMPLE_SLAB, lseg, _ret_lg(h)) for h in hs] * (2 * SAMPLE_PAIRS)
    order = [(u, t, h) for u in range(SAMPLE_PAIRS) for t in range(2) for h in hs]

    def pairs(it, carry):
        ps = [it * SAMPLE_PAIRS + u for u in range(SAMPLE_PAIRS)]
        rows = [pl.ds(pl.multiple_of(p * SAMPLE_SLAB, SAMPLE_SLAB), SAMPLE_SLAB) for p in ps]

        def slabs(ref, w):
            return [[_slab_pair(ref[r, h * w:(h + 1) * w]) for h in hs] for r in rows]

        qs, ks, vs, gs = slabs(q_ref, RET_DK), slabs(k_ref, RET_DK), slabs(v_ref, RET_DV), slabs(g_ref, RET_DV)
        ys, s_new = _ret_chunks([qs[u][h][t] for u, t, h in order], [ks[u][h][t] for u, t, h in order],
                                [vs[u][h][t] for u, t, h in order], [gs[u][h][t] for u, t, h in order],
                                [si_ref[2 * ps[u] + t, h] for u, t, h in order], cos, sin, decs)
        for idx, (u, t, h) in enumerate(order):
            so_ref[2 * ps[u] + t, h] = s_new[idx]
        for u in range(SAMPLE_PAIRS):
            base = u * 2 * RET_HEADS
            for h in hs:
                y_ref[rows[u], h * RET_DV:(h + 1) * RET_DV] = _slab_merge(
                    ys[base + h], ys[base + RET_HEADS + h]).astype(BF16)
        return carry

    lax.fori_loop(0, npair // SAMPLE_PAIRS, pairs, 0)


def _ret_sample(p_main, cos, sin, state, so_prev, y_full, layer, row0, nseq, lseg):
    qw = RET_HEADS * RET_DK
    vw = RET_HEADS * RET_DV
    rb = SAMPLE_SEQS * lseg
    blk0 = row0 // rb
    st_in, st_specs, st_out_spec, out_shapes, aliases = _sample_io(state, so_prev, y_full, layer, RET_HEADS,
                                                                   RET_DK, RET_DV, 6)
    return pl.pallas_call(
        functools.partial(_ret_sample_kernel, lseg=lseg),
        grid=(nseq // SAMPLE_SEQS,),
        in_specs=[
            pl.BlockSpec((rb, qw), lambda i: (blk0 + i, P_RQ // qw)),
            pl.BlockSpec((rb, qw), lambda i: (blk0 + i, P_RK // qw)),
            pl.BlockSpec((rb, vw), lambda i: (blk0 + i, P_RV // vw)),
            pl.BlockSpec((rb, vw), lambda i: (blk0 + i, P_RG // vw)),
            pl.BlockSpec((SAMPLE_SLAB, LANES), lambda i: (0, 0)),
            pl.BlockSpec((SAMPLE_SLAB, LANES), lambda i: (0, 0)),
        ] + st_specs,
        out_specs=[pl.BlockSpec((rb, vw), lambda i: (blk0 + i, 0)), st_out_spec],
        out_shape=out_shapes,
        input_output_aliases=aliases,
        compiler_params=_cparams(("arbitrary",)),
        name="ret_sample",
    )(p_main, p_main, p_main, p_main, cos, sin, *st_in)


def _unit_lower_inverse_all(a_list, nvalid):
    c = a_list[0].shape[0]
    n = range(len(a_list))
    i = _iota((c, c), 0)
    j = _iota((c, c), 1)
    eye = (i == j).astype(F32)
    pair = (i >> 1) == (j >> 1)
    ts = [eye - jnp.where(pair, a, 0.0) for a in a_list]
    blk = 2
    while blk < min(c, nvalid):
        sh = blk.bit_length()
        m = ((i >> sh) == (j >> sh)) & ((i & blk) != 0) & ((j & blk) == 0)
        xs = [jnp.where(m, a, 0.0).astype(BF16) for a in a_list]
        tb = [t.astype(BF16) for t in ts]
        tx = [_mm(tb[p], xs[p]).astype(BF16) for p in n]
        txt = [_mm(tx[p], tb[p]) for p in n]
        ts = [ts[p] - txt[p] for p in n]
        blk *= 2
    return ts


def _gdn_gates(small, small_t, alog_r, dtb_r, alog_c, dtb_c, c):
    r = small.shape[0]
    g = -jnp.exp(alog_r) * jax.nn.softplus(small + dtb_r)
    g_t = -jnp.exp(alog_c) * jax.nn.softplus(small_t + dtb_c)
    i = _iota((r, r), 0)
    j = _iota((r, r), 1)
    sh = c.bit_length() - 1
    same = ((i >> sh) == (j >> sh)) if r > c else True
    cum = _dot_mask_l((same & (j <= i)).astype(F32), g)
    cum_t = _dot_mask_r(g_t, (same & (i <= j)).astype(F32))
    beta = jax.nn.sigmoid(small)
    return cum, cum_t, beta


def _gdn_prep(qs, ks, vs, cum_cs, cum_rs, beta_cs, nvalid):
    n = range(len(qs))
    c = qs[0].shape[0]
    i = _iota((c, c), 0)
    j = _iota((c, c), 1)
    rowv = _iota((c, 1), 0) < nvalid
    qn = [q * (lax.rsqrt(jnp.sum(q * q, axis=-1, keepdims=True) + EPS) * (GDN_DK ** -0.5)) for q in qs]
    kn = [k * lax.rsqrt(jnp.sum(k * k, axis=-1, keepdims=True) + EPS) for k in ks]
    dec = [jnp.exp(jnp.where(j <= i, cum_cs[p] - cum_rs[p], -jnp.inf)) for p in n]
    e_c = [jnp.exp(cum_cs[p]) for p in n]
    kb = [kn[p] * beta_cs[p] for p in n]
    kk = [_dot_nt(kb[p], kn[p]) for p in n]
    qk = [_dot_nt(qn[p], kn[p]) for p in n]
    lower = [jnp.where(j < i, kk[p] * dec[p], 0.0) for p in n]
    ts = _unit_lower_inverse_all(lower, nvalid)
    rhs = [jnp.concatenate([vs[p] * beta_cs[p], kb[p] * e_c[p]], axis=1) for p in n]
    sol = [_dot(ts[p], rhs[p]) for p in n]
    cum_last = [cum_cs[p][nvalid - 1:nvalid, :] for p in n]
    out = []
    for p in n:
        wq = jnp.concatenate([sol[p][:, GDN_DV:], qn[p] * e_c[p]], axis=0).astype(BF16)
        kd = (kn[p] * jnp.where(rowv, jnp.exp(cum_last[p] - cum_cs[p]), 0.0)).astype(BF16)
        out.append((wq, sol[p][:, :GDN_DV], (qk[p] * dec[p]).astype(BF16), kd, jnp.exp(cum_last[p])))
    return out


def _gdn_seq(prep, zs, ss, norm_g):
    n = range(len(prep))
    c = prep[0][1].shape[0]
    st = [_mm(prep[p][0], ss[p].astype(BF16)) for p in n]
    v_new = [prep[p][1] - st[p][:c] for p in n]
    o = [st[p][c:] + _dot(prep[p][2], v_new[p]) for p in n]
    kv = [_dot_tn(prep[p][3], v_new[p]) for p in n]
    s_new = [prep[p][4] * ss[p] + kv[p] for p in n]
    ys = [_rms(o[p]) * norm_g * _silu(zs[p]) for p in n]
    return ys, s_new


def _gdn_conv(xcat, cw):
    acc = xcat[SUBLANES:, :] * cw[GDN_CONV - 1:GDN_CONV, :]
    for w in range(GDN_CONV - 1):
        sh = GDN_CONV - 1 - w
        acc = acc + pltpu.roll(xcat, sh, axis=0)[SUBLANES:, :] * cw[w:w + 1, :]
    return _silu(acc)


def _gdn_split(conv, h):
    hk = GDN_HEADS * GDN_DK
    return (conv[:, h * GDN_DK:(h + 1) * GDN_DK], conv[:, hk + h * GDN_DK: hk + (h + 1) * GDN_DK],
            conv[:, 2 * hk + h * GDN_DV: 2 * hk + (h + 1) * GDN_DV])


def _gdn_prompt_kernel(x_ref, z_ref, sm_ref, smt_ref, cw_ref, alr_ref, dtr_ref, alc_ref, dtc_ref, ng_ref,
                       y_ref, so_ref, s_ref, halo_ref):
    c = pl.program_id(1)

    @pl.when(c == 0)
    def _():
        s_ref[...] = jnp.zeros_like(s_ref)
        halo_ref[...] = jnp.zeros_like(halo_ref)

    x = x_ref[...]
    rows = x.shape[0]
    xcat = jnp.concatenate([halo_ref[...], x], axis=0)
    halo_ref[...] = x[rows - SUBLANES:, :]
    conv = _gdn_conv(xcat, cw_ref[...])
    cum, cum_t, beta = _gdn_gates(sm_ref[...], smt_ref[...], alr_ref[...], dtr_ref[...],
                                  alc_ref[...], dtc_ref[...], CHUNK)
    hs = range(GDN_HEADS)
    order = [(ch, h) for ch in range(rows // CHUNK) for h in hs]

    def rws(a, ch):
        return a[ch * CHUNK:(ch + 1) * CHUNK]

    qkv = [_gdn_split(rws(conv, ch), h) for ch, h in order]
    prep = _gdn_prep([t[0] for t in qkv], [t[1] for t in qkv], [t[2] for t in qkv],
                     [rws(cum, ch)[:, S_DA + h:S_DA + h + 1] for ch, h in order],
                     [cum_t[S_DA + h:S_DA + h + 1, ch * CHUNK:(ch + 1) * CHUNK] for ch, h in order],
                     [rws(beta, ch)[:, S_DB + h:S_DB + h + 1] for ch, h in order], CHUNK)
    ss = [s_ref[h] for h in hs]
    for ch in range(rows // CHUNK):
        ys, ss = _gdn_seq(prep[ch * GDN_HEADS:(ch + 1) * GDN_HEADS],
                          [z_ref[ch * CHUNK:(ch + 1) * CHUNK, h * GDN_DV:(h + 1) * GDN_DV] for h in hs],
                          ss, ng_ref[...])
        for h in hs:
            y_ref[ch * CHUNK:(ch + 1) * CHUNK, h * GDN_DV:(h + 1) * GDN_DV] = ys[h].astype(BF16)
    for h in hs:
        s_ref[h] = ss[h]

    @pl.when(c == pl.num_programs(1) - 1)
    def _():
        so_ref[...] = s_ref[...]


def _gdn_prompt(p_main, p_small, p_small_t, cw, alr, dtr, alc, dtc, ng, layer, nb, nc):
    zw = GDN_HEADS * GDN_DV
    rb = MIX_CHUNKS * CHUNK
    return pl.pallas_call(
        _gdn_prompt_kernel,
        grid=(nb, nc),
        in_specs=[
            pl.BlockSpec((rb, GDN_QKV), lambda b, c: (b * nc + c, P_DQKV // GDN_QKV)),
            pl.BlockSpec((rb, zw), lambda b, c: (b * nc + c, P_DZ // zw)),
            pl.BlockSpec((rb, LANES), lambda b, c: (b * nc + c, 0)),
            pl.BlockSpec((None, LANES, rb), lambda b, c: (b * nc + c, 0, 0)),
            pl.BlockSpec((None, GDN_CONV, GDN_QKV), lambda b, c: (layer, 0, 0)),
            pl.BlockSpec((None, 1, LANES), lambda b, c: (layer, 0, 0)),
            pl.BlockSpec((None, 1, LANES), lambda b, c: (layer, 0, 0)),
            pl.BlockSpec((None, LANES, 1), lambda b, c: (layer, 0, 0)),
            pl.BlockSpec((None, LANES, 1), lambda b, c: (layer, 0, 0)),
            pl.BlockSpec((None, 1, GDN_DV), lambda b, c: (layer, 0, 0)),
        ],
        out_specs=[
            pl.BlockSpec((rb, zw), lambda b, c: (b * nc + c, 0)),
            pl.BlockSpec((None, GDN_HEADS, GDN_DK, GDN_DV), lambda b, c: (b, 0, 0, 0)),
        ],
        out_shape=[jax.ShapeDtypeStruct((p_main.shape[0], zw), BF16),
                   jax.ShapeDtypeStruct((nb, GDN_HEADS, GDN_DK, GDN_DV), F32)],
        scratch_shapes=[pltpu.VMEM((GDN_HEADS, GDN_DK, GDN_DV), F32), pltpu.VMEM((SUBLANES, GDN_QKV), F32)],
        compiler_params=_cparams(("arbitrary", "arbitrary")),
        name="gdn_prompt",
    )(p_main, p_main, p_small, p_small_t, cw, alr, dtr, alc, dtc, ng)


def _gdn_sample_kernel(x_ref, z_ref, sm_ref, smt_ref, sc_ref, cw_ref, alr_ref, dtr_ref, alc_ref, dtc_ref, ng_ref,
                       si_ref, *rest, lseg):
    y_ref, so_ref = rest[-2:]
    npair = x_ref.shape[0] // SAMPLE_SLAB
    cw = cw_ref[...]
    hs = range(GDN_HEADS)

    seqs = [(u, t) for u in range(SAMPLE_PAIRS) for t in range(2)]
    order = [(u, t, h) for u, t in seqs for h in hs]

    def pairs(it, carry):
        ps = [it * SAMPLE_PAIRS + u for u in range(SAMPLE_PAIRS)]
        rows = [pl.ds(pl.multiple_of(p * SAMPLE_SLAB, SAMPLE_SLAB), SAMPLE_SLAB) for p in ps]
        xs = [_slab_pair(x_ref[r, :]) for r in rows]
        sms = [_slab_pair(sm_ref[r, :]) for r in rows]
        zs = [[_slab_pair(z_ref[r, h * GDN_DV:(h + 1) * GDN_DV]) for h in hs] for r in rows]
        convs, gates = {}, {}
        for u, t in seqs:
            xcat = jnp.concatenate([sc_ref[2 * ps[u] + t], xs[u][t]], axis=0)
            convs[u, t] = _gdn_conv(xcat, cw)
            gates[u, t] = _gdn_gates(sms[u][t], smt_ref[2 * ps[u] + t], alr_ref[...], dtr_ref[...],
                                     alc_ref[...], dtc_ref[...], SAMPLE_SLAB)
        qkv = [_gdn_split(convs[u, t], h) for u, t, h in order]
        prep = _gdn_prep([a[0] for a in qkv], [a[1] for a in qkv], [a[2] for a in qkv],
                         [gates[u, t][0][:, S_DA + h:S_DA + h + 1] for u, t, h in order],
                         [gates[u, t][1][S_DA + h:S_DA + h + 1, :] for u, t, h in order],
                         [gates[u, t][2][:, S_DB + h:S_DB + h + 1] for u, t, h in order], lseg)
        ys, s_new = _gdn_seq(prep, [zs[u][h][t] for u, t, h in order],
                             [si_ref[2 * ps[u] + t, h] for u, t, h in order], ng_ref[...])
        for idx, (u, t, h) in enumerate(order):
            so_ref[2 * ps[u] + t, h] = s_new[idx]
        for u in range(SAMPLE_PAIRS):
            base = u * 2 * GDN_HEADS
            for h in hs:
                y_ref[rows[u], h * GDN_DV:(h + 1) * GDN_DV] = _slab_merge(
                    ys[base + h], ys[base + GDN_HEADS + h]).astype(BF16)
        return carry

    lax.fori_loop(0, npair // SAMPLE_PAIRS, pairs, 0)


def _gdn_sample(p_main, p_small, p_small_t, sconv, cw, alr, dtr, alc, dtc, ng, state, so_prev, y_full, layer, row0,
                nseq, lseg):
    zw = GDN_HEADS * GDN_DV
    rb = SAMPLE_SEQS * lseg
    blk0 = row0 // rb
    st_in, st_specs, st_out_spec, out_shapes, aliases = _sample_io(state, so_prev, y_full, layer, GDN_HEADS,
                                                                   GDN_DK, GDN_DV, 11)
    return pl.pallas_call(
        functools.partial(_gdn_sample_kernel, lseg=lseg),
        grid=(nseq // SAMPLE_SEQS,),
        in_specs=[
            pl.BlockSpec((rb, GDN_QKV), lambda i: (blk0 + i, P_DQKV // GDN_QKV)),
            pl.BlockSpec((rb, zw), lambda i: (blk0 + i, P_DZ // zw)),
            pl.BlockSpec((rb, LANES), lambda i: (blk0 + i, 0)),
            pl.BlockSpec((SAMPLE_SEQS, LANES, SAMPLE_SLAB), lambda i: (i, 0, 0)),
            pl.BlockSpec((SAMPLE_SEQS, SUBLANES, GDN_QKV), lambda i: (i, 0, 0)),
            pl.BlockSpec((None, GDN_CONV, GDN_QKV), lambda i: (layer, 0, 0)),
            pl.BlockSpec((None, 1, LANES), lambda i: (layer, 0, 0)),
            pl.BlockSpec((None, 1, LANES), lambda i: (layer, 0, 0)),
            pl.BlockSpec((None, LANES, 1), lambda i: (layer, 0, 0)),
            pl.BlockSpec((None, LANES, 1), lambda i: (layer, 0, 0)),
            pl.BlockSpec((None, 1, GDN_DV), lambda i: (layer, 0, 0)),
        ] + st_specs,
        out_specs=[pl.BlockSpec((rb, zw), lambda i: (blk0 + i, 0)), st_out_spec],
        out_shape=out_shapes,
        input_output_aliases=aliases,
        compiler_params=_cparams(("arbitrary",)),
        name="gdn_sample",
    )(p_main, p_main, p_small, p_small_t, sconv, cw, alr, dtr, alc, dtc, ng, *st_in)


def _gla_scores(q, k, b, sub):
    c = q.shape[0]
    irow = _iota((sub, 1), 0)
    lane = _iota((sub, c), 1)
    blocks = []
    for blk in range(c // sub):
        r0 = blk * sub
        qi, ki, bi = q[r0:r0 + sub], k[r0:r0 + sub], b[r0:r0 + sub]
        if blk > 0:
            b0 = b[r0 - 1:r0, :]
            qt = qi * jnp.exp(bi - b0)
            kt = k[:r0] * jnp.exp(b0 - b[:r0])
            kt = jnp.concatenate([kt, jnp.zeros((c - r0, kt.shape[1]), F32)], axis=0)
            a = _dot_nt(qt, kt)
        else:
            a = jnp.zeros((sub, c), F32)
        for jj in range(sub):
            e = jnp.exp(jnp.where(irow >= jj, bi - bi[jj:jj + 1, :], -jnp.inf))
            col = jnp.sum(qi * e * ki[jj:jj + 1, :], axis=1, keepdims=True)
            a = jnp.where(lane == r0 + jj, col, a)
        blocks.append(a)
    return jnp.concatenate(blocks, axis=0) if len(blocks) > 1 else blocks[0]


def _gla_prep(qs, ks, vs, bs, nvalid, sub):
    n = range(len(qs))
    c = qs[0].shape[0]
    rowv = _iota((c, 1), 0) < nvalid
    qsc = [q * (GLA_DK ** -0.5) for q in qs]
    b_last = [bs[p][nvalid - 1:nvalid, :] for p in n]
    kv = [_dot_tn(ks[p] * jnp.where(rowv, jnp.exp(b_last[p] - bs[p]), 0.0), vs[p]) for p in n]
    a = [_gla_scores(qsc[p], ks[p], bs[p], sub) for p in n]
    o_in = [_dot(a[p], vs[p]) for p in n]
    return [((qsc[p] * jnp.exp(bs[p])).astype(BF16), o_in[p], kv[p], _row_to_col(jnp.exp(b_last[p]))) for p in n]


def _gla_seq(prep, zs, ss, norm_g):
    n = range(len(prep))
    o = [prep[p][1] + _mm(prep[p][0], ss[p].astype(BF16)) for p in n]
    s_new = [prep[p][3] * ss[p] + prep[p][2] for p in n]
    ys = [_rms(o[p]) * norm_g * _silu(zs[p]) for p in n]
    return ys, s_new


def _gla_cumsum(gk, c):
    r = gk.shape[0]
    i = _iota((r, r), 0)
    j = _iota((r, r), 1)
    sh = c.bit_length() - 1
    same = ((i >> sh) == (j >> sh)) if r > c else True
    return _dot_mask_l((same & (j <= i)).astype(F32), gk)


def _gla_gk(small, wup, bup):
    return jax.nn.log_sigmoid(_mm(small.astype(BF16), wup) + bup) / GLA_GATE_NORM


def _gla_prompt_kernel(q_ref, k_ref, v_ref, z_ref, sm_ref, wup_ref, bup_ref, ng_ref, y_ref, so_ref, s_ref):
    c = pl.program_id(1)

    @pl.when(c == 0)
    def _():
        s_ref[...] = jnp.zeros_like(s_ref)

    rows = q_ref.shape[0]
    b_all = _gla_cumsum(_gla_gk(sm_ref[...], wup_ref[...], bup_ref[...]), CHUNK)
    hs = range(GLA_HEADS)
    order = [(ch, h) for ch in range(rows // CHUNK) for h in hs]

    def blk(ref, ch, h, w):
        return ref[ch * CHUNK:(ch + 1) * CHUNK, h * w:(h + 1) * w]

    prep = _gla_prep([blk(q_ref, ch, h, GLA_DK) for ch, h in order], [blk(k_ref, ch, h, GLA_DK) for ch, h in order],
                     [blk(v_ref, ch, h, GLA_DV) for ch, h in order], [blk(b_all, ch, h, GLA_DK) for ch, h in order],
                     CHUNK, GLA_SUB)
    ss = [s_ref[h] for h in hs]
    for ch in range(rows // CHUNK):
        ys, ss = _gla_seq(prep[ch * GLA_HEADS:(ch + 1) * GLA_HEADS], [blk(z_ref, ch, h, GLA_DV) for h in hs],
                          ss, ng_ref[...])
        for h in hs:
            y_ref[ch * CHUNK:(ch + 1) * CHUNK, h * GLA_DV:(h + 1) * GLA_DV] = ys[h].astype(BF16)
    for h in hs:
        s_ref[h] = ss[h]

    @pl.when(c == pl.num_programs(1) - 1)
    def _():
        so_ref[...] = s_ref[...]


def _gla_prompt(p_main, p_small, wup, bup, ng, layer, nb, nc):
    qw = GLA_HEADS * GLA_DK
    vw = GLA_HEADS * GLA_DV
    rb = MIX_CHUNKS * CHUNK
    return pl.pallas_call(
        _gla_prompt_kernel,
        grid=(nb, nc),
        in_specs=[
            pl.BlockSpec((rb, qw), lambda b, c: (b * nc + c, P_LQ // qw)),
            pl.BlockSpec((rb, qw), lambda b, c: (b * nc + c, P_LK // qw)),
            pl.BlockSpec((rb, vw), lambda b, c: (b * nc + c, P_LV // vw)),
            pl.BlockSpec((rb, vw), lambda b, c: (b * nc + c, P_LGT // vw)),
            pl.BlockSpec((rb, LANES), lambda b, c: (b * nc + c, 0)),
            pl.BlockSpec((None, LANES, qw), lambda b, c: (layer, 0, 0)),
            pl.BlockSpec((None, 1, qw), lambda b, c: (layer, 0, 0)),
            pl.BlockSpec((None, 1, GLA_DV), lambda b, c: (layer, 0, 0)),
        ],
        out_specs=[
            pl.BlockSpec((rb, vw), lambda b, c: (b * nc + c, 0)),
            pl.BlockSpec((None, GLA_HEADS, GLA_DK, GLA_DV), lambda b, c: (b, 0, 0, 0)),
        ],
        out_shape=[jax.ShapeDtypeStruct((p_main.shape[0], vw), BF16),
                   jax.ShapeDtypeStruct((nb, GLA_HEADS, GLA_DK, GLA_DV), F32)],
        scratch_shapes=[pltpu.VMEM((GLA_HEADS, GLA_DK, GLA_DV), F32)],
        compiler_params=_cparams(("arbitrary", "arbitrary")),
        name="gla_prompt",
    )(p_main, p_main, p_main, p_main, p_small, wup, bup, ng)


def _gla_sample_kernel(q_ref, k_ref, v_ref, z_ref, sm_ref, wup_ref, bup_ref, ng_ref, si_ref, *rest, lseg):
    y_ref, so_ref = rest[-2:]
    npair = q_ref.shape[0] // SAMPLE_SLAB
    hs = range(GLA_HEADS)

    order = [(u, t, h) for u in range(SAMPLE_PAIRS) for t in range(2) for h in hs]

    def pairs(it, carry):
        ps = [it * SAMPLE_PAIRS + u for u in range(SAMPLE_PAIRS)]
        rows = [pl.ds(pl.multiple_of(p * SAMPLE_SLAB, SAMPLE_SLAB), SAMPLE_SLAB) for p in ps]

        def slabs(ref, w):
            return [[_slab_pair(ref[r, h * w:(h + 1) * w]) for h in hs] for r in rows]

        qs, ks, vs, zs = slabs(q_ref, GLA_DK), slabs(k_ref, GLA_DK), slabs(v_ref, GLA_DV), slabs(z_ref, GLA_DV)
        bs = [[_gla_cumsum(g, SAMPLE_SLAB) for g in _slab_pair(_gla_gk(sm_ref[r, :], wup_ref[...], bup_ref[...]))]
              for r in rows]
        prep = _gla_prep([qs[u][h][t] for u, t, h in order], [ks[u][h][t] for u, t, h in order],
                         [vs[u][h][t] for u, t, h in order],
                         [bs[u][t][:, h * GLA_DK:(h + 1) * GLA_DK] for u, t, h in order], lseg, SAMPLE_SLAB)
        ys, s_new = _gla_seq(prep, [zs[u][h][t] for u, t, h in order],
                             [si_ref[2 * ps[u] + t, h] for u, t, h in order], ng_ref[...])
        for idx, (u, t, h) in enumerate(order):
            so_ref[2 * ps[u] + t, h] = s_new[idx]
        for u in range(SAMPLE_PAIRS):
            base = u * 2 * GLA_HEADS
            for h in hs:
                y_ref[rows[u], h * GLA_DV:(h + 1) * GLA_DV] = _slab_merge(
                    ys[base + h], ys[base + GLA_HEADS + h]).astype(BF16)
        return carry

    lax.fori_loop(0, npair // SAMPLE_PAIRS, pairs, 0)


def _gla_sample(p_main, p_small, wup, bup, ng, state, so_prev, y_full, layer, row0, nseq, lseg):
    qw = GLA_HEADS * GLA_DK
    vw = GLA_HEADS * GLA_DV
    rb = SAMPLE_SEQS * lseg
    blk0 = row0 // rb
    st_in, st_specs, st_out_spec, out_shapes, aliases = _sample_io(state, so_prev, y_full, layer, GLA_HEADS,
                                                                   GLA_DK, GLA_DV, 8)
    return pl.pallas_call(
        functools.partial(_gla_sample_kernel, lseg=lseg),
        grid=(nseq // SAMPLE_SEQS,),
        in_specs=[
            pl.BlockSpec((rb, qw), lambda i: (blk0 + i, P_LQ // qw)),
            pl.BlockSpec((rb, qw), lambda i: (blk0 + i, P_LK // qw)),
            pl.BlockSpec((rb, vw), lambda i: (blk0 + i, P_LV // vw)),
            pl.BlockSpec((rb, vw), lambda i: (blk0 + i, P_LGT // vw)),
            pl.BlockSpec((rb, LANES), lambda i: (blk0 + i, 0)),
            pl.BlockSpec((None, LANES, qw), lambda i: (layer, 0, 0)),
            pl.BlockSpec((None, 1, qw), lambda i: (layer, 0, 0)),
            pl.BlockSpec((None, 1, GLA_DV), lambda i: (layer, 0, 0)),
        ] + st_specs,
        out_specs=[pl.BlockSpec((rb, vw), lambda i: (blk0 + i, 0)), st_out_spec],
        out_shape=out_shapes,
        input_output_aliases=aliases,
        compiler_params=_cparams(("arbitrary",)),
        name="gla_sample",
    )(p_main, p_main, p_main, p_main, p_small, wup, bup, ng, *st_in)


def _rope_tables(pos0, length):
    inv = 1.0 / (ROPE_BASE ** jnp.linspace(0.0, 1.0, RET_DK // 2, dtype=F32))
    ang = (jnp.arange(length, dtype=F32) + pos0)[:, None] * inv[None, :]
    cos = jnp.repeat(jnp.cos(ang), 2, axis=1)
    sin = jnp.stack([-jnp.sin(ang), jnp.sin(ang)], axis=-1).reshape(length, RET_DK)
    return cos, sin


def _lane_pad(v, off):
    n = v.shape[-1]
    return jnp.pad(v, ((0, 0), (off, LANES - off - n)))[:, None, :]


def kernel(x_prompt, x_sample, state_ret, state_gdn, state_gdn_conv, state_gla, norm_mix, norm_ffn, norm_final,
           w_in, b_merge, gdn_conv_w, gdn_a_log, gdn_dt_bias, gdn_norm, gla_w_up, gla_b_up, gla_norm, w_branch,
           w_o, w_gate_up, w_down):
    nb, seq, d = x_prompt.shape
    nsq, lseg, _ = x_sample.shape
    depth = w_in.shape[0]
    n_p = nb * seq
    n_s = nsq * lseg
    mix_rows = MIX_CHUNKS * CHUNK
    nc = seq // mix_rows
    nc_ret = seq // RET_CHUNK
    tm = _dense_tiles(n_p + n_s)
    tm_out = math.gcd(n_p, n_s)
    assert seq % mix_rows == 0 and seq % RET_CHUNK == 0
    assert nsq % SAMPLE_SEQS == 0 and n_p % (SAMPLE_SEQS * lseg) == 0
    assert 2 * lseg == SAMPLE_SLAB and lseg >= GDN_CONV - 1

    w_t = jnp.swapaxes(w_in, 1, 2)
    w_main = _pack_w_in(w_t, 1024, 1024)
    w_small = _pack_w_small(w_t, 512)
    wb = w_branch.astype(BF16)
    wo = w_o.astype(BF16)
    wgu = w_gate_up.astype(BF16)
    wdn = w_down.astype(BF16)
    g_mix = norm_mix[:, None, :]
    g_ffn = norm_ffn[:, None, :]
    bm = b_merge.reshape(depth, N_BRANCH, 1, D_MODEL)
    alr = _lane_pad(gdn_a_log, S_DA)
    dtr = _lane_pad(gdn_dt_bias, S_DA)
    alc = jnp.swapaxes(alr, 1, 2)
    dtc = jnp.swapaxes(dtr, 1, 2)
    gdn_ng = gdn_norm[:, None, :]
    gla_ng = gla_norm[:, None, :]
    wup = jnp.pad(gla_w_up, ((0, 0), (S_LLR, LANES - S_LLR - GLA_LOWRANK), (0, 0))).astype(BF16)
    bup = gla_b_up[:, None, :]
    cos_p, sin_p = _rope_tables(0.0, seq)
    cos_s, sin_s = _rope_tables(float(PAST_LEN), lseg)
    cos_s = jnp.tile(cos_s, (SAMPLE_SLAB // lseg, 1))
    sin_s = jnp.tile(sin_s, (SAMPLE_SLAB // lseg, 1))

    x = jnp.concatenate([x_prompt.reshape(n_p, d), x_sample.reshape(n_s, d)], axis=0)

    outs = {k: [] for k in ("p_ret", "p_gdn", "p_conv", "p_gla", "s_conv")}
    s_ret = s_gdn = s_gla = None
    for layer in range(depth):
        p_main, p_gate, p_small = _in_proj(x, g_mix, w_main, w_small, layer, tm, 1024)
        pst_p = jnp.swapaxes(p_small[:n_p].reshape(nb * nc, mix_rows, LANES), 1, 2)
        pst_s = jnp.swapaxes(jnp.pad(p_small[n_p:].reshape(nsq, lseg, LANES),
                                     ((0, 0), (0, SAMPLE_SLAB - lseg), (0, 0))), 1, 2)

        y_ret, st = _ret_prompt(p_main, cos_p, sin_p, nb, nc_ret)
        outs["p_ret"].append(st)
        y_ret, s_ret = _ret_sample(p_main, cos_s, sin_s, state_ret, s_ret, y_ret, layer, n_p, nsq, lseg)

        y_gdn, st = _gdn_prompt(p_main, p_small, pst_p, gdn_conv_w, alr, dtr, alc, dtc, gdn_ng, layer, nb, nc)
        outs["p_gdn"].append(st)
        sconv = jnp.pad(state_gdn_conv[layer], ((0, 0), (SUBLANES - (GDN_CONV - 1), 0), (0, 0)))
        y_gdn, s_gdn = _gdn_sample(p_main, p_small, pst_s, sconv, gdn_conv_w, alr, dtr, alc, dtc, gdn_ng,
                                   state_gdn, s_gdn, y_gdn, layer, n_p, nsq, lseg)
        outs["p_conv"].append(jnp.stack([
            lax.slice(p_main, ((b + 1) * seq - (GDN_CONV - 1), P_DQKV), ((b + 1) * seq, P_DQKV + GDN_QKV))
            for b in range(nb)]))
        dq_s = lax.slice(p_main, (n_p, P_DQKV), (n_p + n_s, P_DQKV + GDN_QKV)).reshape(nsq, lseg, GDN_QKV)
        outs["s_conv"].append(jnp.concatenate([state_gdn_conv[layer], dq_s], axis=1)[:, -(GDN_CONV - 1):])

        y_gla, st = _gla_prompt(p_main, p_small, wup, bup, gla_ng, layer, nb, nc)
        outs["p_gla"].append(st)
        y_gla, s_gla = _gla_sample(p_main, p_small, wup, bup, gla_ng, state_gla, s_gla, y_gla, layer, n_p, nsq, lseg)

        mrg = _merge((y_ret, y_gdn, y_gla), wb, p_gate, bm, layer, tm, 512)
        x = _out_proj(mrg, wo, x, layer, tm, 1024)
        x = _ffn(x, g_ffn, wgu, wdn, layer, tm, 512)

    g_fin = norm_final[None, :]
    y_p = _final_norm(x, g_fin, 0, n_p, tm_out)
    y_s = _final_norm(x, g_fin, n_p, n_s, tm_out)
    st = {k: jnp.stack(v) for k, v in outs.items()}
    return (y_p.reshape(nb, seq, d), y_s.reshape(nsq, lseg, d),
            st["p_ret"], st["p_gdn"], st["p_conv"], st["p_gla"],
            s_ret, s_gdn, st["s_conv"], s_gla)
```

```python
import functools
import math

import jax
import jax.numpy as jnp
from jax import lax
from jax.experimental import pallas as pl
from jax.experimental.pallas import tpu as pltpu

F32 = jnp.float32
BF16 = jnp.bfloat16

D_MODEL = 2048
RET_HEADS, RET_DK, RET_DV = 4, 128, 256
GDN_HEADS, GDN_DK, GDN_DV, GDN_CONV = 8, 128, 128, 4
GDN_QKV = GDN_HEADS * (2 * GDN_DK + GDN_DV)
GLA_HEADS, GLA_DK, GLA_DV, GLA_LOWRANK = 4, 128, 256, 16
GLA_GATE_NORM = 16.0
N_BRANCH = 3
BRANCH_WIDTH = 1024
EPS = 1e-6
ROPE_BASE = 10000.0
PAST_LEN = 16384

LANES = 128
SUBLANES = 8
CHUNK = 64
MIX_CHUNKS = 4
RET_CHUNK = 256
GLA_SUB = 8
SAMPLE_SLAB = 8
SAMPLE_SEQS = 16
SAMPLE_PAIRS = 4
VMEM_LIMIT = 56 * 1024 * 1024

W_SEGS = ((0, 6144), (6160, 9232), (9248, 16416))
W_SMALL = ((6144, 6160), (9232, 9248))
P_RQ, P_RK, P_RV, P_RG = 0, 512, 1024, 2048
P_DQKV, P_DZ = 3072, 6144
P_LQ, P_LK, P_LV, P_LGT, P_MG = 7168, 7680, 8192, 9216, 10240
P_MAIN = 16384
S_DA, S_DB, S_LLR = 0, 8, 16


def _cparams(sem):
    return pltpu.CompilerParams(dimension_semantics=sem, vmem_limit_bytes=VMEM_LIMIT)


def _dense_tiles(m):
    best = 64
    for t in range(64, 1089, 64):
        if m % t == 0:
            best = t
    return best


_mm = functools.partial(jnp.dot, preferred_element_type=F32)


def _dot(a, b):
    return _mm(a.astype(BF16), b.astype(BF16))


def _dot_nt(a, b):
    return lax.dot_general(a.astype(BF16), b.astype(BF16), (((1,), (1,)), ((), ())),
                           preferred_element_type=F32)


def _dot_tn(a, b):
    return lax.dot_general(a.astype(BF16), b.astype(BF16), (((0,), (0,)), ((), ())),
                           preferred_element_type=F32)


def _split3(x):
    hi = x.astype(BF16)
    r = x - hi.astype(F32)
    mid = r.astype(BF16)
    lo = (r - mid.astype(F32)).astype(BF16)
    return hi, mid, lo


def _dot_mask_l(m, x):
    mb = m.astype(BF16)
    hi, mid, lo = _split3(x)
    return _mm(mb, hi) + (_mm(mb, mid) + _mm(mb, lo))


def _dot_mask_r(x, m):
    mb = m.astype(BF16)
    hi, mid, lo = _split3(x)
    return _mm(hi, mb) + (_mm(mid, mb) + _mm(lo, mb))


def _iota(shape, dim):
    return lax.broadcasted_iota(jnp.int32, shape, dim)


def _silu(x):
    return x * jax.nn.sigmoid(x)


def _rms(x, eps=EPS):
    return x * lax.rsqrt(jnp.mean(x * x, axis=-1, keepdims=True) + eps)


def _row_to_col(r):
    n = r.shape[1]
    eye = _iota((n, n), 0) == _iota((n, n), 1)
    return jnp.sum(jnp.where(eye, jnp.broadcast_to(r, (n, n)), 0.0), axis=1, keepdims=True)


def _tril_incl(c):
    return (_iota((c, c), 1) <= _iota((c, c), 0)).astype(F32)


PACK_TAIL = 32


def _pack_kernel(a_ref, b_ref, o_ref, *, edges):
    j = pl.program_id(1)
    lo = 0
    for shift, hi in edges:
        def _(shift=shift):
            src = a_ref[...]
            if shift:
                src = jnp.concatenate([src[shift:], b_ref[:shift]], axis=0)
            o_ref[...] = src.T.astype(BF16)
        pl.when((j >= lo) & (j < hi))(_)
        lo = hi


def _pack_w_in(w_t, tk, tn):
    depth, _, d = w_t.shape
    edges, out0 = [], 0
    for a, b in W_SEGS:
        assert out0 % tn == 0 and (b - a) % tn == 0 and 0 <= a - out0 <= PACK_TAIL and (a - out0) % SUBLANES == 0
        edges.append((a - out0, (out0 + b - a) // tn))
        out0 += b - a
    assert out0 == P_MAIN
    return pl.pallas_call(
        functools.partial(_pack_kernel, edges=tuple(edges)),
        grid=(depth, P_MAIN // tn, d // tk),
        in_specs=[
            pl.BlockSpec((None, tn, tk), lambda l, j, k: (l, j, k)),
            pl.BlockSpec((None, PACK_TAIL, tk), lambda l, j, k: (l, (j + 1) * (tn // PACK_TAIL), k)),
        ],
        out_specs=pl.BlockSpec((None, tk, tn), lambda l, j, k: (l, k, j)),
        out_shape=jax.ShapeDtypeStruct((depth, d, P_MAIN), BF16),
        compiler_params=_cparams(("arbitrary", "arbitrary", "arbitrary")),
        name="pack_w_in",
    )(w_t, w_t)


def _pack_small_kernel(a_ref, b_ref, o_ref):
    rows = jnp.concatenate([a_ref[...], b_ref[...],
                            jnp.zeros((LANES - a_ref.shape[0] - b_ref.shape[0], a_ref.shape[1]), F32)], axis=0)
    o_ref[...] = rows.T.astype(BF16)


def _pack_w_small(w_t, tk):
    depth, _, d = w_t.shape
    (a0, a1), (b0, b1) = W_SMALL
    na, nb = a1 - a0, b1 - b0
    assert a0 % na == 0 and b0 % nb == 0 and na % SUBLANES == 0 and nb % SUBLANES == 0
    return pl.pallas_call(
        _pack_small_kernel,
        grid=(depth, d // tk),
        in_specs=[pl.BlockSpec((None, na, tk), lambda l, k: (l, a0 // na, k)),
                  pl.BlockSpec((None, nb, tk), lambda l, k: (l, b0 // nb, k))],
        out_specs=pl.BlockSpec((None, tk, LANES), lambda l, k: (l, k, 0)),
        out_shape=jax.ShapeDtypeStruct((depth, d, LANES), BF16),
        compiler_params=_cparams(("arbitrary", "arbitrary")),
        name="pack_w_small",
    )(w_t, w_t)


def _in_proj_kernel(x_ref, g_ref, w_ref, ws_ref, p_ref, pg_ref, ps_ref, h_ref, *, n_f32):
    j = pl.program_id(1)

    @pl.when(j == 0)
    def _():
        h = (_rms(x_ref[...]) * g_ref[...]).astype(BF16)
        h_ref[...] = h
        ps_ref[...] = _mm(h, ws_ref[...])

    @pl.when(j < n_f32)
    def _():
        p_ref[...] = _mm(h_ref[...], w_ref[...])

    @pl.when(j >= n_f32)
    def _():
        pg_ref[...] = _mm(h_ref[...], w_ref[...]).astype(BF16)


def _in_proj(x, g, w_main, w_small, layer, tm, tn):
    m, d = x.shape
    n = w_main.shape[2]
    n_f32 = P_MG // tn
    assert P_MG % tn == 0 and (n - P_MG) % tn == 0
    return pl.pallas_call(
        functools.partial(_in_proj_kernel, n_f32=n_f32),
        grid=(m // tm, n // tn),
        in_specs=[
            pl.BlockSpec((tm, d), lambda i, j: (i, 0)),
            pl.BlockSpec((None, 1, d), lambda i, j: (layer, 0, 0)),
            pl.BlockSpec((None, d, tn), lambda i, j: (layer, 0, j)),
            pl.BlockSpec((None, d, LANES), lambda i, j: (layer, 0, 0)),
        ],
        out_specs=[
            pl.BlockSpec((tm, tn), lambda i, j: (i, jnp.minimum(j, n_f32 - 1))),
            pl.BlockSpec((tm, tn), lambda i, j: (i, jnp.maximum(j - n_f32, 0))),
            pl.BlockSpec((tm, LANES), lambda i, j: (i, 0)),
        ],
        out_shape=[jax.ShapeDtypeStruct((m, P_MG), F32), jax.ShapeDtypeStruct((m, n - P_MG), BF16),
                   jax.ShapeDtypeStruct((m, LANES), F32)],
        scratch_shapes=[pltpu.VMEM((tm, d), BF16)],
        compiler_params=_cparams(("arbitrary", "arbitrary")),
        name="in_proj",
    )(x, g, w_main, w_small)


def _merge_kernel(y0_ref, y1_ref, y2_ref, w0_ref, w1_ref, w2_ref, g0_ref, g1_ref, g2_ref,
                  b0_ref, b1_ref, b2_ref, o_ref):
    acc = None
    for y_ref, w_ref, g_ref, b_ref in ((y0_ref, w0_ref, g0_ref, b0_ref), (y1_ref, w1_ref, g1_ref, b1_ref),
                                       (y2_ref, w2_ref, g2_ref, b2_ref)):
        t = jax.nn.sigmoid(g_ref[...].astype(F32) + b_ref[...]) * _mm(y_ref[...], w_ref[...])
        acc = t if acc is None else acc + t
    o_ref[...] = acc.astype(BF16)


def _merge(ys, wb, p_gate, b_merge, layer, tm, tn):
    m, bw = ys[0].shape
    d = wb.shape[3]
    nj = d // tn
    mg0 = 0

    def y_spec():
        return pl.BlockSpec((tm, bw), lambda i, j: (i, 0))

    def w_spec(b):
        return pl.BlockSpec((None, None, bw, tn), lambda i, j: (layer, b, 0, j))

    def g_spec(b):
        return pl.BlockSpec((tm, tn), lambda i, j: (i, mg0 + b * nj + j))

    def b_spec(b):
        return pl.BlockSpec((None, None, 1, tn), lambda i, j: (layer, b, 0, j))

    rng = range(N_BRANCH)
    return pl.pallas_call(
        _merge_kernel,
        grid=(m // tm, nj),
        in_specs=[y_spec() for _ in rng] + [w_spec(b) for b in rng] + [g_spec(b) for b in rng]
        + [b_spec(b) for b in rng],
        out_specs=pl.BlockSpec((tm, tn), lambda i, j: (i, j)),
        out_shape=jax.ShapeDtypeStruct((m, d), BF16),
        compiler_params=_cparams(("arbitrary", "arbitrary")),
        name="merge",
    )(*ys, wb, wb, wb, p_gate, p_gate, p_gate, b_merge, b_merge, b_merge)


def _out_proj_kernel(m_ref, w_ref, x_ref, o_ref):
    o_ref[...] = x_ref[...] + _mm(m_ref[...], w_ref[...])


def _out_proj(mrg, w_o, x, layer, tm, tn):
    m, d = x.shape
    return pl.pallas_call(
        _out_proj_kernel,
        grid=(m // tm, d // tn),
        in_specs=[
            pl.BlockSpec((tm, d), lambda i, j: (i, 0)),
            pl.BlockSpec((None, d, tn), lambda i, j: (layer, 0, j)),
            pl.BlockSpec((tm, tn), lambda i, j: (i, j)),
        ],
        out_specs=pl.BlockSpec((tm, tn), lambda i, j: (i, j)),
        out_shape=jax.ShapeDtypeStruct((m, d), F32),
        compiler_params=_cparams(("arbitrary", "arbitrary")),
        name="out_proj",
    )(mrg, w_o, x)


def _ffn_kernel(x_ref, g_ref, wg_ref, wu_ref, wd_ref, o_ref, h_ref):
    j = pl.program_id(1)

    @pl.when(j == 0)
    def _():
        x = x_ref[...]
        h_ref[...] = (_rms(x) * g_ref[...]).astype(BF16)
        o_ref[...] = x

    h = h_ref[...]
    act = (_silu(_mm(h, wg_ref[...])) * _mm(h, wu_ref[...])).astype(BF16)
    o_ref[...] += _mm(act, wd_ref[...])


def _ffn(x, g, w_gu, w_dn, layer, tm, tf):
    m, d = x.shape
    nf = w_dn.shape[1] // tf
    return pl.pallas_call(
        _ffn_kernel,
        grid=(m // tm, nf),
        in_specs=[
            pl.BlockSpec((tm, d), lambda i, j: (i, 0), pipeline_mode=pl.Buffered(1)),
            pl.BlockSpec((None, 1, d), lambda i, j: (layer, 0, 0)),
            pl.BlockSpec((None, d, tf), lambda i, j: (layer, 0, j)),
            pl.BlockSpec((None, d, tf), lambda i, j: (layer, 0, nf + j)),
            pl.BlockSpec((None, tf, d), lambda i, j: (layer, j, 0)),
        ],
        out_specs=pl.BlockSpec((tm, d), lambda i, j: (i, 0)),
        out_shape=jax.ShapeDtypeStruct((m, d), F32),
        scratch_shapes=[pltpu.VMEM((tm, d), BF16)],
        compiler_params=_cparams(("arbitrary", "arbitrary")),
        name="ffn",
    )(x, g, w_gu, w_gu, w_dn)


def _final_norm_kernel(x_ref, g_ref, o_ref):
    o_ref[...] = _rms(x_ref[...]) * g_ref[...]


def _final_norm(x, g, row0, rows, tm):
    d = x.shape[1]
    blk0 = row0 // tm
    return pl.pallas_call(
        _final_norm_kernel,
        grid=(rows // tm,),
        in_specs=[pl.BlockSpec((tm, d), lambda i: (blk0 + i, 0)), pl.BlockSpec((1, d), lambda i: (0, 0))],
        out_specs=pl.BlockSpec((tm, d), lambda i: (i, 0)),
        out_shape=jax.ShapeDtypeStruct((rows, d), F32),
        compiler_params=_cparams(("arbitrary",)),
        name="final_norm",
    )(x, g)


def _slab_pair(x):
    return x, pltpu.roll(x, SAMPLE_SLAB // 2, axis=0)


def _slab_merge(ya, yb):
    top = _iota(ya.shape, 0) < SAMPLE_SLAB // 2
    return jnp.where(top, ya, pltpu.roll(yb, SAMPLE_SLAB // 2, axis=0))


def _sample_io(state, so_prev, y_full, layer, heads, dk, dv, n_fixed):
    blk = pl.BlockSpec((None, SAMPLE_SEQS, heads, dk, dv), lambda i: (layer, i, 0, 0, 0))
    ins, specs = [state, y_full], [blk, pl.BlockSpec(memory_space=pl.ANY)]
    aliases = {n_fixed + 1: 0}
    if so_prev is not None:
        ins.append(so_prev)
        specs.append(pl.BlockSpec(memory_space=pl.ANY))
        aliases[n_fixed + 2] = 1
    out_shapes = [jax.ShapeDtypeStruct(y_full.shape, y_full.dtype), jax.ShapeDtypeStruct(state.shape, F32)]
    return ins, specs, blk, out_shapes, aliases


def _rotary(t, cos, sin_signed):
    even = (_iota(t.shape, 1) & 1) == 0
    nxt = pltpu.roll(t, LANES - 1, axis=1)
    prv = pltpu.roll(t, 1, axis=1)
    return t * cos + jnp.where(even, nxt, prv) * sin_signed


def _ret_decays(c, nvalid, lg):
    i = _iota((c, c), 0)
    j = _iota((c, c), 1)
    diff = i - j
    d_intra = jnp.where(diff >= 0, jnp.exp(lg * jnp.maximum(diff, 0).astype(F32)), 0.0)
    pos_i = _iota((c, 1), 0)
    pos = pos_i.astype(F32)
    d_q = jnp.exp(lg * (pos + 1.0))
    d_k = jnp.where(pos_i < nvalid, jnp.exp(lg * (nvalid - 1.0 - pos)), 0.0)
    d_c = math.exp(lg * nvalid)
    return d_intra, d_q, d_k, d_c


def _ret_lg(h):
    return math.log1p(-(2.0 ** (-5.0 - h)))


def _ret_chunks(qs, ks, vs, gs, ss, cos, sin, decs):
    n = range(len(qs))
    qr = [_rotary(q, cos, sin) for q in qs]
    kr = [_rotary(k, cos, sin) * (RET_DK ** -0.5) for k in ks]
    sc = [_dot_nt(qr[p], kr[p]) * decs[p][0] for p in n]
    o_st = [_dot(qr[p] * decs[p][1], ss[p]) for p in n]
    kv = [_dot_tn(kr[p] * decs[p][2], vs[p]) for p in n]
    o = [_dot(sc[p], vs[p]) + o_st[p] for p in n]
    s_new = [decs[p][3] * ss[p] + kv[p] for p in n]
    ys = [_rms(o[p]) * _silu(gs[p]) for p in n]
    return ys, s_new


def _ret_prompt_kernel(q_ref, k_ref, v_ref, g_ref, cos_ref, sin_ref, y_ref, so_ref, s_ref):
    c = pl.program_id(1)

    @pl.when(c == 0)
    def _():
        s_ref[...] = jnp.zeros_like(s_ref)

    hs = range(RET_HEADS)
    decs = [_ret_decays(RET_CHUNK, RET_CHUNK, _ret_lg(h)) for h in hs]
    ys, s_new = _ret_chunks([q_ref[:, h * RET_DK:(h + 1) * RET_DK] for h in hs],
                            [k_ref[:, h * RET_DK:(h + 1) * RET_DK] for h in hs],
                            [v_ref[:, h * RET_DV:(h + 1) * RET_DV] for h in hs],
                            [g_ref[:, h * RET_DV:(h + 1) * RET_DV] for h in hs],
                            [s_ref[h] for h in hs], cos_ref[...], sin_ref[...], decs)
    for h in hs:
        y_ref[:, h * RET_DV:(h + 1) * RET_DV] = ys[h].astype(BF16)
        s_ref[h] = s_new[h]

    @pl.when(c == pl.num_programs(1) - 1)
    def _():
        so_ref[...] = s_ref[...]


def _ret_prompt(p_main, cos, sin, nb, nc):
    qw = RET_HEADS * RET_DK
    vw = RET_HEADS * RET_DV
    return pl.pallas_call(
        _ret_prompt_kernel,
        grid=(nb, nc),
        in_specs=[
            pl.BlockSpec((RET_CHUNK, qw), lambda b, c: (b * nc + c, P_RQ // qw)),
            pl.BlockSpec((RET_CHUNK, qw), lambda b, c: (b * nc + c, P_RK // qw)),
            pl.BlockSpec((RET_CHUNK, vw), lambda b, c: (b * nc + c, P_RV // vw)),
            pl.BlockSpec((RET_CHUNK, vw), lambda b, c: (b * nc + c, P_RG // vw)),
            pl.BlockSpec((RET_CHUNK, LANES), lambda b, c: (c, 0)),
            pl.BlockSpec((RET_CHUNK, LANES), lambda b, c: (c, 0)),
        ],
        out_specs=[
            pl.BlockSpec((RET_CHUNK, vw), lambda b, c: (b * nc + c, 0)),
            pl.BlockSpec((None, RET_HEADS, RET_DK, RET_DV), lambda b, c: (b, 0, 0, 0)),
        ],
        out_shape=[jax.ShapeDtypeStruct((p_main.shape[0], vw), BF16),
                   jax.ShapeDtypeStruct((nb, RET_HEADS, RET_DK, RET_DV), F32)],
        scratch_shapes=[pltpu.VMEM((RET_HEADS, RET_DK, RET_DV), F32)],
        compiler_params=_cparams(("arbitrary", "arbitrary")),
        name="ret_prompt",
    )(p_main, p_main, p_main, p_main, cos, sin)


def _ret_sample_kernel(q_ref, k_ref, v_ref, g_ref, cos_ref, sin_ref, si_ref, *rest, lseg):
    y_ref, so_ref = rest[-2:]
    npair = q_ref.shape[0] // SAMPLE_SLAB
    cos = cos_ref[...]
    sin = sin_ref[...]
    hs = range(RET_HEADS)
    decs = [_ret_decays(SAMPLE_SLAB, lseg, _ret_lg(h)) for h in hs] * (2 * SAMPLE_PAIRS)
    order = [(u, t, h) for u in range(SAMPLE_PAIRS) for t in range(2) for h in hs]

    def pairs(it, carry):
        ps = [it * SAMPLE_PAIRS + u for u in range(SAMPLE_PAIRS)]
        rows = [pl.ds(pl.multiple_of(p * SAMPLE_SLAB, SAMPLE_SLAB), SAMPLE_SLAB) for p in ps]

        def slabs(ref, w):
            return [[_slab_pair(ref[r, h * w:(h + 1) * w]) for h in hs] for r in rows]

        qs, ks, vs, gs = slabs(q_ref, RET_DK), slabs(k_ref, RET_DK), slabs(v_ref, RET_DV), slabs(g_ref, RET_DV)
        ys, s_new = _ret_chunks([qs[u][h][t] for u, t, h in order], [ks[u][h][t] for u, t, h in order],
                                [vs[u][h][t] for u, t, h in order], [gs[u][h][t] for u, t, h in order],
                                [si_ref[2 * ps[u] + t, h] for u, t, h in order], cos, sin, decs)
        for idx, (u, t, h) in enumerate(order):
            so_ref[2 * ps[u] + t, h] = s_new[idx]
        for u in range(SAMPLE_PAIRS):
            base = u * 2 * RET_HEADS
            for h in hs:
                y_ref[rows[u], h * RET_DV:(h + 1) * RET_DV] = _slab_merge(
                    ys[base + h], ys[base + RET_HEADS + h]).astype(BF16)
        return carry

    lax.fori_loop(0, npair // SAMPLE_PAIRS, pairs, 0)


def _ret_sample(p_main, cos, sin, state, so_prev, y_full, layer, row0, nseq, lseg):
    qw = RET_HEADS * RET_DK
    vw = RET_HEADS * RET_DV
    rb = SAMPLE_SEQS * lseg
    blk0 = row0 // rb
    st_in, st_specs, st_out_spec, out_shapes, aliases = _sample_io(state, so_prev, y_full, layer, RET_HEADS,
                                                                   RET_DK, RET_DV, 6)
    return pl.pallas_call(
        functools.partial(_ret_sample_kernel, lseg=lseg),
        grid=(nseq // SAMPLE_SEQS,),
        in_specs=[
            pl.BlockSpec((rb, qw), lambda i: (blk0 + i, P_RQ // qw)),
            pl.BlockSpec((rb, qw), lambda i: (blk0 + i, P_RK // qw)),
            pl.BlockSpec((rb, vw), lambda i: (blk0 + i, P_RV // vw)),
            pl.BlockSpec((rb, vw), lambda i: (blk0 + i, P_RG // vw)),
            pl.BlockSpec((SAMPLE_SLAB, LANES), lambda i: (0, 0)),
            pl.BlockSpec((SAMPLE_SLAB, LANES), lambda i: (0, 0)),
        ] + st_specs,
        out_specs=[pl.BlockSpec((rb, vw), lambda i: (blk0 + i, 0)), st_out_spec],
        out_shape=out_shapes,
        input_output_aliases=aliases,
        compiler_params=_cparams(("arbitrary",)),
        name="ret_sample",
    )(p_main, p_main, p_main, p_main, cos, sin, *st_in)


def _unit_lower_inverse_all(a_list, nvalid):
    c = a_list[0].shape[0]
    n = range(len(a_list))
    i = _iota((c, c), 0)
    j = _iota((c, c), 1)
    eye = (i == j).astype(F32)
    pair = (i >> 1) == (j >> 1)
    ts = [eye - jnp.where(pair, a, 0.0) for a in a_list]
    blk = 2
    while blk < min(c, nvalid):
        sh = blk.bit_length()
        m = ((i >> sh) == (j >> sh)) & ((i & blk) != 0) & ((j & blk) == 0)
        xs = [jnp.where(m, a, 0.0).astype(BF16) for a in a_list]
        tb = [t.astype(BF16) for t in ts]
        tx = [_mm(tb[p], xs[p]).astype(BF16) for p in n]
        txt = [_mm(tx[p], tb[p]) for p in n]
        ts = [ts[p] - txt[p] for p in n]
        blk *= 2
    return ts


def _gdn_gates(small, small_t, alog_r, dtb_r, alog_c, dtb_c, c):
    r = small.shape[0]
    g = -jnp.exp(alog_r) * jax.nn.softplus(small + dtb_r)
    g_t = -jnp.exp(alog_c) * jax.nn.softplus(small_t + dtb_c)
    i = _iota((r, r), 0)
    j = _iota((r, r), 1)
    sh = c.bit_length() - 1
    same = ((i >> sh) == (j >> sh)) if r > c else True
    cum = _dot_mask_l((same & (j <= i)).astype(F32), g)
    cum_t = _dot_mask_r(g_t, (same & (i <= j)).astype(F32))
    beta = jax.nn.sigmoid(small)
    return cum, cum_t, beta


def _gdn_prep(qs, ks, vs, cum_cs, cum_rs, beta_cs, nvalid):
    n = range(len(qs))
    c = qs[0].shape[0]
    i = _iota((c, c), 0)
    j = _iota((c, c), 1)
    rowv = _iota((c, 1), 0) < nvalid
    qn = [q * (lax.rsqrt(jnp.sum(q * q, axis=-1, keepdims=True) + EPS) * (GDN_DK ** -0.5)) for q in qs]
    kn = [k * lax.rsqrt(jnp.sum(k * k, axis=-1, keepdims=True) + EPS) for k in ks]
    dec = [jnp.exp(jnp.where(j <= i, cum_cs[p] - cum_rs[p], -jnp.inf)) for p in n]
    e_c = [jnp.exp(cum_cs[p]) for p in n]
    kb = [kn[p] * beta_cs[p] for p in n]
    kk = [_dot_nt(kb[p], kn[p]) for p in n]
    qk = [_dot_nt(qn[p], kn[p]) for p in n]
    lower = [jnp.where(j < i, kk[p] * dec[p], 0.0) for p in n]
    ts = _unit_lower_inverse_all(lower, nvalid)
    rhs = [jnp.concatenate([vs[p] * beta_cs[p], kb[p] * e_c[p]], axis=1) for p in n]
    sol = [_dot(ts[p], rhs[p]) for p in n]
    cum_last = [cum_cs[p][nvalid - 1:nvalid, :] for p in n]
    out = []
    for p in n:
        wq = jnp.concatenate([sol[p][:, GDN_DV:], qn[p] * e_c[p]], axis=0).astype(BF16)
        kd = (kn[p] * jnp.where(rowv, jnp.exp(cum_last[p] - cum_cs[p]), 0.0)).astype(BF16)
        out.append((wq, sol[p][:, :GDN_DV], (qk[p] * dec[p]).astype(BF16), kd, jnp.exp(cum_last[p])))
    return out


def _gdn_seq(prep, zs, ss, norm_g):
    n = range(len(prep))
    c = prep[0][1].shape[0]
    st = [_mm(prep[p][0], ss[p].astype(BF16)) for p in n]
    v_new = [prep[p][1] - st[p][:c] for p in n]
    o = [st[p][c:] + _dot(prep[p][2], v_new[p]) for p in n]
    kv = [_dot_tn(prep[p][3], v_new[p]) for p in n]
    s_new = [prep[p][4] * ss[p] + kv[p] for p in n]
    ys = [_rms(o[p]) * norm_g * _silu(zs[p]) for p in n]
    return ys, s_new


def _gdn_conv(xcat, cw):
    acc = xcat[SUBLANES:, :] * cw[GDN_CONV - 1:GDN_CONV, :]
    for w in range(GDN_CONV - 1):
        sh = GDN_CONV - 1 - w
        acc = acc + pltpu.roll(xcat, sh, axis=0)[SUBLANES:, :] * cw[w:w + 1, :]
    return _silu(acc)


def _gdn_split(conv, h):
    hk = GDN_HEADS * GDN_DK
    return (conv[:, h * GDN_DK:(h + 1) * GDN_DK], conv[:, hk + h * GDN_DK: hk + (h + 1) * GDN_DK],
            conv[:, 2 * hk + h * GDN_DV: 2 * hk + (h + 1) * GDN_DV])


def _gdn_prompt_kernel(x_ref, z_ref, sm_ref, smt_ref, cw_ref, alr_ref, dtr_ref, alc_ref, dtc_ref, ng_ref,
                       y_ref, so_ref, s_ref, halo_ref):
    c = pl.program_id(1)

    @pl.when(c == 0)
    def _():
        s_ref[...] = jnp.zeros_like(s_ref)
        halo_ref[...] = jnp.zeros_like(halo_ref)

    x = x_ref[...]
    rows = x.shape[0]
    xcat = jnp.concatenate([halo_ref[...], x], axis=0)
    halo_ref[...] = x[rows - SUBLANES:, :]
    conv = _gdn_conv(xcat, cw_ref[...])
    cum, cum_t, beta = _gdn_gates(sm_ref[...], smt_ref[...], alr_ref[...], dtr_ref[...],
                                  alc_ref[...], dtc_ref[...], CHUNK)
    hs = range(GDN_HEADS)
    order = [(ch, h) for ch in range(rows // CHUNK) for h in hs]

    def rws(a, ch):
        return a[ch * CHUNK:(ch + 1) * CHUNK]

    qkv = [_gdn_split(rws(conv, ch), h) for ch, h in order]
    prep = _gdn_prep([t[0] for t in qkv], [t[1] for t in qkv], [t[2] for t in qkv],
                     [rws(cum, ch)[:, S_DA + h:S_DA + h + 1] for ch, h in order],
                     [cum_t[S_DA + h:S_DA + h + 1, ch * CHUNK:(ch + 1) * CHUNK] for ch, h in order],
                     [rws(beta, ch)[:, S_DB + h:S_DB + h + 1] for ch, h in order], CHUNK)
    ss = [s_ref[h] for h in hs]
    for ch in range(rows // CHUNK):
        ys, ss = _gdn_seq(prep[ch * GDN_HEADS:(ch + 1) * GDN_HEADS],
                          [z_ref[ch * CHUNK:(ch + 1) * CHUNK, h * GDN_DV:(h + 1) * GDN_DV] for h in hs],
                          ss, ng_ref[...])
        for h in hs:
            y_ref[ch * CHUNK:(ch + 1) * CHUNK, h * GDN_DV:(h + 1) * GDN_DV] = ys[h].astype(BF16)
    for h in hs:
        s_ref[h] = ss[h]

    @pl.when(c == pl.num_programs(1) - 1)
    def _():
        so_ref[...] = s_ref[...]


def _gdn_prompt(p_main, p_small, p_small_t, cw, alr, dtr, alc, dtc, ng, layer, nb, nc):
    zw = GDN_HEADS * GDN_DV
    rb = MIX_CHUNKS * CHUNK
    return pl.pallas_call(
        _gdn_prompt_kernel,
        grid=(nb, nc),
        in_specs=[
            pl.BlockSpec((rb, GDN_QKV), lambda b, c: (b * nc + c, P_DQKV // GDN_QKV)),
            pl.BlockSpec((rb, zw), lambda b, c: (b * nc + c, P_DZ // zw)),
            pl.BlockSpec((rb, LANES), lambda b, c: (b * nc + c, 0)),
            pl.BlockSpec((None, LANES, rb), lambda b, c: (b * nc + c, 0, 0)),
            pl.BlockSpec((None, GDN_CONV, GDN_QKV), lambda b, c: (layer, 0, 0)),
            pl.BlockSpec((None, 1, LANES), lambda b, c: (layer, 0, 0)),
            pl.BlockSpec((None, 1, LANES), lambda b, c: (layer, 0, 0)),
            pl.BlockSpec((None, LANES, 1), lambda b, c: (layer, 0, 0)),
            pl.BlockSpec((None, LANES, 1), lambda b, c: (layer, 0, 0)),
            pl.BlockSpec((None, 1, GDN_DV), lambda b, c: (layer, 0, 0)),
        ],
        out_specs=[
            pl.BlockSpec((rb, zw), lambda b, c: (b * nc + c, 0)),
            pl.BlockSpec((None, GDN_HEADS, GDN_DK, GDN_DV), lambda b, c: (b, 0, 0, 0)),
        ],
        out_shape=[jax.ShapeDtypeStruct((p_main.shape[0], zw), BF16),
                   jax.ShapeDtypeStruct((nb, GDN_HEADS, GDN_DK, GDN_DV), F32)],
        scratch_shapes=[pltpu.VMEM((GDN_HEADS, GDN_DK, GDN_DV), F32), pltpu.VMEM((SUBLANES, GDN_QKV), F32)],
        compiler_params=_cparams(("arbitrary", "arbitrary")),
        name="gdn_prompt",
    )(p_main, p_main, p_small, p_small_t, cw, alr, dtr, alc, dtc, ng)


def _gdn_sample_kernel(x_ref, z_ref, sm_ref, smt_ref, sc_ref, cw_ref, alr_ref, dtr_ref, alc_ref, dtc_ref, ng_ref,
                       si_ref, *rest, lseg):
    y_ref, so_ref = rest[-2:]
    npair = x_ref.shape[0] // SAMPLE_SLAB
    cw = cw_ref[...]
    hs = range(GDN_HEADS)

    seqs = [(u, t) for u in range(SAMPLE_PAIRS) for t in range(2)]
    order = [(u, t, h) for u, t in seqs for h in hs]

    def pairs(it, carry):
        ps = [it * SAMPLE_PAIRS + u for u in range(SAMPLE_PAIRS)]
        rows = [pl.ds(pl.multiple_of(p * SAMPLE_SLAB, SAMPLE_SLAB), SAMPLE_SLAB) for p in ps]
        xs = [_slab_pair(x_ref[r, :]) for r in rows]
        sms = [_slab_pair(sm_ref[r, :]) for r in rows]
        zs = [[_slab_pair(z_ref[r, h * GDN_DV:(h + 1) * GDN_DV]) for h in hs] for r in rows]
        convs, gates = {}, {}
        for u, t in seqs:
            xcat = jnp.concatenate([sc_ref[2 * ps[u] + t], xs[u][t]], axis=0)
            convs[u, t] = _gdn_conv(xcat, cw)
            gates[u, t] = _gdn_gates(sms[u][t], smt_ref[2 * ps[u] + t], alr_ref[...], dtr_ref[...],
                                     alc_ref[...], dtc_ref[...], SAMPLE_SLAB)
        qkv = [_gdn_split(convs[u, t], h) for u, t, h in order]
        prep = _gdn_prep([a[0] for a in qkv], [a[1] for a in qkv], [a[2] for a in qkv],
                         [gates[u, t][0][:, S_DA + h:S_DA + h + 1] for u, t, h in order],
                         [gates[u, t][1][S_DA + h:S_DA + h + 1, :] for u, t, h in order],
                         [gates[u, t][2][:, S_DB + h:S_DB + h + 1] for u, t, h in order], lseg)
        ys, s_new = _gdn_seq(prep, [zs[u][h][t] for u, t, h in order],
                             [si_ref[2 * ps[u] + t, h] for u, t, h in order], ng_ref[...])
        for idx, (u, t, h) in enumerate(order):
            so_ref[2 * ps[u] + t, h] = s_new[idx]
        for u in range(SAMPLE_PAIRS):
            base = u * 2 * GDN_HEADS
            for h in hs:
                y_ref[rows[u], h * GDN_DV:(h + 1) * GDN_DV] = _slab_merge(
                    ys[base + h], ys[base + GDN_HEADS + h]).astype(BF16)
        return carry

    lax.fori_loop(0, npair // SAMPLE_PAIRS, pairs, 0)


def _gdn_sample(p_main, p_small, p_small_t, sconv, cw, alr, dtr, alc, dtc, ng, state, so_prev, y_full, layer, row0,
                nseq, lseg):
    zw = GDN_HEADS * GDN_DV
    rb = SAMPLE_SEQS * lseg
    blk0 = row0 // rb
    st_in, st_specs, st_out_spec, out_shapes, aliases = _sample_io(state, so_prev, y_full, layer, GDN_HEADS,
                                                                   GDN_DK, GDN_DV, 11)
    return pl.pallas_call(
        functools.partial(_gdn_sample_kernel, lseg=lseg),
        grid=(nseq // SAMPLE_SEQS,),
        in_specs=[
            pl.BlockSpec((rb, GDN_QKV), lambda i: (blk0 + i, P_DQKV // GDN_QKV)),
            pl.BlockSpec((rb, zw), lambda i: (blk0 + i, P_DZ // zw)),
            pl.BlockSpec((rb, LANES), lambda i: (blk0 + i, 0)),
            pl.BlockSpec((SAMPLE_SEQS, LANES, SAMPLE_SLAB), lambda i: (i, 0, 0)),
            pl.BlockSpec((SAMPLE_SEQS, SUBLANES, GDN_QKV), lambda i: (i, 0, 0)),
            pl.BlockSpec((None, GDN_CONV, GDN_QKV), lambda i: (layer, 0, 0)),
            pl.BlockSpec((None, 1, LANES), lambda i: (layer, 0, 0)),
            pl.BlockSpec((None, 1, LANES), lambda i: (layer, 0, 0)),
            pl.BlockSpec((None, LANES, 1), lambda i: (layer, 0, 0)),
            pl.BlockSpec((None, LANES, 1), lambda i: (layer, 0, 0)),
            pl.BlockSpec((None, 1, GDN_DV), lambda i: (layer, 0, 0)),
        ] + st_specs,
        out_specs=[pl.BlockSpec((rb, zw), lambda i: (blk0 + i, 0)), st_out_spec],
        out_shape=out_shapes,
        input_output_aliases=aliases,
        compiler_params=_cparams(("arbitrary",)),
        name="gdn_sample",
    )(p_main, p_main, p_small, p_small_t, sconv, cw, alr, dtr, alc, dtc, ng, *st_in)


def _gla_scores(q, k, b, sub):
    c = q.shape[0]
    irow = _iota((sub, 1), 0)
    lane = _iota((sub, c), 1)
    blocks = []
    for blk in range(c // sub):
        r0 = blk * sub
        qi, ki, bi = q[r0:r0 + sub], k[r0:r0 + sub], b[r0:r0 + sub]
        if blk > 0:
            b0 = b[r0 - 1:r0, :]
            qt = qi * jnp.exp(bi - b0)
            kt = k[:r0] * jnp.exp(b0 - b[:r0])
            kt = jnp.concatenate([kt, jnp.zeros((c - r0, kt.shape[1]), F32)], axis=0)
            a = _dot_nt(qt, kt)
        else:
            a = jnp.zeros((sub, c), F32)
        for jj in range(sub):
            e = jnp.exp(jnp.where(irow >= jj, bi - bi[jj:jj + 1, :], -jnp.inf))
            col = jnp.sum(qi * e * ki[jj:jj + 1, :], axis=1, keepdims=True)
            a = jnp.where(lane == r0 + jj, col, a)
        blocks.append(a)
    return jnp.concatenate(blocks, axis=0) if len(blocks) > 1 else blocks[0]


def _gla_prep(qs, ks, vs, bs, nvalid, sub):
    n = range(len(qs))
    c = qs[0].shape[0]
    rowv = _iota((c, 1), 0) < nvalid
    qsc = [q * (GLA_DK ** -0.5) for q in qs]
    b_last = [bs[p][nvalid - 1:nvalid, :] for p in n]
    kv = [_dot_tn(ks[p] * jnp.where(rowv, jnp.exp(b_last[p] - bs[p]), 0.0), vs[p]) for p in n]
    a = [_gla_scores(qsc[p], ks[p], bs[p], sub) for p in n]
    o_in = [_dot(a[p], vs[p]) for p in n]
    return [((qsc[p] * jnp.exp(bs[p])).astype(BF16), o_in[p], kv[p], _row_to_col(jnp.exp(b_last[p]))) for p in n]


def _gla_seq(prep, zs, ss, norm_g):
    n = range(len(prep))
    o = [prep[p][1] + _mm(prep[p][0], ss[p].astype(BF16)) for p in n]
    s_new = [prep[p][3] * ss[p] + prep[p][2] for p in n]
    ys = [_rms(o[p]) * norm_g * _silu(zs[p]) for p in n]
    return ys, s_new


def _gla_cumsum(gk, c):
    r = gk.shape[0]
    i = _iota((r, r), 0)
    j = _iota((r, r), 1)
    sh = c.bit_length() - 1
    same = ((i >> sh) == (j >> sh)) if r > c else True
    return _dot_mask_l((same & (j <= i)).astype(F32), gk)


def _gla_gk(small, wup, bup):
    return jax.nn.log_sigmoid(_mm(small.astype(BF16), wup) + bup) / GLA_GATE_NORM


def _gla_prompt_kernel(q_ref, k_ref, v_ref, z_ref, sm_ref, wup_ref, bup_ref, ng_ref, y_ref, so_ref, s_ref):
    c = pl.program_id(1)

    @pl.when(c == 0)
    def _():
        s_ref[...] = jnp.zeros_like(s_ref)

    rows = q_ref.shape[0]
    b_all = _gla_cumsum(_gla_gk(sm_ref[...], wup_ref[...], bup_ref[...]), CHUNK)
    hs = range(GLA_HEADS)
    order = [(ch, h) for ch in range(rows // CHUNK) for h in hs]

    def blk(ref, ch, h, w):
        return ref[ch * CHUNK:(ch + 1) * CHUNK, h * w:(h + 1) * w]

    prep = _gla_prep([blk(q_ref, ch, h, GLA_DK) for ch, h in order], [blk(k_ref, ch, h, GLA_DK) for ch, h in order],
                     [blk(v_ref, ch, h, GLA_DV) for ch, h in order], [blk(b_all, ch, h, GLA_DK) for ch, h in order],
                     CHUNK, GLA_SUB)
    ss = [s_ref[h] for h in hs]
    for ch in range(rows // CHUNK):
        ys, ss = _gla_seq(prep[ch * GLA_HEADS:(ch + 1) * GLA_HEADS], [blk(z_ref, ch, h, GLA_DV) for h in hs],
                          ss, ng_ref[...])
        for h in hs:
            y_ref[ch * CHUNK:(ch + 1) * CHUNK, h * GLA_DV:(h + 1) * GLA_DV] = ys[h].astype(BF16)
    for h in hs:
        s_ref[h] = ss[h]

    @pl.when(c == pl.num_programs(1) - 1)
    def _():
        so_ref[...] = s_ref[...]


def _gla_prompt(p_main, p_small, wup, bup, ng, layer, nb, nc):
    qw = GLA_HEADS * GLA_DK
    vw = GLA_HEADS * GLA_DV
    rb = MIX_CHUNKS * CHUNK
    return pl.pallas_call(
        _gla_prompt_kernel,
        grid=(nb, nc),
        in_specs=[
            pl.BlockSpec((rb, qw), lambda b, c: (b * nc + c, P_LQ // qw)),
            pl.BlockSpec((rb, qw), lambda b, c: (b * nc + c, P_LK // qw)),
            pl.BlockSpec((rb, vw), lambda b, c: (b * nc + c, P_LV // vw)),
            pl.BlockSpec((rb, vw), lambda b, c: (b * nc + c, P_LGT // vw)),
            pl.BlockSpec((rb, LANES), lambda b, c: (b * nc + c, 0)),
            pl.BlockSpec((None, LANES, qw), lambda b, c: (layer, 0, 0)),
            pl.BlockSpec((None, 1, qw), lambda b, c: (layer, 0, 0)),
            pl.BlockSpec((None, 1, GLA_DV), lambda b, c: (layer, 0, 0)),
        ],
        out_specs=[
            pl.BlockSpec((rb, vw), lambda b, c: (b * nc + c, 0)),
            pl.BlockSpec((None, GLA_HEADS, GLA_DK, GLA_DV), lambda b, c: (b, 0, 0, 0)),
        ],
        out_shape=[jax.ShapeDtypeStruct((p_main.shape[0], vw), BF16),
                   jax.ShapeDtypeStruct((nb, GLA_HEADS, GLA_DK, GLA_DV), F32)],
        scratch_shapes=[pltpu.VMEM((GLA_HEADS, GLA_DK, GLA_DV), F32)],
        compiler_params=_cparams(("arbitrary", "arbitrary")),
        name="gla_prompt",
    )(p_main, p_main, p_main, p_main, p_small, wup, bup, ng)


def _gla_sample_kernel(q_ref, k_ref, v_ref, z_ref, sm_ref, wup_ref, bup_ref, ng_ref, si_ref, *rest, lseg):
    y_ref, so_ref = rest[-2:]
    npair = q_ref.shape[0] // SAMPLE_SLAB
    hs = range(GLA_HEADS)

    order = [(u, t, h) for u in range(SAMPLE_PAIRS) for t in range(2) for h in hs]

    def pairs(it, carry):
        ps = [it * SAMPLE_PAIRS + u for u in range(SAMPLE_PAIRS)]
        rows = [pl.ds(pl.multiple_of(p * SAMPLE_SLAB, SAMPLE_SLAB), SAMPLE_SLAB) for p in ps]

        def slabs(ref, w):
            return [[_slab_pair(ref[r, h * w:(h + 1) * w]) for h in hs] for r in rows]

        qs, ks, vs, zs = slabs(q_ref, GLA_DK), slabs(k_ref, GLA_DK), slabs(v_ref, GLA_DV), slabs(z_ref, GLA_DV)
        bs = [[_gla_cumsum(g, SAMPLE_SLAB) for g in _slab_pair(_gla_gk(sm_ref[r, :], wup_ref[...], bup_ref[...]))]
              for r in rows]
        prep = _gla_prep([qs[u][h][t] for u, t, h in order], [ks[u][h][t] for u, t, h in order],
                         [vs[u][h][t] for u, t, h in order],
                         [bs[u][t][:, h * GLA_DK:(h + 1) * GLA_DK] for u, t, h in order], lseg, SAMPLE_SLAB)
        ys, s_new = _gla_seq(prep, [zs[u][h][t] for u, t, h in order],
                             [si_ref[2 * ps[u] + t, h] for u, t, h in order], ng_ref[...])
        for idx, (u, t, h) in enumerate(order):
            so_ref[2 * ps[u] + t, h] = s_new[idx]
        for u in range(SAMPLE_PAIRS):
            base = u * 2 * GLA_HEADS
            for h in hs:
                y_ref[rows[u], h * GLA_DV:(h + 1) * GLA_DV] = _slab_merge(
                    ys[base + h], ys[base + GLA_HEADS + h]).astype(BF16)
        return carry

    lax.fori_loop(0, npair // SAMPLE_PAIRS, pairs, 0)


def _gla_sample(p_main, p_small, wup, bup, ng, state, so_prev, y_full, layer, row0, nseq, lseg):
    qw = GLA_HEADS * GLA_DK
    vw = GLA_HEADS * GLA_DV
    rb = SAMPLE_SEQS * lseg
    blk0 = row0 // rb
    st_in, st_specs, st_out_spec, out_shapes, aliases = _sample_io(state, so_prev, y_full, layer, GLA_HEADS,
                                                                   GLA_DK, GLA_DV, 8)
    return pl.pallas_call(
        functools.partial(_gla_sample_kernel, lseg=lseg),
        grid=(nseq // SAMPLE_SEQS,),
        in_specs=[
            pl.BlockSpec((rb, qw), lambda i: (blk0 + i, P_LQ // qw)),
            pl.BlockSpec((rb, qw), lambda i: (blk0 + i, P_LK // qw)),
            pl.BlockSpec((rb, vw), lambda i: (blk0 + i, P_LV // vw)),
            pl.BlockSpec((rb, vw), lambda i: (blk0 + i, P_LGT // vw)),
            pl.BlockSpec((rb, LANES), lambda i: (blk0 + i, 0)),
            pl.BlockSpec((None, LANES, qw), lambda i: (layer, 0, 0)),
            pl.BlockSpec((None, 1, qw), lambda i: (layer, 0, 0)),
            pl.BlockSpec((None, 1, GLA_DV), lambda i: (layer, 0, 0)),
        ] + st_specs,
        out_specs=[pl.BlockSpec((rb, vw), lambda i: (blk0 + i, 0)), st_out_spec],
        out_shape=out_shapes,
        input_output_aliases=aliases,
        compiler_params=_cparams(("arbitrary",)),
        name="gla_sample",
    )(p_main, p_main, p_main, p_main, p_small, wup, bup, ng, *st_in)


def _rope_tables(pos0, length):
    inv = 1.0 / (ROPE_BASE ** jnp.linspace(0.0, 1.0, RET_DK // 2, dtype=F32))
    ang = (jnp.arange(length, dtype=F32) + pos0)[:, None] * inv[None, :]
    cos = jnp.repeat(jnp.cos(ang), 2, axis=1)
    sin = jnp.stack([-jnp.sin(ang), jnp.sin(ang)], axis=-1).reshape(length, RET_DK)
    return cos, sin


def _lane_pad(v, off):
    n = v.shape[-1]
    return jnp.pad(v, ((0, 0), (off, LANES - off - n)))[:, None, :]


def kernel(x_prompt, x_sample, state_ret, state_gdn, state_gdn_conv, state_gla, norm_mix, norm_ffn, norm_final,
           w_in, b_merge, gdn_conv_w, gdn_a_log, gdn_dt_bias, gdn_norm, gla_w_up, gla_b_up, gla_norm, w_branch,
           w_o, w_gate_up, w_down):
    nb, seq, d = x_prompt.shape
    nsq, lseg, _ = x_sample.shape
    depth = w_in.shape[0]
    n_p = nb * seq
    n_s = nsq * lseg
    mix_rows = MIX_CHUNKS * CHUNK
    nc = seq // mix_rows
    nc_ret = seq // RET_CHUNK
    tm = _dense_tiles(n_p + n_s)
    tm_out = math.gcd(n_p, n_s)
    assert seq % mix_rows == 0 and seq % RET_CHUNK == 0
    assert nsq % SAMPLE_SEQS == 0 and n_p % (SAMPLE_SEQS * lseg) == 0
    assert 2 * lseg == SAMPLE_SLAB and lseg >= GDN_CONV - 1

    w_t = jnp.swapaxes(w_in, 1, 2)
    w_main = _pack_w_in(w_t, 1024, 1024)
    w_small = _pack_w_small(w_t, 512)
    wb = w_branch.astype(BF16)
    wo = w_o.astype(BF16)
    wgu = w_gate_up.astype(BF16)
    wdn = w_down.astype(BF16)
    g_mix = norm_mix[:, None, :]
    g_ffn = norm_ffn[:, None, :]
    bm = b_merge.reshape(depth, N_BRANCH, 1, D_MODEL)
    alr = _lane_pad(gdn_a_log, S_DA)
    dtr = _lane_pad(gdn_dt_bias, S_DA)
    alc = jnp.swapaxes(alr, 1, 2)
    dtc = jnp.swapaxes(dtr, 1, 2)
    gdn_ng = gdn_norm[:, None, :]
    gla_ng = gla_norm[:, None, :]
    wup = jnp.pad(gla_w_up, ((0, 0), (S_LLR, LANES - S_LLR - GLA_LOWRANK), (0, 0))).astype(BF16)
    bup = gla_b_up[:, None, :]
    cos_p, sin_p = _rope_tables(0.0, seq)
    cos_s, sin_s = _rope_tables(float(PAST_LEN), lseg)
    cos_s = jnp.tile(cos_s, (SAMPLE_SLAB // lseg, 1))
    sin_s = jnp.tile(sin_s, (SAMPLE_SLAB // lseg, 1))

    x = jnp.concatenate([x_prompt.reshape(n_p, d), x_sample.reshape(n_s, d)], axis=0)

    outs = {k: [] for k in ("p_ret", "p_gdn", "p_conv", "p_gla", "s_conv")}
    s_ret = s_gdn = s_gla = None
    for layer in range(depth):
        p_main, p_gate, p_small = _in_proj(x, g_mix, w_main, w_small, layer, tm, 1024)
        pst_p = jnp.swapaxes(p_small[:n_p].reshape(nb * nc, mix_rows, LANES), 1, 2)
        pst_s = jnp.swapaxes(jnp.pad(p_small[n_p:].reshape(nsq, lseg, LANES),
                                     ((0, 0), (0, SAMPLE_SLAB - lseg), (0, 0))), 1, 2)

        y_ret, st = _ret_prompt(p_main, cos_p, sin_p, nb, nc_ret)
        outs["p_ret"].append(st)
        y_ret, s_ret = _ret_sample(p_main, cos_s, sin_s, state_ret, s_ret, y_ret, layer, n_p, nsq, lseg)

        y_gdn, st = _gdn_prompt(p_main, p_small, pst_p, gdn_conv_w, alr, dtr, alc, dtc, gdn_ng, layer, nb, nc)
        outs["p_gdn"].append(st)
        sconv = jnp.pad(state_gdn_conv[layer], ((0, 0), (SUBLANES - (GDN_CONV - 1), 0), (0, 0)))
        y_gdn, s_gdn = _gdn_sample(p_main, p_small, pst_s, sconv, gdn_conv_w, alr, dtr, alc, dtc, gdn_ng,
                                   state_gdn, s_gdn, y_gdn, layer, n_p, nsq, lseg)
        outs["p_conv"].append(jnp.stack([
            lax.slice(p_main, ((b + 1) * seq - (GDN_CONV - 1), P_DQKV), ((b + 1) * seq, P_DQKV + GDN_QKV))
            for b in range(nb)]))
        dq_s = lax.slice(p_main, (n_p, P_DQKV), (n_p + n_s, P_DQKV + GDN_QKV)).reshape(nsq, lseg, GDN_QKV)
        outs["s_conv"].append(jnp.concatenate([state_gdn_conv[layer], dq_s], axis=1)[:, -(GDN_CONV - 1):])

        y_gla, st = _gla_prompt(p_main, p_small, wup, bup, gla_ng, layer, nb, nc)
        outs["p_gla"].append(st)
        y_gla, s_gla = _gla_sample(p_main, p_small, wup, bup, gla_ng, state_gla, s_gla, y_gla, layer, n_p, nsq, lseg)

        mrg = _merge((y_ret, y_gdn, y_gla), wb, p_gate, bm, layer, tm, 512)
        x = _out_proj(mrg, wo, x, layer, tm, 1024)
        x = _ffn(x, g_ffn, wgu, wdn, layer, tm, 512)

    g_fin = norm_final[None, :]
    y_p = _final_norm(x, g_fin, 0, n_p, tm_out)
    y_s = _final_norm(x, g_fin, n_p, n_s, tm_out)
    st = {k: jnp.stack(v) for k, v in outs.items()}
    return (y_p.reshape(nb, seq, d), y_s.reshape(nsq, lseg, d),
            st["p_ret"], st["p_gdn"], st["p_conv"], st["p_gla"],
            s_ret, s_gdn, st["s_conv"], s_gla)
```
